```python
import math
import jax
import jax.numpy as jnp
from jax import lax
import numpy as np

D_MODEL = 1024
BATCH = 4
SEQ = 4096
DEPTH = 4

N_HEADS = 16
HEAD_DIM = D_MODEL // N_HEADS
D_FF = 4 * D_MODEL
N_MIXERS = 4
N_REPEATS = DEPTH // N_MIXERS
RMS_EPS = 1e-6
NEG_INF = -1e30
POS_BIG = 1e30
TINY = 1e-30

REL_BUCKETS = 32
REL_MAX_EXACT = REL_BUCKETS // 2
REL_MAX_DIST = 2048

BAND_BLOCK = 128
Q_BLOCK = 128
GATHER_CHUNK = 32

DILATED_PATTERNS = ((128, 1), (512, 4), (2048, 16))

NSA_KV_HEADS = 4
NSA_GROUP = N_HEADS // NSA_KV_HEADS
NSA_KV_DIM = NSA_KV_HEADS * HEAD_DIM
CMP_STRIDE = 16
CMP_BLOCK = 2 * CMP_STRIDE
CMP_HIDDEN = 256
SEL_BLOCK = 64
SEL_TOPK = 16
NSA_WINDOW = 512
NSA_IN_DIM = D_MODEL + 6 * NSA_KV_DIM + 3 * N_HEADS

FOX_IN_DIM = 3 * D_MODEL + N_HEADS

MOBA_BLOCK = 256
MOBA_TOPK = 3

kernel_name = "hybrid_dilated_nsa_fox_moba"


def rmsnorm(x, g):
    x32 = x.astype(jnp.float32)
    y = x32 * lax.rsqrt(jnp.mean(x32 * x32, axis=-1, keepdims=True) + RMS_EPS)
    return y.astype(x.dtype) * g


def rel_bucket(dist):
    d = jnp.maximum(dist, 0)
    df = jnp.maximum(d.astype(jnp.float32), 1.0)
    large = REL_MAX_EXACT + (jnp.log(df / REL_MAX_EXACT) / math.log(REL_MAX_DIST / REL_MAX_EXACT)
                             * (REL_BUCKETS - REL_MAX_EXACT)).astype(jnp.int32)
    large = jnp.minimum(large, REL_BUCKETS - 1)
    return jnp.where(d < REL_MAX_EXACT, d, large)


def masked_stats(logits, mask):
    l = jnp.where(mask, logits, NEG_INF)
    m = jnp.max(l, axis=-1, keepdims=True)
    p = jnp.where(mask, jnp.exp(l - m), 0.0)
    s = jnp.sum(p, axis=-1, keepdims=True)
    return p, m, s


def combine_by_denominators(parts):
    m_all = parts[0][1]
    for _, m, _ in parts[1:]:
        m_all = jnp.maximum(m_all, m)
    w = [s * jnp.exp(m - m_all) for _, m, s in parts]
    num = sum(wi * o for wi, (o, _, _) in zip(w, parts))
    return num / jnp.maximum(sum(w), TINY)


def split_chunks(t, axis, size):
    shp = t.shape
    t = t.reshape(shp[:axis] + (shp[axis] // size, size) + shp[axis + 1:])
    return jnp.moveaxis(t, axis, 0)


def merge_chunks(t, axis):
    t = jnp.moveaxis(t, 0, axis)
    shp = t.shape
    return t.reshape(shp[:axis] + (shp[axis] * shp[axis + 1],) + shp[axis + 2:])


def banded_attention(q, k, v, bias_tab, max_dist, dilation):
    b, kh, g, L, dh = q.shape
    blk = BAND_BLOCK
    n_prev = -(-max_dist // blk)
    nb = -(-L // blk)
    pad = nb * blk - L
    qb = jnp.pad(q, ((0, 0), (0, 0), (0, 0), (0, pad), (0, 0))).reshape(b, kh, g, nb, blk, dh)

    def windows(t):
        tp = jnp.pad(t, ((0, 0), (0, 0), (n_prev * blk, pad), (0, 0))).reshape(b, kh, n_prev + nb, blk, dh)
        return jnp.concatenate([tp[:, :, i:i + nb] for i in range(n_prev + 1)], axis=3)

    kw, vw = windows(k), windows(v)
    nk = (n_prev + 1) * blk
    qi = jnp.arange(blk)[:, None]
    kj = jnp.arange(nk)[None, :]
    rel = qi + n_prev * blk - kj
    kpos = (jnp.arange(nb) * blk)[:, None, None] - n_prev * blk + kj[None]
    mask = (rel >= 0) & (rel <= max_dist) & (kpos >= 0)
    bias = jnp.transpose(bias_tab[rel_bucket(rel * dilation)], (2, 3, 0, 1))[None, :, :, None]
    logits = jnp.einsum("bhgnqd,bhnkd->bhgnqk", qb, kw, preferred_element_type=jnp.float32) * (dh ** -0.5) + bias
    p, m, s = masked_stats(logits, mask)
    o = jnp.einsum("bhgnqk,bhnkd->bhgnqd", p.astype(v.dtype), vw,
                   preferred_element_type=jnp.float32) / jnp.maximum(s, TINY)

    def unblock(t):
        return t.reshape(b, kh, g, nb * blk, t.shape[-1])[:, :, :, :L]

    return unblock(o), unblock(m), unblock(s)


def dilated_mixer(h, w_in, q_gain, k_gain, rel_table):
    b, S, _ = h.shape
    qkv = (h @ w_in).reshape(b, S, 3, N_HEADS, HEAD_DIM)
    q = rmsnorm(qkv[:, :, 0], q_gain)
    k = rmsnorm(qkv[:, :, 1], k_gain)
    v = qkv[:, :, 2]
    parts = []
    for window, dil in DILATED_PATTERNS:
        n = S // dil

        def by_residue(t):
            return t.reshape(b, n, dil, N_HEADS, HEAD_DIM).transpose(0, 2, 3, 1, 4).reshape(b * dil, N_HEADS, n, HEAD_DIM)

        def back(t):
            c = t.shape[-1]
            return t.reshape(b, dil, N_HEADS, n, c).transpose(0, 3, 1, 2, 4).reshape(b, S, N_HEADS, c)

        o, m, s = banded_attention(by_residue(q)[:, :, None], by_residue(k), by_residue(v),
                                   rel_table[:, :, None], window // dil, dil)
        parts.append((back(o), back(m), back(s)))
    return combine_by_denominators(parts).reshape(b, S, D_MODEL)


def nsa_mixer(h, w_in, cmp_pos, cmp_w1, cmp_w2, q_gain, k_gain, rel_table):
    b, S, _ = h.shape
    KH, G, dh = NSA_KV_HEADS, NSA_GROUP, HEAD_DIM
    scale = dh ** -0.5
    proj = h @ w_in
    splits = np.cumsum([D_MODEL] + [NSA_KV_DIM] * 6).tolist()
    q, k_c, v_c, k_s, v_s, k_w, v_w, gate_logits = jnp.split(proj, splits, axis=-1)
    q = rmsnorm(q.reshape(b, S, KH, G, dh), q_gain).transpose(0, 2, 3, 1, 4)

    def heads(t):
        return t.reshape(b, S, KH, dh).transpose(0, 2, 1, 3)

    k_c, v_c, k_s, v_s, k_w, v_w = map(heads, (k_c, v_c, k_s, v_s, k_w, v_w))
    k_s, k_w = rmsnorm(k_s, k_gain), rmsnorm(k_w, k_gain)
    gates = jax.nn.sigmoid(gate_logits.astype(jnp.float32)).reshape(b, S, KH, G, 3).transpose(0, 2, 3, 1, 4)
    bias_tab = rel_table.reshape(REL_BUCKETS, KH, G)
    qpos = jnp.arange(S)

    n_c = S // CMP_STRIDE - 1

    def compress(t, pos, w1, w2):
        chunks = t.reshape(b, KH, S // CMP_STRIDE, CMP_STRIDE, dh)
        blocks = jnp.concatenate([chunks[:, :, :-1], chunks[:, :, 1:]], axis=3) + pos
        hid = jax.nn.gelu(blocks.reshape(b, KH, n_c, CMP_BLOCK * dh) @ w1)
        return hid @ w2

    k_cmp = rmsnorm(compress(k_c, cmp_pos[0], cmp_w1[0], cmp_w2[0]), k_gain)
    v_cmp = compress(v_c, cmp_pos[1], cmp_w1[1], cmp_w2[1])
    c_start = jnp.arange(n_c) * CMP_STRIDE
    dist_c = qpos[:, None] - (c_start + CMP_BLOCK - 1)[None, :]
    bias_c = jnp.transpose(bias_tab[rel_bucket(dist_c)], (2, 3, 0, 1))
    logits_c = jnp.einsum("bhgsd,bhnd->bhgsn", q, k_cmp, preferred_element_type=jnp.float32) * scale + bias_c
    p_c, _, s_c = masked_stats(logits_c, dist_c >= 0)
    probs_c = p_c / jnp.maximum(s_c, TINY)
    o_cmp = jnp.einsum("bhgsn,bhnd->bhgsd", probs_c.astype(v_cmp.dtype), v_cmp, preferred_element_type=jnp.float32)

    n_sel = S // SEL_BLOCK
    s_start = jnp.arange(n_sel) * SEL_BLOCK
    overlap = ((c_start[:, None] < s_start[None, :] + SEL_BLOCK)
               & (c_start[:, None] + CMP_BLOCK > s_start[None, :])).astype(jnp.float32)
    imp = jnp.einsum("bhgsn,nj->bhsj", probs_c, overlap)
    own = (qpos // SEL_BLOCK)[:, None]
    jj = jnp.arange(n_sel)[None, :]
    imp = jnp.where(jj == own, POS_BIG, jnp.where(jj < own, imp, NEG_INF))
    k_top = min(SEL_TOPK, n_sel)
    top_val, top_idx = lax.top_k(imp, k_top)
    top_valid = top_val > 0.5 * NEG_INF
    kb = k_s.reshape(b, KH, n_sel, SEL_BLOCK, dh)
    vb = v_s.reshape(b, KH, n_sel, SEL_BLOCK, dh)
    b_i = jnp.arange(b)[:, None, None, None]
    h_i = jnp.arange(KH)[None, :, None, None]
    g_i = jnp.arange(G)[None, None, :, None, None]
    tab = jnp.transpose(bias_tab, (1, 2, 0))
    C = GATHER_CHUNK

    def sel_chunk(args):
        qc, idx, valid, pos = args
        kg = kb[b_i, h_i, idx].reshape(b, KH, C, k_top * SEL_BLOCK, dh)
        vg = vb[b_i, h_i, idx].reshape(b, KH, C, k_top * SEL_BLOCK, dh)
        kpos = (idx[..., None] * SEL_BLOCK + jnp.arange(SEL_BLOCK)).reshape(b, KH, C, k_top * SEL_BLOCK)
        dist = pos[:, None] - kpos
        mask = (jnp.repeat(valid, SEL_BLOCK, axis=-1) & (dist >= 0))[:, :, None]
        bias = tab[h_i[:, :, None], g_i, rel_bucket(dist)[:, :, None]]
        logits = jnp.einsum("bhgqd,bhqkd->bhgqk", qc, kg, preferred_element_type=jnp.float32) * scale + bias
        p, _, s = masked_stats(logits, mask)
        return jnp.einsum("bhgqk,bhqkd->bhgqd", p.astype(vg.dtype), vg,
                          preferred_element_type=jnp.float32) / jnp.maximum(s, TINY)

    o_sel = merge_chunks(lax.map(sel_chunk, (split_chunks(q, 3, C), split_chunks(top_idx, 2, C),
                                             split_chunks(top_valid, 2, C), split_chunks(qpos, 0, C))), 3)

    o_win, _, _ = banded_attention(q, k_w, v_w, bias_tab, NSA_WINDOW - 1, 1)

    o = gates[..., 0:1] * o_cmp + gates[..., 1:2] * o_sel + gates[..., 2:3] * o_win
    return o.transpose(0, 3, 1, 2, 4).reshape(b, S, D_MODEL)


def fox_mixer(h, w_in, b_f, q_gain, k_gain):
    b, S, _ = h.shape
    scale = HEAD_DIM ** -0.5
    proj = h @ w_in
    qkv = proj[..., :3 * D_MODEL].reshape(b, S, 3, N_HEADS, HEAD_DIM)
    q = rmsnorm(qkv[:, :, 0], q_gain).transpose(0, 2, 1, 3)
    k = rmsnorm(qkv[:, :, 1], k_gain).transpose(0, 2, 1, 3)
    v = qkv[:, :, 2].transpose(0, 2, 1, 3)
    log_f = jax.nn.log_sigmoid((proj[..., 3 * D_MODEL:] + b_f).astype(jnp.float32))
    c = jnp.cumsum(log_f, axis=1).transpose(0, 2, 1)
    kpos = jnp.arange(S)

    def q_block(args):
        qb, cb, pos = args
        logits = (jnp.einsum("bhqd,bhkd->bhqk", qb, k, preferred_element_type=jnp.float32) * scale
                  + (cb[..., :, None] - c[:, :, None, :]))
        p, _, s = masked_stats(logits, pos[:, None] >= kpos[None, :])
        return jnp.einsum("bhqk,bhkd->bhqd", p.astype(v.dtype), v,
                          preferred_element_type=jnp.float32) / jnp.maximum(s, TINY)

    o = merge_chunks(lax.map(q_block, (split_chunks(q, 2, Q_BLOCK), split_chunks(c, 2, Q_BLOCK),
                                       split_chunks(kpos, 0, Q_BLOCK))), 2)
    return o.transpose(0, 2, 1, 3).reshape(b, S, D_MODEL)


def moba_mixer(h, w_in, q_gain, k_gain, rel_table):
    b, S, _ = h.shape
    H, dh = N_HEADS, HEAD_DIM
    scale = dh ** -0.5
    qkv = (h @ w_in).reshape(b, S, 3, H, dh)
    q = rmsnorm(qkv[:, :, 0], q_gain).transpose(0, 2, 1, 3)
    k = rmsnorm(qkv[:, :, 1], k_gain).transpose(0, 2, 1, 3)
    v = qkv[:, :, 2].transpose(0, 2, 1, 3)
    nblk = -(-S // MOBA_BLOCK)
    pad = nblk * MOBA_BLOCK - S

    def to_blocks(t):
        return jnp.pad(t, ((0, 0), (0, 0), (0, pad), (0, 0))).reshape(b, H, nblk, MOBA_BLOCK, dh)

    qb, kb, vb = to_blocks(q), to_blocks(k), to_blocks(v)

    i = jnp.arange(MOBA_BLOCK)
    rel = i[:, None] - i[None, :]
    bias_own = jnp.transpose(rel_table[rel_bucket(rel)], (2, 0, 1))
    logits = jnp.einsum("bhnqd,bhnkd->bhnqk", qb, kb, preferred_element_type=jnp.float32) * scale + bias_own[:, None]
    p, m, s = masked_stats(logits, rel >= 0)
    o_own = jnp.einsum("bhnqk,bhnkd->bhnqd", p.astype(vb.dtype), vb,
                       preferred_element_type=jnp.float32) / jnp.maximum(s, TINY)

    def unblock(t):
        return t.reshape(b, H, nblk * MOBA_BLOCK, t.shape[-1])[:, :, :S]

    parts = [(unblock(o_own), unblock(m), unblock(s))]

    k_top = min(MOBA_TOPK, nblk - 1)
    if k_top > 0:
        k_mean = jnp.mean(kb, axis=3)
        gate = jnp.einsum("bhsd,bhnd->bhsn", q, k_mean, preferred_element_type=jnp.float32)
        qpos = jnp.arange(S)
        past = jnp.arange(nblk)[None, :] < (qpos // MOBA_BLOCK)[:, None]
        top_val, top_idx = lax.top_k(jnp.where(past, gate, NEG_INF), k_top)
        top_valid = top_val > 0.5 * NEG_INF
        b_i = jnp.arange(b)[:, None, None, None]
        h_i = jnp.arange(H)[None, :, None, None]
        tab = rel_table.T
        C = GATHER_CHUNK

        def sel_chunk(args):
            qc, idx, valid, pos = args
            kg = kb[b_i, h_i, idx].reshape(b, H, C, k_top * MOBA_BLOCK, dh)
            vg = vb[b_i, h_i, idx].reshape(b, H, C, k_top * MOBA_BLOCK, dh)
            kpos = (idx[..., None] * MOBA_BLOCK + jnp.arange(MOBA_BLOCK)).reshape(b, H, C, k_top * MOBA_BLOCK)
            bias = tab[h_i, rel_bucket(pos[:, None] - kpos)]
            mask = jnp.repeat(valid, MOBA_BLOCK, axis=-1)
            lg = jnp.einsum("bhqd,bhqkd->bhqk", qc, kg, preferred_element_type=jnp.float32) * scale + bias
            pp, mm, ss = masked_stats(lg, mask)
            oo = jnp.einsum("bhqk,bhqkd->bhqd", pp.astype(vg.dtype), vg,
                            preferred_element_type=jnp.float32) / jnp.maximum(ss, TINY)
            return oo, mm, ss

        o_sel, m_sel, s_sel = lax.map(sel_chunk, (split_chunks(q, 2, C), split_chunks(top_idx, 2, C),
                                                  split_chunks(top_valid, 2, C), split_chunks(qpos, 0, C)))
        parts.append((merge_chunks(o_sel, 2), merge_chunks(m_sel, 2), merge_chunks(s_sel, 2)))

    o = combine_by_denominators(parts)
    return o.transpose(0, 2, 1, 3).reshape(b, S, D_MODEL)


def setup_inputs(seed: int = 0) -> dict:
    key = jax.random.key(seed)
    ks = jax.random.split(key, 17)
    f32 = jnp.float32
    nrm = lambda k, shape, scale: jax.random.normal(k, shape, f32) * scale
    return {
        "x": jax.random.normal(ks[0], (BATCH, SEQ, D_MODEL), f32),
        "rel_table": nrm(ks[1], (REL_BUCKETS, N_HEADS), 0.5),
        "attn_norm": 1.0 + nrm(ks[2], (DEPTH, D_MODEL), 0.02),
        "mlp_norm": 1.0 + nrm(ks[3], (DEPTH, D_MODEL), 0.02),
        "q_gain": 1.0 + nrm(ks[4], (DEPTH, HEAD_DIM), 0.02),
        "k_gain": 1.0 + nrm(ks[5], (DEPTH, HEAD_DIM), 0.02),
        "w_out": nrm(ks[6], (DEPTH, D_MODEL, D_MODEL), D_MODEL ** -0.5),
        "mlp_w_up": nrm(ks[7], (DEPTH, D_MODEL, D_FF), D_MODEL ** -0.5),
        "mlp_w_down": nrm(ks[8], (DEPTH, D_FF, D_MODEL), D_FF ** -0.5),
        "dsa_w_in": nrm(ks[9], (N_REPEATS, D_MODEL, 3 * D_MODEL), D_MODEL ** -0.5),
        "nsa_w_in": nrm(ks[10], (N_REPEATS, D_MODEL, NSA_IN_DIM), D_MODEL ** -0.5),
        "nsa_cmp_pos": nrm(ks[11], (N_REPEATS, 2, CMP_BLOCK, HEAD_DIM), 0.1),
        "nsa_cmp_w1": nrm(ks[12], (N_REPEATS, 2, CMP_BLOCK * HEAD_DIM, CMP_HIDDEN), (CMP_BLOCK * HEAD_DIM) ** -0.5),
        "nsa_cmp_w2": nrm(ks[13], (N_REPEATS, 2, CMP_HIDDEN, HEAD_DIM), CMP_HIDDEN ** -0.5),
        "fox_w_in": nrm(ks[14], (N_REPEATS, D_MODEL, FOX_IN_DIM), D_MODEL ** -0.5),
        "fox_b_f": jax.random.uniform(ks[15], (N_REPEATS, N_HEADS), f32, 1.0, 6.0),
        "moba_w_in": nrm(ks[16], (N_REPEATS, D_MODEL, 3 * D_MODEL), D_MODEL ** -0.5),
    }


def reference(x, rel_table, attn_norm, mlp_norm, q_gain, k_gain, w_out, mlp_w_up, mlp_w_down,
              dsa_w_in, nsa_w_in, nsa_cmp_pos, nsa_cmp_w1, nsa_cmp_w2, fox_w_in, fox_b_f, moba_w_in):
    for layer in range(DEPTH):
        kind = layer % N_MIXERS
        r = layer // N_MIXERS
        h = rmsnorm(x, attn_norm[layer])
        if kind == 0:
            mixed = dilated_mixer(h, dsa_w_in[r], q_gain[layer], k_gain[layer], rel_table)
        elif kind == 1:
            mixed = nsa_mixer(h, nsa_w_in[r], nsa_cmp_pos[r], nsa_cmp_w1[r], nsa_cmp_w2[r],
                              q_gain[layer], k_gain[layer], rel_table)
        elif kind == 2:
            mixed = fox_mixer(h, fox_w_in[r], fox_b_f[r], q_gain[layer], k_gain[layer])
        else:
            mixed = moba_mixer(h, moba_w_in[r], q_gain[layer], k_gain[layer], rel_table)
        x = x + mixed.astype(x.dtype) @ w_out[layer]
        h = rmsnorm(x, mlp_norm[layer])
        x = x + jnp.square(jax.nn.relu(h @ mlp_w_up[layer])) @ mlp_w_down[layer]
    return x
```

```python
import functools
import math

import jax
import jax.numpy as jnp
import numpy as np
from jax import lax
from jax.experimental import pallas as pl
from jax.experimental.pallas import tpu as pltpu

F32 = jnp.float32
BF16 = jnp.bfloat16

D_MODEL = 1024
N_HEADS = 16
HEAD_DIM = 64
D_FF = 4 * D_MODEL
RMS_EPS = 1e-6
MASK_NEG = -1e30
TINY = 1e-30
QK_SCALE = HEAD_DIM ** -0.5

REL_BUCKETS = 32
REL_MAX_EXACT = REL_BUCKETS // 2
REL_MAX_DIST = 2048

DILATED_PATTERNS = ((128, 1), (512, 4), (2048, 16))
BAND_BLOCK = 128

NSA_KV_HEADS = 4
NSA_GROUP = N_HEADS // NSA_KV_HEADS
CMP_STRIDE = 16
CMP_BLOCK = 2 * CMP_STRIDE
CMP_HIDDEN = 256
SEL_BLOCK = 64
SEL_TOPK = 16
NSA_WINDOW = 512

MOBA_BLOCK = 256
MOBA_TOPK = 3

LANES = 128
VMEM_BYTES_V7X = 64 * 1024 * 1024
VMEM_HEADROOM = 6 * 1024 * 1024

DENSE_TILE = 256
SEL_NEG = -(2.0 ** 100)

_NT = (((1,), (1,)), ((), ()))


def _vmem_limit(block_bytes, temp_bytes):
    need = 2 * int(sum(block_bytes)) + int(temp_bytes) + (2 << 20)
    return int(min(need, VMEM_BYTES_V7X - VMEM_HEADROOM))


def _split3(v):
    a = v.astype(BF16)
    r = v - a.astype(F32)
    b = r.astype(BF16)
    c = (r - b.astype(F32)).astype(BF16)
    return a, b, c


def _proj_kernel(*refs, has_aux):
    if has_aux:
        x_ref, g_ref, w_ref, cg_ref, cf_ref, bd_ref, wa_ref, o_ref, oa_ref, h_scr = refs
    else:
        x_ref, g_ref, w_ref, cg_ref, cf_ref, bd_ref, o_ref, h_scr = refs

    @pl.when(pl.program_id(1) == 0)
    def _():
        x = x_ref[...]
        ms = jnp.mean(x * x, axis=-1, keepdims=True)
        h_scr[...] = (x * lax.rsqrt(ms + RMS_EPS) * g_ref[...]).astype(BF16)
        if has_aux:
            oa_ref[...] = jnp.dot(h_scr[...], wa_ref[...], preferred_element_type=F32)

    y = jnp.dot(h_scr[...], w_ref[...], preferred_element_type=F32)
    y2 = y * y
    hi = y2.astype(BF16)
    lo = (y2 - hi.astype(F32)).astype(BF16)
    ms = (jnp.dot(hi, bd_ref[...], preferred_element_type=F32)
          + jnp.dot(lo, bd_ref[...], preferred_element_type=F32))
    rs = lax.rsqrt(ms + RMS_EPS)
    out = (y * jnp.where(cf_ref[...] > 0.5, rs, 1.0) * cg_ref[...]).astype(o_ref.dtype)
    o_ref[0, 0] = out[:, :LANES]
    o_ref[0, 1] = out[:, LANES:]


def _proj(x2, norm_g, w_pad, col_gain, col_flag, w_aux, batch, seq):
    T = x2.shape[0]
    N = w_pad.shape[1]
    tm, tn = 1024, 2 * LANES
    spb = seq // tm
    bd = np.zeros((tn, tn), np.float32)
    for a in range(tn // LANES):
        bd[a * LANES:(a + 1) * LANES, a * LANES:(a + 1) * LANES] = 1.0 / HEAD_DIM
    bd = jnp.asarray(bd, BF16)
    has_aux = w_aux is not None
    in_specs = [
        pl.BlockSpec((tm, D_MODEL), lambda i, j: (i, 0)),
        pl.BlockSpec((1, D_MODEL), lambda i, j: (0, 0)),
        pl.BlockSpec((D_MODEL, tn), lambda i, j: (0, j)),
        pl.BlockSpec((1, tn), lambda i, j: (0, j)),
        pl.BlockSpec((1, tn), lambda i, j: (0, j)),
        pl.BlockSpec((tn, tn), lambda i, j: (0, 0)),
    ]
    args = [x2, norm_g.reshape(1, D_MODEL), w_pad, col_gain, col_flag, bd]
    out_shape = [jax.ShapeDtypeStruct((batch, N // LANES, seq, LANES), BF16)]
    out_specs = [pl.BlockSpec((1, 2, tm, LANES), lambda i, j: (i // spb, j, i % spb, 0))]
    if has_aux:
        in_specs.append(pl.BlockSpec((D_MODEL, LANES), lambda i, j: (0, 0)))
        args.append(w_aux)
        out_shape.append(jax.ShapeDtypeStruct((T, LANES), F32))
        out_specs.append(pl.BlockSpec((tm, LANES), lambda i, j: (i, 0)))
    blocks = [tm * D_MODEL * 4, D_MODEL * tn * 2, tm * tn * 2, tn * tn * 2, tm * LANES * 4, D_MODEL * LANES * 2]
    res = pl.pallas_call(
        functools.partial(_proj_kernel, has_aux=has_aux),
        grid=(T // tm, N // tn),
        in_specs=in_specs,
        out_specs=out_specs,
        out_shape=out_shape,
        scratch_shapes=[pltpu.VMEM((tm, D_MODEL), BF16)],
        compiler_params=pltpu.CompilerParams(
            dimension_semantics=("parallel", "arbitrary"),
            vmem_limit_bytes=_vmem_limit(blocks, tm * D_MODEL * 2 + 6 * tm * tn * 4)),
        name="proj",
    )(*args)
    return res if has_aux else res[0]


def _out_mlp_kernel(x_ref, m_ref, wo_ref, g_ref, wu_ref, wd_ref, o_ref, h_scr):
    @pl.when(pl.program_id(1) == 0)
    def _():
        x1 = x_ref[...]
        for h in range(N_HEADS):
            x1 = x1 + jnp.dot(m_ref[0, h], wo_ref[h], preferred_element_type=F32)
        o_ref[...] = x1
        ms = jnp.mean(x1 * x1, axis=-1, keepdims=True)
        h_scr[...] = (x1 * lax.rsqrt(ms + RMS_EPS) * g_ref[...]).astype(BF16)

    u = jnp.dot(h_scr[...], wu_ref[...], preferred_element_type=F32)
    a = jnp.square(jnp.maximum(u, 0.0)).astype(BF16)
    o_ref[...] += jnp.dot(a, wd_ref[...], preferred_element_type=F32)


def _out_mlp(x2, mixed, w_out_pad, norm_g, w_up, w_down, batch, seq):
    T = x2.shape[0]
    tm, tf = 1024, 512
    spb = seq // tm
    blocks = [tm * D_MODEL * 4, N_HEADS * tm * LANES * 2, N_HEADS * LANES * D_MODEL * 2,
              D_MODEL * tf * 2, tf * D_MODEL * 2, tm * D_MODEL * 4]
    return pl.pallas_call(
        _out_mlp_kernel,
        grid=(T // tm, D_FF // tf),
        in_specs=[
            pl.BlockSpec((tm, D_MODEL), lambda i, f: (i, 0)),
            pl.BlockSpec((1, N_HEADS, tm, LANES), lambda i, f: (i // spb, 0, i % spb, 0)),
            pl.BlockSpec((N_HEADS, LANES, D_MODEL), lambda i, f: (0, 0, 0)),
            pl.BlockSpec((1, D_MODEL), lambda i, f: (0, 0)),
            pl.BlockSpec((D_MODEL, tf), lambda i, f: (0, f)),
            pl.BlockSpec((tf, D_MODEL), lambda i, f: (f, 0)),
        ],
        out_specs=pl.BlockSpec((tm, D_MODEL), lambda i, f: (i, 0)),
        out_shape=jax.ShapeDtypeStruct((T, D_MODEL), F32),
        scratch_shapes=[pltpu.VMEM((tm, D_MODEL), BF16)],
        compiler_params=pltpu.CompilerParams(
            dimension_semantics=("parallel", "arbitrary"),
            vmem_limit_bytes=_vmem_limit(blocks, tm * D_MODEL * 2 + 3 * tm * tf * 4 + 2 * tm * D_MODEL * 4)),
        name="out_mlp",
    )(x2, mixed, w_out_pad, norm_g.reshape(1, D_MODEL), w_up, w_down)


def _dense_attn_kernel(*refs, tile, n_kv, group, n_qt, n_back, has_bias, n_bias, diag_causal, emit_stats):
    if has_bias:
        q_ref, k_ref, v_ref, b_ref, o_ref = refs
    else:
        q_ref, k_ref, v_ref, o_ref = refs
        b_ref = None
    chunk = pl.program_id(2)
    lane = lax.broadcasted_iota(jnp.int32, (tile, LANES), 1)
    if diag_causal:
        visible = (lax.broadcasted_iota(jnp.int32, (tile, tile), 0)
                   >= lax.broadcasted_iota(jnp.int32, (tile, tile), 1))

    def one_query_tile(idx, _):
        u = idx // (group * n_qt)
        g = (idx // n_qt) % group
        t = idx % n_qt
        hq = u * group + g
        qi = chunk * n_qt + t
        q = q_ref[0, hq, pl.ds(pl.multiple_of(t * tile, tile), tile), :]

        def tile_scores(d):
            start = pl.multiple_of((qi - d) * tile, tile)
            k = k_ref[0, u, pl.ds(start, tile), :]
            s = lax.dot_general(q, k, _NT, preferred_element_type=F32)
            if has_bias:
                s = s + b_ref[hq, jnp.minimum(d, n_bias - 1)]
            return s, start

        s, start = tile_scores(0)
        if diag_causal:
            s = jnp.where(visible, s, MASK_NEG)
        m = jnp.max(s, axis=-1, keepdims=True)
        p = jnp.exp(s - m)
        l = jnp.sum(p, axis=-1, keepdims=True)
        acc = jnp.dot(p.astype(BF16), v_ref[0, u, pl.ds(start, tile), :], preferred_element_type=F32)

        def kv_step(d, carry):
            m, l, acc = carry
            s, start = tile_scores(d)
            m_new = jnp.maximum(m, jnp.max(s, axis=-1, keepdims=True))
            alpha = jnp.exp(m - m_new)
            p = jnp.exp(s - m_new)
            l = alpha * l + jnp.sum(p, axis=-1, keepdims=True)
            acc = alpha * acc + jnp.dot(p.astype(BF16), v_ref[0, u, pl.ds(start, tile), :],
                                        preferred_element_type=F32)
            return m_new, l, acc

        n_steps = qi + 1 if n_back is None else jnp.minimum(qi, n_back) + 1
        m, l, acc = lax.fori_loop(1, n_steps, kv_step, (m, l, acc))
        if emit_stats:
            out = jnp.where(lane == HEAD_DIM, m, jnp.where(lane == HEAD_DIM + 1, l, acc))
        else:
            out = acc / jnp.maximum(l, TINY)
        o_ref[0, hq, pl.ds(pl.multiple_of(t * tile, tile), tile), :] = out.astype(o_ref.dtype)
        return 0

    lax.fori_loop(0, n_kv * group * n_qt, one_query_tile, 0)


def _dense_attn(q_arr, q_off, k_arr, k_off, v_arr, v_off, bias, *, n_kv_heads, group, tile, n_kv, q_chunk,
                n_back, diag_causal, emit_stats, out_dtype, name):
    B, _, S, _ = q_arr.shape
    n_q_heads = n_kv_heads * group
    nq = n_kv * group
    assert q_off % nq == 0 and k_off % n_kv == 0 and v_off % n_kv == 0
    assert S % q_chunk == 0 and q_chunk % tile == 0 and n_kv_heads % n_kv == 0
    n_qt = q_chunk // tile
    has_bias = bias is not None
    n_bias = bias.shape[1] if has_bias else 0
    in_specs = [
        pl.BlockSpec((1, nq, q_chunk, LANES), lambda u, b, c: (b, q_off // nq + u, c, 0)),
        pl.BlockSpec((1, n_kv, S, LANES), lambda u, b, c: (b, k_off // n_kv + u, 0, 0)),
        pl.BlockSpec((1, n_kv, S, LANES), lambda u, b, c: (b, v_off // n_kv + u, 0, 0)),
    ]
    args = [q_arr, k_arr, v_arr]
    out_bytes = jnp.dtype(out_dtype).itemsize
    blocks = [nq * q_chunk * LANES * 2, 2 * n_kv * S * LANES * 2, nq * q_chunk * LANES * out_bytes]
    if has_bias:
        in_specs.append(pl.BlockSpec((nq, n_bias, tile, tile), lambda u, b, c: (u, 0, 0, 0)))
        args.append(bias)
        blocks.append(nq * n_bias * tile * tile * 4)
    kern = functools.partial(_dense_attn_kernel, tile=tile, n_kv=n_kv, group=group, n_qt=n_qt, n_back=n_back,
                             has_bias=has_bias, n_bias=n_bias, diag_causal=diag_causal, emit_stats=emit_stats)
    return pl.pallas_call(
        kern,
        grid=(n_kv_heads // n_kv, B, S // q_chunk),
        in_specs=in_specs,
        out_specs=pl.BlockSpec((1, nq, q_chunk, LANES), lambda u, b, c: (b, u, c, 0)),
        out_shape=jax.ShapeDtypeStruct((B, n_q_heads, S, LANES), out_dtype),
        compiler_params=pltpu.CompilerParams(
            dimension_semantics=("parallel", "parallel", "arbitrary"),
            vmem_limit_bytes=_vmem_limit(blocks, 8 * tile * tile * 4)),
        name=name,
    )(*args)


def _bucket_np(dist):
    d = np.maximum(dist, 0)
    df = np.maximum(d.astype(np.float32), np.float32(1.0))
    large = REL_MAX_EXACT + (np.log(df / np.float32(REL_MAX_EXACT))
                             / np.float32(math.log(REL_MAX_DIST / REL_MAX_EXACT))
                             * np.float32(REL_BUCKETS - REL_MAX_EXACT)).astype(np.int32)
    large = np.minimum(large, REL_BUCKETS - 1)
    return np.where(d < REL_MAX_EXACT, d, large).astype(np.int32)


def _tile_bias(rel_table, tile, n_tiles, dist_scale, max_dist):
    i = np.arange(tile)[:, None]
    j = np.arange(tile)[None, :]
    dist = np.stack([delta * tile + i - j for delta in range(n_tiles)])
    ok = dist >= 0
    if max_dist is not None:
        ok &= dist <= max_dist
    tab = rel_table[jnp.asarray(_bucket_np(dist * dist_scale))]
    tab = jnp.where(jnp.asarray(ok)[..., None], tab, MASK_NEG)
    return jnp.transpose(tab, (3, 0, 1, 2))


def _merge3_kernel(a_ref, b_ref, c_ref, o_ref):
    parts = [r[0, 0] for r in (a_ref, b_ref, c_ref)]
    ms = [p[:, HEAD_DIM:HEAD_DIM + 1] for p in parts]
    ls = [p[:, HEAD_DIM + 1:HEAD_DIM + 2] for p in parts]
    m_all = jnp.maximum(jnp.maximum(ms[0], ms[1]), ms[2])
    ws = [jnp.exp(m - m_all) for m in ms]
    num = ws[0] * parts[0] + ws[1] * parts[1] + ws[2] * parts[2]
    den = ws[0] * ls[0] + ws[1] * ls[1] + ws[2] * ls[2]
    lane = lax.broadcasted_iota(jnp.int32, num.shape, 1)
    o_ref[0, 0] = jnp.where(lane < HEAD_DIM, num / jnp.maximum(den, TINY), 0.0).astype(o_ref.dtype)


def _merge3(parts):
    B, H, S, _ = parts[0].shape
    ts = 2048
    spec = pl.BlockSpec((1, 1, ts, LANES), lambda b, h, s: (b, h, s, 0))
    return pl.pallas_call(
        _merge3_kernel,
        grid=(B, H, S // ts),
        in_specs=[spec, spec, spec],
        out_specs=spec,
        out_shape=jax.ShapeDtypeStruct((B, H, S, LANES), BF16),
        compiler_params=pltpu.CompilerParams(
            dimension_semantics=("parallel", "parallel", "parallel"),
            vmem_limit_bytes=_vmem_limit([3 * ts * LANES * 4, ts * LANES * 2], 8 * ts * LANES * 4)),
        name="dilated_merge",
    )(*parts)


def _dilated_mixer(P, rel_table, batch, seq):
    parts = []
    for window, dil in DILATED_PATTERNS:
        n = seq // dil
        if dil == 1:
            Pr = P
        else:
            Pr = P.reshape(batch, 3 * N_HEADS, n, dil, LANES).transpose(0, 3, 1, 2, 4)
            Pr = Pr.reshape(batch * dil, 3 * N_HEADS, n, LANES)
        bias = _tile_bias(rel_table, BAND_BLOCK, 2, dil, window // dil)
        n_kv = 8 if dil > 1 else 2
        o = _dense_attn(Pr, 0, Pr, N_HEADS, Pr, 2 * N_HEADS, bias, n_kv_heads=N_HEADS, group=1, tile=BAND_BLOCK,
                        n_kv=n_kv, q_chunk=min(n, 1024), n_back=1, diag_causal=False, emit_stats=True,
                        out_dtype=F32, name=f"dilated_attn_d{dil}")
        if dil > 1:
            o = o.reshape(batch, dil, N_HEADS, n, LANES).transpose(0, 2, 3, 1, 4).reshape(batch, N_HEADS, seq, LANES)
        parts.append(o)
    return _merge3(parts)


def _compress_kernel(x_ref, w1_ref, w2_ref, pos_ref, g_ref, o_ref):
    posw = jnp.dot(pos_ref[0], w1_ref[0], preferred_element_type=F32)[0:1]
    pre = jnp.dot(x_ref[0, 0], w1_ref[0], preferred_element_type=F32) + posw
    hid = 0.5 * pre * (1.0 + jnp.tanh(math.sqrt(2.0 / math.pi) * (pre + 0.044715 * (pre * pre * pre))))
    out = jnp.dot(hid.astype(BF16), w2_ref[0], preferred_element_type=F32)
    ms = jnp.sum(out * out, axis=-1, keepdims=True) * (1.0 / HEAD_DIM)
    normed = out * lax.rsqrt(ms + RMS_EPS) * g_ref[...]
    o_ref[0, 0] = jnp.where(pl.program_id(0) == 0, normed, out).astype(o_ref.dtype)


def _compress(blocks, w1, w2, pos, k_gain_pad):
    _, BK, n, K = blocks.shape
    return pl.pallas_call(
        _compress_kernel,
        grid=(2, BK),
        in_specs=[
            pl.BlockSpec((1, 1, n, K), lambda a, i: (a, i, 0, 0)),
            pl.BlockSpec((1, K, CMP_HIDDEN), lambda a, i: (a, 0, 0)),
            pl.BlockSpec((1, CMP_HIDDEN, LANES), lambda a, i: (a, 0, 0)),
            pl.BlockSpec((1, 8, K), lambda a, i: (a, 0, 0)),
            pl.BlockSpec((1, LANES), lambda a, i: (0, 0)),
        ],
        out_specs=pl.BlockSpec((1, 1, n, LANES), lambda a, i: (a, i, 0, 0)),
        out_shape=jax.ShapeDtypeStruct((2, BK, n, LANES), BF16),
        compiler_params=pltpu.CompilerParams(
            dimension_semantics=("parallel", "parallel"),
            vmem_limit_bytes=_vmem_limit([n * K * 2, K * CMP_HIDDEN * 2, CMP_HIDDEN * LANES * 2, 8 * K * 2,
                                          n * LANES * 2], 8 * n * CMP_HIDDEN * 4)),
        name="nsa_compress",
    )(blocks, w1, w2, pos, k_gain_pad)


def _rank_select(vals, own_row, n_blocks, top):
    jidx = lax.broadcasted_iota(jnp.int32, vals.shape, 0)
    cnt = jnp.zeros(vals.shape, F32)
    for j in range(n_blocks):
        row = vals[j:j + 1, :]
        ahead = (row > vals) | ((row == vals) & (jidx > j))
        cnt = cnt + jnp.where(ahead, 1.0, 0.0)
    return (jidx < own_row) & (cnt < top)


def _cmp_attn_kernel(q_ref, k_ref, v_ref, b_ref, ov_ref, o_ref, sel_ref, *, tq):
    qi = pl.program_id(1)
    k = k_ref[0, 0]
    v = v_ref[0, 0]
    psum = None
    for g in range(NSA_GROUP):
        s = lax.dot_general(q_ref[0, g], k, _NT, preferred_element_type=F32) + b_ref[g]
        m = jnp.max(s, axis=-1, keepdims=True)
        p = jnp.where(m > 0.5 * MASK_NEG, jnp.exp(s - m), 0.0)
        l = jnp.sum(p, axis=-1, keepdims=True)
        probs = p / jnp.maximum(l, TINY)
        o_ref[0, g] = jnp.dot(probs.astype(BF16), v, preferred_element_type=F32).astype(o_ref.dtype)
        psum = probs if psum is None else psum + probs
    imp = None
    for part in _split3(psum):
        term = lax.dot_general(ov_ref[...], part, _NT, preferred_element_type=F32)
        imp = term if imp is None else imp + term
    n_sel = imp.shape[0]
    pos = qi * tq + lax.broadcasted_iota(jnp.int32, (1, tq), 1)
    own = pos // SEL_BLOCK
    jidx = lax.broadcasted_iota(jnp.int32, imp.shape, 0)
    vals = jnp.where(jidx < own, imp, -1.0)
    chosen = _rank_select(vals, own, n_sel, SEL_TOPK - 1) | (jidx == own)
    sel_ref[0, 0] = jnp.where(chosen, 0.0, SEL_NEG).astype(sel_ref.dtype)


def _cmp_attn(P, kv_cmp, bias_c, overlap_t, batch, seq):
    tq = 256
    n_cmp = kv_cmp.shape[2]
    n_sel = overlap_t.shape[0]
    kern = functools.partial(_cmp_attn_kernel, tq=tq)
    return pl.pallas_call(
        kern,
        grid=(NSA_KV_HEADS, seq // tq, batch),
        in_specs=[
            pl.BlockSpec((1, NSA_GROUP, tq, LANES), lambda kh, qi, b: (b, kh, qi, 0)),
            pl.BlockSpec((1, 1, n_cmp, LANES), lambda kh, qi, b: (0, b * NSA_KV_HEADS + kh, 0, 0)),
            pl.BlockSpec((1, 1, n_cmp, LANES), lambda kh, qi, b: (1, b * NSA_KV_HEADS + kh, 0, 0)),
            pl.BlockSpec((NSA_GROUP, tq, n_cmp), lambda kh, qi, b: (kh, qi, 0)),
            pl.BlockSpec((n_sel, n_cmp), lambda kh, qi, b: (0, 0)),
        ],
        out_specs=[
            pl.BlockSpec((1, NSA_GROUP, tq, LANES), lambda kh, qi, b: (b, kh, qi, 0)),
            pl.BlockSpec((1, 1, n_sel, tq), lambda kh, qi, b: (b, kh, 0, qi)),
        ],
        out_shape=[
            jax.ShapeDtypeStruct((batch, N_HEADS, seq, LANES), F32),
            jax.ShapeDtypeStruct((batch, NSA_KV_HEADS, n_sel, seq), BF16),
        ],
        compiler_params=pltpu.CompilerParams(
            dimension_semantics=("parallel", "parallel", "arbitrary"),
            vmem_limit_bytes=_vmem_limit([NSA_GROUP * tq * LANES * 2, 2 * n_cmp * LANES * 2,
                                          NSA_GROUP * tq * n_cmp * 4, NSA_GROUP * tq * LANES * 4, n_sel * tq * 2],
                                         16 * tq * n_cmp * 4)),
        name="nsa_cmp_attn",
    )(P, kv_cmp, kv_cmp, bias_c, overlap_t)


def _nsa_merge_kernel(c_ref, s_ref, w_ref, g_ref, o_ref):
    gates = jax.nn.sigmoid(g_ref[0])
    for h in range(N_HEADS):
        o = (gates[:, 3 * h:3 * h + 1] * c_ref[0, h] + gates[:, 3 * h + 1:3 * h + 2] * s_ref[0, h]
             + gates[:, 3 * h + 2:3 * h + 3] * w_ref[0, h])
        o_ref[0, h] = o.astype(o_ref.dtype)


def _nsa_merge(o_cmp, o_sel, o_win, gate_logits, batch, seq):
    ts = 256
    spec = pl.BlockSpec((1, N_HEADS, ts, LANES), lambda b, s: (b, 0, s, 0))
    return pl.pallas_call(
        _nsa_merge_kernel,
        grid=(batch, seq // ts),
        in_specs=[spec, spec, spec, pl.BlockSpec((1, ts, LANES), lambda b, s: (b, s, 0))],
        out_specs=spec,
        out_shape=jax.ShapeDtypeStruct((batch, N_HEADS, seq, LANES), BF16),
        compiler_params=pltpu.CompilerParams(
            dimension_semantics=("parallel", "parallel"),
            vmem_limit_bytes=_vmem_limit([3 * N_HEADS * ts * LANES * 4, ts * LANES * 4, N_HEADS * ts * LANES * 2],
                                         4 * ts * LANES * 4)),
        name="nsa_merge",
    )(o_cmp, o_sel, o_win, gate_logits.reshape(batch, seq, LANES))


def _nsa_mixer(P, gate_logits, cmp_pos, cmp_w1, cmp_w2, k_gain, rel_table, batch, seq):
    KH = NSA_KV_HEADS
    n_chunk = seq // CMP_STRIDE
    n_sel = seq // SEL_BLOCK

    raw = P[:, N_HEADS:N_HEADS + 4].reshape(batch, 2, 2, seq, 2, HEAD_DIM)
    raw = raw.transpose(1, 0, 2, 4, 3, 5).reshape(2, batch * KH, n_chunk, CMP_STRIDE * HEAD_DIM)
    blocks = jnp.concatenate([raw, jnp.roll(raw, -1, axis=2)], axis=-1)
    pos = jnp.broadcast_to(cmp_pos.reshape(2, 1, CMP_BLOCK * HEAD_DIM), (2, 8, CMP_BLOCK * HEAD_DIM)).astype(BF16)
    w2 = jnp.pad(cmp_w2, ((0, 0), (0, 0), (0, LANES - HEAD_DIM))).astype(BF16)
    k_gain_pad = jnp.pad(k_gain, (0, LANES - HEAD_DIM)).reshape(1, LANES)
    kv_cmp = _compress(blocks, cmp_w1.astype(BF16), w2, pos, k_gain_pad)

    t = np.arange(seq)[:, None]
    i = np.arange(n_chunk)[None, :]
    dist_c = t - (i * CMP_STRIDE + CMP_BLOCK - 1)
    ok = (dist_c >= 0) & (i < n_chunk - 1)
    bias_c = jnp.where(jnp.asarray(ok)[..., None], rel_table[jnp.asarray(_bucket_np(dist_c))], MASK_NEG)
    bias_c = jnp.transpose(bias_c, (2, 0, 1))
    c_start = np.arange(n_chunk)[None, :] * CMP_STRIDE
    s_start = np.arange(n_sel)[:, None] * SEL_BLOCK
    overlap_t = ((c_start < s_start + SEL_BLOCK) & (c_start + CMP_BLOCK > s_start)
                 & (np.arange(n_chunk)[None, :] < n_chunk - 1))
    overlap_t = jnp.asarray(overlap_t.astype(np.float32), BF16)
    o_cmp, sel_neg = _cmp_attn(P, kv_cmp, bias_c, overlap_t, batch, seq)

    lane = jnp.arange(LANES)
    sel_q = jnp.transpose(sel_neg, (0, 1, 3, 2))
    sel_q = jnp.pad(sel_q, ((0, 0), (0, 0), (0, 0), (HEAD_DIM, LANES - HEAD_DIM - n_sel)))
    sel_q = jnp.repeat(sel_q, NSA_GROUP, axis=1)
    q_aug = jnp.where(lane < HEAD_DIM, P[:, :N_HEADS], sel_q)
    onehot = (np.arange(seq)[:, None] // SEL_BLOCK == np.arange(LANES - HEAD_DIM)[None, :]).astype(np.float32)
    onehot = jnp.asarray(np.pad(onehot, ((0, 0), (HEAD_DIM, 0))), BF16)
    k_aug = jnp.where(lane < HEAD_DIM, P[:, 20:24], onehot[None, None])
    bias_sel = _tile_bias(rel_table, DENSE_TILE, 10, 1, None)
    o_sel = _dense_attn(q_aug, 0, k_aug, 0, P, 24, bias_sel, n_kv_heads=KH, group=NSA_GROUP, tile=DENSE_TILE,
                        n_kv=1, q_chunk=DENSE_TILE, n_back=None, diag_causal=False, emit_stats=False,
                        out_dtype=F32, name="nsa_sel_attn")
    bias_win = _tile_bias(rel_table, DENSE_TILE, 3, 1, NSA_WINDOW - 1)
    o_win = _dense_attn(q_aug, 0, P, 28, P, 32, bias_win, n_kv_heads=KH, group=NSA_GROUP, tile=DENSE_TILE,
                        n_kv=1, q_chunk=DENSE_TILE, n_back=2, diag_causal=False, emit_stats=False,
                        out_dtype=F32, name="nsa_win_attn")
    return _nsa_merge(o_cmp, o_sel, o_win, gate_logits, batch, seq)


def _fox_decay_kernel(f_ref, bf_ref, tri_ref, eq_ref, ek_ref, carry):
    @pl.when(pl.program_id(1) == 0)
    def _():
        carry[...] = jnp.zeros_like(carry)

    z = f_ref[0] + bf_ref[...]
    log_f = jnp.minimum(z, 0.0) - jnp.log1p(jnp.exp(-jnp.abs(z)))
    c = carry[0:1, :]
    for part in _split3(log_f):
        c = c + jnp.dot(tri_ref[...], part, preferred_element_type=F32)
    ts = c.shape[0]
    carry[...] = jnp.broadcast_to(c[ts - 1:ts, :], carry.shape)
    lane = lax.broadcasted_iota(jnp.int32, (ts, LANES), 1) - HEAD_DIM
    for h in range(N_HEADS):
        c1, c2, c3 = _split3(c[:, h:h + 1])
        c1, c2, c3 = c1.astype(F32), c2.astype(F32), c3.astype(F32)
        eq = jnp.where(lane == 0, c1, jnp.where(lane == 1, c2, jnp.where(lane == 2, c3,
                       jnp.where((lane >= 3) & (lane < 6), 1.0, 0.0))))
        ek = jnp.where(lane == 3, -c1, jnp.where(lane == 4, -c2, jnp.where(lane == 5, -c3,
                       jnp.where((lane >= 0) & (lane < 3), 1.0, 0.0))))
        eq_ref[0, h] = eq.astype(eq_ref.dtype)
        ek_ref[0, h] = ek.astype(ek_ref.dtype)


def _fox_decay(f_logits, b_f, batch, seq):
    ts = 256
    tri = jnp.asarray(np.tril(np.ones((ts, ts), np.float32)), BF16)
    spec = pl.BlockSpec((1, N_HEADS, ts, LANES), lambda b, s: (b, 0, s, 0))
    return pl.pallas_call(
        _fox_decay_kernel,
        grid=(batch, seq // ts),
        in_specs=[
            pl.BlockSpec((1, ts, LANES), lambda b, s: (b, s, 0)),
            pl.BlockSpec((1, LANES), lambda b, s: (0, 0)),
            pl.BlockSpec((ts, ts), lambda b, s: (0, 0)),
        ],
        out_specs=[spec, spec],
        out_shape=[jax.ShapeDtypeStruct((batch, N_HEADS, seq, LANES), BF16)] * 2,
        scratch_shapes=[pltpu.VMEM((8, LANES), F32)],
        compiler_params=pltpu.CompilerParams(
            dimension_semantics=("parallel", "arbitrary"),
            vmem_limit_bytes=_vmem_limit([ts * LANES * 4, ts * ts * 2, 2 * N_HEADS * ts * LANES * 2],
                                         16 * ts * LANES * 4)),
        name="fox_decay",
    )(f_logits.reshape(batch, seq, LANES), jnp.pad(b_f, (0, LANES - N_HEADS)).reshape(1, LANES), tri)


def _fox_mixer(P, f_logits, b_f, batch, seq):
    eq, ek = _fox_decay(f_logits, b_f, batch, seq)
    lane = jnp.arange(LANES)
    q_aug = jnp.where(lane < HEAD_DIM, P[:, :N_HEADS], eq)
    k_aug = jnp.where(lane < HEAD_DIM, P[:, N_HEADS:2 * N_HEADS], ek)
    return _dense_attn(q_aug, 0, k_aug, 0, P, 2 * N_HEADS, None, n_kv_heads=N_HEADS, group=1, tile=DENSE_TILE,
                       n_kv=1, q_chunk=2 * DENSE_TILE, n_back=None, diag_causal=True, emit_stats=False,
                       out_dtype=BF16, name="fox_attn")


def _moba_gate_kernel(q_ref, k_ref, sel_ref, *, n_blk):
    k = k_ref[0, 0].astype(F32)
    S = k.shape[0]
    k_mean = jnp.sum(k.reshape(n_blk, S // n_blk, LANES), axis=1) * (1.0 / (S // n_blk))
    hi = k_mean.astype(BF16)
    lo = (k_mean - hi.astype(F32)).astype(BF16)
    q = q_ref[0, 0]
    gate = (lax.dot_general(hi, q, _NT, preferred_element_type=F32)
            + lax.dot_general(lo, q, _NT, preferred_element_type=F32))
    own = lax.broadcasted_iota(jnp.int32, (1, S), 1) // (S // n_blk)
    jidx = lax.broadcasted_iota(jnp.int32, gate.shape, 0)
    vals = jnp.where(jidx < own, gate, -jnp.inf)
    chosen = _rank_select(vals, own, n_blk, MOBA_TOPK) | (jidx == own)
    sel_ref[0, 0] = jnp.where(chosen, 0.0, SEL_NEG).astype(sel_ref.dtype)


def _moba_gate(P, batch, seq):
    n_blk = seq // MOBA_BLOCK
    return pl.pallas_call(
        functools.partial(_moba_gate_kernel, n_blk=n_blk),
        grid=(batch, N_HEADS),
        in_specs=[
            pl.BlockSpec((1, 1, seq, LANES), lambda b, h: (b, h, 0, 0)),
            pl.BlockSpec((1, 1, seq, LANES), lambda b, h: (b, N_HEADS + h, 0, 0)),
        ],
        out_specs=pl.BlockSpec((1, 1, n_blk, seq), lambda b, h: (b, h, 0, 0)),
        out_shape=jax.ShapeDtypeStruct((batch, N_HEADS, n_blk, seq), BF16),
        compiler_params=pltpu.CompilerParams(
            dimension_semantics=("parallel", "parallel"),
            vmem_limit_bytes=_vmem_limit([2 * seq * LANES * 2, n_blk * seq * 2], 2 * seq * LANES * 4 + 24 * n_blk * seq * 4)),
        name="moba_gate",
    )(P, P)


def _moba_mixer(P, rel_table, batch, seq):
    n_blk = seq // MOBA_BLOCK
    sel_neg = _moba_gate(P, batch, seq)
    lane = jnp.arange(LANES)
    sel_q = jnp.pad(jnp.transpose(sel_neg, (0, 1, 3, 2)),
                    ((0, 0), (0, 0), (0, 0), (HEAD_DIM, LANES - HEAD_DIM - n_blk)))
    q_aug = jnp.where(lane < HEAD_DIM, P[:, :N_HEADS], sel_q)
    onehot = (np.arange(seq)[:, None] // MOBA_BLOCK == np.arange(LANES - HEAD_DIM)[None, :]).astype(np.float32)
    onehot = jnp.asarray(np.pad(onehot, ((0, 0), (HEAD_DIM, 0))), BF16)
    k_aug = jnp.where(lane < HEAD_DIM, P[:, N_HEADS:2 * N_HEADS], onehot[None, None])
    bias = _tile_bias(rel_table, DENSE_TILE, 10, 1, None)
    return _dense_attn(q_aug, 0, k_aug, 0, P, 2 * N_HEADS, bias, n_kv_heads=N_HEADS, group=1, tile=DENSE_TILE,
                       n_kv=1, q_chunk=2 * DENSE_TILE, n_back=None, diag_causal=False, emit_stats=False,
                       out_dtype=BF16, name="moba_attn")


def _pad_head_cols(w, n_heads):
    w = w.reshape(w.shape[0], n_heads, HEAD_DIM)
    return jnp.pad(w, ((0, 0), (0, 0), (0, LANES - HEAD_DIM))).reshape(w.shape[0], n_heads * LANES)


def _head_vec(gain, n_heads, scale):
    v = jnp.pad(gain * scale, (0, LANES - HEAD_DIM))
    return jnp.tile(v, n_heads)


def _qkv_columns(w_q, w_k, w_v, q_gain, k_gain):
    w = jnp.concatenate([_pad_head_cols(w_q, N_HEADS), _pad_head_cols(w_k, N_HEADS), _pad_head_cols(w_v, N_HEADS)],
                        axis=1).astype(BF16)
    ones = jnp.ones((N_HEADS * LANES,), F32)
    gain = jnp.concatenate([_head_vec(q_gain, N_HEADS, QK_SCALE), _head_vec(k_gain, N_HEADS, 1.0), ones])
    flag = jnp.concatenate([ones, ones, 0.0 * ones])
    return w, gain.reshape(1, -1), flag.reshape(1, -1)


def _dilated_layer(x2, norm_g, w, qg, kg, rel_table, batch, seq):
    D = D_MODEL
    w_pad, gain, flag = _qkv_columns(w[:, :D], w[:, D:2 * D], w[:, 2 * D:], qg, kg)
    P = _proj(x2, norm_g, w_pad, gain, flag, None, batch, seq)
    return _dilated_mixer(P, rel_table, batch, seq)


def _nsa_layer(x2, norm_g, w, cmp_pos, cmp_w1, cmp_w2, qg, kg, rel_table, batch, seq):
    D = D_MODEL
    kvd = NSA_KV_HEADS * HEAD_DIM
    seg = [w[:, D + i * kvd:D + (i + 1) * kvd] for i in range(6)]
    w_pad = jnp.concatenate([_pad_head_cols(w[:, :D], N_HEADS), seg[0], seg[1]]
                            + [_pad_head_cols(s, NSA_KV_HEADS) for s in seg[2:]], axis=1).astype(BF16)
    one4 = jnp.ones((NSA_KV_HEADS * LANES,), F32)
    kvec = _head_vec(kg, NSA_KV_HEADS, 1.0)
    gain = jnp.concatenate([_head_vec(qg, N_HEADS, QK_SCALE), one4, kvec, one4, kvec, one4]).reshape(1, -1)
    flag = jnp.concatenate([jnp.ones((N_HEADS * LANES,), F32), 0 * one4, one4, 0 * one4, one4,
                            0 * one4]).reshape(1, -1)
    w_aux = jnp.pad(w[:, D + 6 * kvd:], ((0, 0), (0, LANES - 3 * N_HEADS))).astype(BF16)
    P, gate_logits = _proj(x2, norm_g, w_pad, gain, flag, w_aux, batch, seq)
    return _nsa_mixer(P, gate_logits, cmp_pos, cmp_w1, cmp_w2, kg, rel_table, batch, seq)


def _fox_layer(x2, norm_g, w, b_f, qg, kg, batch, seq):
    D = D_MODEL
    w_pad, gain, flag = _qkv_columns(w[:, :D], w[:, D:2 * D], w[:, 2 * D:3 * D], qg, kg)
    w_aux = jnp.pad(w[:, 3 * D:], ((0, 0), (0, LANES - N_HEADS))).astype(BF16)
    P, f_logits = _proj(x2, norm_g, w_pad, gain, flag, w_aux, batch, seq)
    return _fox_mixer(P, f_logits, b_f, batch, seq)


def _moba_layer(x2, norm_g, w, qg, kg, rel_table, batch, seq):
    D = D_MODEL
    w_pad, gain, flag = _qkv_columns(w[:, :D], w[:, D:2 * D], w[:, 2 * D:], qg, kg)
    P = _proj(x2, norm_g, w_pad, gain, flag, None, batch, seq)
    return _moba_mixer(P, rel_table, batch, seq)


def kernel(x, rel_table, attn_norm, mlp_norm, q_gain, k_gain, w_out, mlp_w_up, mlp_w_down, dsa_w_in, nsa_w_in,
           nsa_cmp_pos, nsa_cmp_w1, nsa_cmp_w2, fox_w_in, fox_b_f, moba_w_in):
    batch, seq, _ = x.shape
    depth = attn_norm.shape[0]
    x2 = x.reshape(batch * seq, D_MODEL)
    D = D_MODEL
    for layer in range(depth):
        kind, r = layer % 4, layer // 4
        qg, kg, ng = q_gain[layer], k_gain[layer], attn_norm[layer]
        if kind == 0:
            mixed = _dilated_layer(x2, ng, dsa_w_in[r], qg, kg, rel_table, batch, seq)
        elif kind == 1:
            mixed = _nsa_layer(x2, ng, nsa_w_in[r], nsa_cmp_pos[r], nsa_cmp_w1[r], nsa_cmp_w2[r], qg, kg,
                               rel_table, batch, seq)
        elif kind == 2:
            mixed = _fox_layer(x2, ng, fox_w_in[r], fox_b_f[r], qg, kg, batch, seq)
        else:
            mixed = _moba_layer(x2, ng, moba_w_in[r], qg, kg, rel_table, batch, seq)
        w_out_pad = jnp.pad(w_out[layer].reshape(N_HEADS, HEAD_DIM, D), ((0, 0), (0, LANES - HEAD_DIM), (0, 0)))
        x2 = _out_mlp(x2, mixed, w_out_pad.astype(BF16), mlp_norm[layer], mlp_w_up[layer].astype(BF16),
                      mlp_w_down[layer].astype(BF16), batch, seq)
    return x2.reshape(batch, seq, D_MODEL)
```

```python
import functools
import math

import jax
import jax.numpy as jnp
import numpy as np
from jax import lax
from jax.experimental import pallas as pl
from jax.experimental.pallas import tpu as pltpu

F32 = jnp.float32
BF16 = jnp.bfloat16

D_MODEL = 1024
N_HEADS = 16
HEAD_DIM = 64
D_FF = 4 * D_MODEL
RMS_EPS = 1e-6
MASK_NEG = -1e30
TINY = 1e-30
QK_SCALE = HEAD_DIM ** -0.5

REL_BUCKETS = 32
REL_MAX_EXACT = REL_BUCKETS // 2
REL_MAX_DIST = 2048

DILATED_PATTERNS = ((128, 1), (512, 4), (2048, 16))
BAND_BLOCK = 128

NSA_KV_HEADS = 4
NSA_GROUP = N_HEADS // NSA_KV_HEADS
CMP_STRIDE = 16
CMP_BLOCK = 2 * CMP_STRIDE
CMP_HIDDEN = 256
SEL_BLOCK = 64
SEL_TOPK = 16
NSA_WINDOW = 512

MOBA_BLOCK = 256
MOBA_TOPK = 3

LANES = 128
VMEM_BYTES_V7X = 64 * 1024 * 1024
VMEM_HEADROOM = 6 * 1024 * 1024

DENSE_TILE = 256
SEL_NEG = -(2.0 ** 100)

_NT = (((1,), (1,)), ((), ()))


def _vmem_limit(block_bytes, temp_bytes):
    need = 2 * int(sum(block_bytes)) + int(temp_bytes) + (2 << 20)
    return int(min(need, VMEM_BYTES_V7X - VMEM_HEADROOM))


def _split3(v):
    a = v.astype(BF16)
    r = v - a.astype(F32)
    b = r.astype(BF16)
    c = (r - b.astype(F32)).astype(BF16)
    return a, b, c


def _proj_kernel(*refs, has_aux):
    if has_aux:
        x_ref, g_ref, w_ref, cg_ref, cf_ref, bd_ref, wa_ref, o_ref, oa_ref, h_scr = refs
    else:
        x_ref, g_ref, w_ref, cg_ref, cf_ref, bd_ref, o_ref, h_scr = refs

    @pl.when(pl.program_id(1) == 0)
    def _():
        x = x_ref[...]
        ms = jnp.mean(x * x, axis=-1, keepdims=True)
        h_scr[...] = (x * lax.rsqrt(ms + RMS_EPS) * g_ref[...]).astype(BF16)
        if has_aux:
            oa_ref[...] = jnp.dot(h_scr[...], wa_ref[...], preferred_element_type=F32)

    y = jnp.dot(h_scr[...], w_ref[...], preferred_element_type=F32)
    y2 = y * y
    hi = y2.astype(BF16)
    lo = (y2 - hi.astype(F32)).astype(BF16)
    ms = (jnp.dot(hi, bd_ref[...], preferred_element_type=F32)
          + jnp.dot(lo, bd_ref[...], preferred_element_type=F32))
    rs = lax.rsqrt(ms + RMS_EPS)
    out = (y * jnp.where(cf_ref[...] > 0.5, rs, 1.0) * cg_ref[...]).astype(o_ref.dtype)
    o_ref[0, 0] = out[:, :LANES]
    o_ref[0, 1] = out[:, LANES:]


def _proj(x2, norm_g, w_pad, col_gain, col_flag, w_aux, batch, seq):
    T = x2.shape[0]
    N = w_pad.shape[1]
    tm, tn = 1024, 2 * LANES
    spb = seq // tm
    bd = np.zeros((tn, tn), np.float32)
    for a in range(tn // LANES):
        bd[a * LANES:(a + 1) * LANES, a * LANES:(a + 1) * LANES] = 1.0 / HEAD_DIM
    bd = jnp.asarray(bd, BF16)
    has_aux = w_aux is not None
    in_specs = [
        pl.BlockSpec((tm, D_MODEL), lambda i, j: (i, 0)),
        pl.BlockSpec((1, D_MODEL), lambda i, j: (0, 0)),
        pl.BlockSpec((D_MODEL, tn), lambda i, j: (0, j)),
        pl.BlockSpec((1, tn), lambda i, j: (0, j)),
        pl.BlockSpec((1, tn), lambda i, j: (0, j)),
        pl.BlockSpec((tn, tn), lambda i, j: (0, 0)),
    ]
    args = [x2, norm_g.reshape(1, D_MODEL), w_pad, col_gain, col_flag, bd]
    out_shape = [jax.ShapeDtypeStruct((batch, N // LANES, seq, LANES), BF16)]
    out_specs = [pl.BlockSpec((1, 2, tm, LANES), lambda i, j: (i // spb, j, i % spb, 0))]
    if has_aux:
        in_specs.append(pl.BlockSpec((D_MODEL, LANES), lambda i, j: (0, 0)))
        args.append(w_aux)
        out_shape.append(jax.ShapeDtypeStruct((T, LANES), F32))
        out_specs.append(pl.BlockSpec((tm, LANES), lambda i, j: (i, 0)))
    blocks = [tm * D_MODEL * 4, D_MODEL * tn * 2, tm * tn * 2, tn * tn * 2, tm * LANES * 4, D_MODEL * LANES * 2]
    res = pl.pallas_call(
        functools.partial(_proj_kernel, has_aux=has_aux),
        grid=(T // tm, N // tn),
        in_specs=in_specs,
        out_specs=out_specs,
        out_shape=out_shape,
        scratch_shapes=[pltpu.VMEM((tm, D_MODEL), BF16)],
        compiler_params=pltpu.CompilerParams(
            dimension_semantics=("parallel", "arbitrary"),
            vmem_limit_bytes=_vmem_limit(blocks, tm * D_MODEL * 2 + 6 * tm * tn * 4)),
        name="proj",
    )(*args)
    return res if has_aux else res[0]


def _out_mlp_kernel(x_ref, m_ref, wo_ref, g_ref, wu_ref, wd_ref, o_ref, h_scr):
    @pl.when(pl.program_id(1) == 0)
    def _():
        x1 = x_ref[...]
        for h in range(N_HEADS):
            x1 = x1 + jnp.dot(m_ref[0, h], wo_ref[h], preferred_element_type=F32)
        o_ref[...] = x1
        ms = jnp.mean(x1 * x1, axis=-1, keepdims=True)
        h_scr[...] = (x1 * lax.rsqrt(ms + RMS_EPS) * g_ref[...]).astype(BF16)

    u = jnp.dot(h_scr[...], wu_ref[...], preferred_element_type=F32)
    a = jnp.square(jnp.maximum(u, 0.0)).astype(BF16)
    o_ref[...] += jnp.dot(a, wd_ref[...], preferred_element_type=F32)


def _out_mlp(x2, mixed, w_out_pad, norm_g, w_up, w_down, batch, seq):
    T = x2.shape[0]
    tm, tf = 1024, 512
    spb = seq // tm
    blocks = [tm * D_MODEL * 4, N_HEADS * tm * LANES * 2, N_HEADS * LANES * D_MODEL * 2,
              D_MODEL * tf * 2, tf * D_MODEL * 2, tm * D_MODEL * 4]
    return pl.pallas_call(
        _out_mlp_kernel,
        grid=(T // tm, D_FF // tf),
        in_specs=[
            pl.BlockSpec((tm, D_MODEL), lambda i, f: (i, 0)),
            pl.BlockSpec((1, N_HEADS, tm, LANES), lambda i, f: (i // spb, 0, i % spb, 0)),
            pl.BlockSpec((N_HEADS, LANES, D_MODEL), lambda i, f: (0, 0, 0)),
            pl.BlockSpec((1, D_MODEL), lambda i, f: (0, 0)),
            pl.BlockSpec((D_MODEL, tf), lambda i, f: (0, f)),
            pl.BlockSpec((tf, D_MODEL), lambda i, f: (f, 0)),
        ],
        out_specs=pl.BlockSpec((tm, D_MODEL), lambda i, f: (i, 0)),
        out_shape=jax.ShapeDtypeStruct((T, D_MODEL), F32),
        scratch_shapes=[pltpu.VMEM((tm, D_MODEL), BF16)],
        compiler_params=pltpu.CompilerParams(
            dimension_semantics=("parallel", "arbitrary"),
            vmem_limit_bytes=_vmem_limit(blocks, tm * D_MODEL * 2 + 3 * tm * tf * 4 + 2 * tm * D_MODEL * 4)),
        name="out_mlp",
    )(x2, mixed, w_out_pad, norm_g.reshape(1, D_MODEL), w_up, w_down)


def _dense_attn_kernel(*refs, tile, n_kv, group, n_qt, n_back, has_bias, n_bias, diag_causal, emit_stats):
    if has_bias:
        q_ref, k_ref, vt_ref, b_ref, o_ref, acc_scr = refs
    else:
        q_ref, k_ref, vt_ref, o_ref, acc_scr = refs
        b_ref = None
    chunk = pl.program_id(2)
    n_streams = n_kv * group
    channel = lax.broadcasted_iota(jnp.int32, (LANES, tile), 0)
    if diag_causal:
        visible = (lax.broadcasted_iota(jnp.int32, (tile, tile), 0)
                   <= lax.broadcasted_iota(jnp.int32, (tile, tile), 1))

    def one_query_tile(t, _):
        qi = chunk * n_qt + t
        row0 = pl.multiple_of(t * tile, tile)
        qs = [q_ref[0, h, pl.ds(row0, tile), :] for h in range(n_streams)]

        def tile_step(d, carry):
            first = carry is None
            ki = qi - d
            start = pl.multiple_of(ki * tile, tile)
            scores = []
            for h in range(n_streams):
                s = lax.dot_general(k_ref[0, h // group, pl.ds(start, tile), :], qs[h], _NT,
                                    preferred_element_type=F32)
                if has_bias:
                    s = s + b_ref[h, jnp.minimum(d, n_bias - 1)]
                if first and diag_causal:
                    s = jnp.where(visible, s, MASK_NEG)
                scores.append(s)
            stats = []
            for h in range(n_streams):
                s = scores[h]
                if first:
                    m = jnp.max(s, axis=0, keepdims=True)
                    p = jnp.exp(s - m)
                    stats.append((m, None, jnp.sum(p, axis=0, keepdims=True), p.astype(BF16)))
                else:
                    m0, l0 = carry[h]
                    m = jnp.maximum(m0, jnp.max(s, axis=0, keepdims=True))
                    alpha = jnp.exp(m0 - m)
                    p = jnp.exp(s - m)
                    stats.append((m, alpha, alpha * l0 + jnp.sum(p, axis=0, keepdims=True), p.astype(BF16)))
            for h in range(n_streams):
                alpha, p = stats[h][1], stats[h][3]
                pv = jnp.dot(vt_ref[0, h // group, ki], p, preferred_element_type=F32)
                acc_scr[h] = pv if first else alpha * acc_scr[h] + pv
            return tuple((m, l) for m, _, l, _ in stats)

        carry = tile_step(0, None)
        n_steps = qi + 1 if n_back is None else jnp.minimum(qi, n_back) + 1
        carry = lax.fori_loop(1, n_steps, tile_step, carry)
        for h in range(n_streams):
            m, l = carry[h]
            acc = acc_scr[h]
            if emit_stats:
                out_t = jnp.where(channel == HEAD_DIM, m, jnp.where(channel == HEAD_DIM + 1, l, acc))
            else:
                out_t = acc / jnp.maximum(l, TINY)
            o_ref[0, h, pl.ds(row0, tile), :] = out_t.T.astype(o_ref.dtype)
        return 0

    lax.fori_loop(0, n_qt, one_query_tile, 0)


def _value_tiles(v, tile):
    B, H, S, _ = v.shape
    return jnp.swapaxes(v.reshape(B, H, S // tile, tile, LANES), -1, -2)


def _dense_attn(q_arr, q_off, k_arr, k_off, vt_arr, v_off, bias, *, n_kv_heads, group, tile, n_kv, q_chunk,
                n_back, diag_causal, emit_stats, out_dtype, name):
    B, _, S, _ = q_arr.shape
    n_q_heads = n_kv_heads * group
    nq = n_kv * group
    assert q_off % nq == 0 and k_off % n_kv == 0 and v_off % n_kv == 0
    assert S % q_chunk == 0 and q_chunk % tile == 0 and n_kv_heads % n_kv == 0
    n_qt = q_chunk // tile
    has_bias = bias is not None
    n_bias = bias.shape[1] if has_bias else 0
    in_specs = [
        pl.BlockSpec((1, nq, q_chunk, LANES), lambda u, b, c: (b, q_off // nq + u, c, 0)),
        pl.BlockSpec((1, n_kv, S, LANES), lambda u, b, c: (b, k_off // n_kv + u, 0, 0)),
        pl.BlockSpec((1, n_kv, S // tile, LANES, tile), lambda u, b, c: (b, v_off // n_kv + u, 0, 0, 0)),
    ]
    args = [q_arr, k_arr, vt_arr]
    out_bytes = jnp.dtype(out_dtype).itemsize
    blocks = [nq * q_chunk * LANES * 2, 2 * n_kv * S * LANES * 2, nq * q_chunk * LANES * out_bytes]
    if has_bias:
        in_specs.append(pl.BlockSpec((nq, n_bias, tile, tile), lambda u, b, c: (u, 0, 0, 0)))
        args.append(bias)
        blocks.append(nq * n_bias * tile * tile * 4)
    kern = functools.partial(_dense_attn_kernel, tile=tile, n_kv=n_kv, group=group, n_qt=n_qt, n_back=n_back,
                             has_bias=has_bias, n_bias=n_bias, diag_causal=diag_causal, emit_stats=emit_stats)
    return pl.pallas_call(
        kern,
        grid=(n_kv_heads // n_kv, B, S // q_chunk),
        in_specs=in_specs,
        out_specs=pl.BlockSpec((1, nq, q_chunk, LANES), lambda u, b, c: (b, u, c, 0)),
        out_shape=jax.ShapeDtypeStruct((B, n_q_heads, S, LANES), out_dtype),
        scratch_shapes=[pltpu.VMEM((nq, LANES, tile), F32)],
        compiler_params=pltpu.CompilerParams(
            dimension_semantics=("parallel", "parallel", "arbitrary"),
            vmem_limit_bytes=_vmem_limit(blocks, nq * (4 * tile * tile + 3 * LANES * tile) * 4)),
        name=name,
    )(*args)


def _bucket_np(dist):
    d = np.maximum(dist, 0)
    df = np.maximum(d.astype(np.float32), np.float32(1.0))
    large = REL_MAX_EXACT + (np.log(df / np.float32(REL_MAX_EXACT))
                             / np.float32(math.log(REL_MAX_DIST / REL_MAX_EXACT))
                             * np.float32(REL_BUCKETS - REL_MAX_EXACT)).astype(np.int32)
    large = np.minimum(large, REL_BUCKETS - 1)
    return np.where(d < REL_MAX_EXACT, d, large).astype(np.int32)


def _dist_values(rel_table, dist, ok):
    vals = rel_table[jnp.asarray(_bucket_np(dist))]
    return jnp.where(jnp.asarray(ok)[:, None], vals, MASK_NEG).T


def _skew_kernel(w_ref, o_ref, *, step, col0, mask_last_row):
    _, n_out, n_rows, width = o_ref.shape
    x = jnp.broadcast_to(w_ref[0], (n_rows, w_ref.shape[2]))
    x = pltpu.roll(x, 0, 1, stride=step, stride_axis=0)
    if mask_last_row:
        row = lax.broadcasted_iota(jnp.int32, x.shape, 0)
        x = jnp.where(row == n_rows - 1, MASK_NEG, x)
    for a in range(n_out):
        o_ref[0, a] = x[:, col0 + a * width:col0 + (a + 1) * width]


def _skew(w, n_rows, step, col0, n_out, width, mask_last_row=False):
    H, W = w.shape
    assert W % LANES == 0 and col0 % LANES == 0 and width % LANES == 0 and col0 + n_out * width <= W
    assert col0 - step * (n_rows - 1) >= 0
    return pl.pallas_call(
        functools.partial(_skew_kernel, step=step, col0=col0, mask_last_row=mask_last_row),
        grid=(H,),
        in_specs=[pl.BlockSpec((1, 1, W), lambda h: (h, 0, 0))],
        out_specs=pl.BlockSpec((1, n_out, n_rows, width), lambda h: (h, 0, 0, 0)),
        out_shape=jax.ShapeDtypeStruct((H, n_out, n_rows, width), F32),
        compiler_params=pltpu.CompilerParams(
            dimension_semantics=("parallel",),
            vmem_limit_bytes=_vmem_limit([W * 4, n_out * n_rows * width * 4], 3 * n_rows * W * 4)),
        name="bias_skew",
    )(w.reshape(H, 1, W))


def _tile_bias(rel_table, tile, n_tiles, dist_scale, max_dist):
    d = np.arange((n_tiles + 1) * tile) - tile
    ok = d >= 0
    if max_dist is not None:
        ok &= d <= max_dist
    w = _dist_values(rel_table, d * dist_scale, ok)
    return _skew(w, tile, 1, tile, n_tiles, tile)


def _merge3_kernel(a_ref, b_ref, c_ref, o_ref):
    parts = [r[0, 0] for r in (a_ref, b_ref, c_ref)]
    ms = [p[:, HEAD_DIM:HEAD_DIM + 1] for p in parts]
    ls = [p[:, HEAD_DIM + 1:HEAD_DIM + 2] for p in parts]
    m_all = jnp.maximum(jnp.maximum(ms[0], ms[1]), ms[2])
    ws = [jnp.exp(m - m_all) for m in ms]
    num = ws[0] * parts[0] + ws[1] * parts[1] + ws[2] * parts[2]
    den = ws[0] * ls[0] + ws[1] * ls[1] + ws[2] * ls[2]
    lane = lax.broadcasted_iota(jnp.int32, num.shape, 1)
    o_ref[0, 0] = jnp.where(lane < HEAD_DIM, num / jnp.maximum(den, TINY), 0.0).astype(o_ref.dtype)


def _merge3(parts):
    B, H, S, _ = parts[0].shape
    ts = 2048
    spec = pl.BlockSpec((1, 1, ts, LANES), lambda b, h, s: (b, h, s, 0))
    return pl.pallas_call(
        _merge3_kernel,
        grid=(B, H, S // ts),
        in_specs=[spec, spec, spec],
        out_specs=spec,
        out_shape=jax.ShapeDtypeStruct((B, H, S, LANES), BF16),
        compiler_params=pltpu.CompilerParams(
            dimension_semantics=("parallel", "parallel", "parallel"),
            vmem_limit_bytes=_vmem_limit([3 * ts * LANES * 4, ts * LANES * 2], 8 * ts * LANES * 4)),
        name="dilated_merge",
    )(*parts)


def _dilated_mixer(P, rel_table, batch, seq):
    parts = []
    for window, dil in DILATED_PATTERNS:
        n = seq // dil
        if dil == 1:
            Pr = P
        else:
            Pr = P.reshape(batch, 3 * N_HEADS, n, dil, LANES).transpose(0, 3, 1, 2, 4)
            Pr = Pr.reshape(batch * dil, 3 * N_HEADS, n, LANES)
        bias = _tile_bias(rel_table, BAND_BLOCK, 2, dil, window // dil)
        vt = _value_tiles(Pr[:, 2 * N_HEADS:], BAND_BLOCK)
        o = _dense_attn(Pr, 0, Pr, N_HEADS, vt, 0, bias, n_kv_heads=N_HEADS, group=1, tile=BAND_BLOCK,
                        n_kv=4, q_chunk=min(n, 1024), n_back=1, diag_causal=False, emit_stats=True,
                        out_dtype=F32, name=f"dilated_attn_d{dil}")
        if dil > 1:
            o = o.reshape(batch, dil, N_HEADS, n, LANES).transpose(0, 2, 3, 1, 4).reshape(batch, N_HEADS, seq, LANES)
        parts.append(o)
    return _merge3(parts)


def _compress_kernel(x_ref, w1_ref, w2_ref, pos_ref, g_ref, o_ref):
    posw = jnp.dot(pos_ref[0], w1_ref[0], preferred_element_type=F32)[0:1]
    pre = jnp.dot(x_ref[0, 0], w1_ref[0], preferred_element_type=F32) + posw
    hid = 0.5 * pre * (1.0 + jnp.tanh(math.sqrt(2.0 / math.pi) * (pre + 0.044715 * (pre * pre * pre))))
    out = jnp.dot(hid.astype(BF16), w2_ref[0], preferred_element_type=F32)
    ms = jnp.sum(out * out, axis=-1, keepdims=True) * (1.0 / HEAD_DIM)
    normed = out * lax.rsqrt(ms + RMS_EPS) * g_ref[...]
    o_ref[0, 0] = jnp.where(pl.program_id(0) == 0, normed, out).astype(o_ref.dtype)


def _compress(blocks, w1, w2, pos, k_gain_pad):
    _, BK, n, K = blocks.shape
    return pl.pallas_call(
        _compress_kernel,
        grid=(2, BK),
        in_specs=[
            pl.BlockSpec((1, 1, n, K), lambda a, i: (a, i, 0, 0)),
            pl.BlockSpec((1, K, CMP_HIDDEN), lambda a, i: (a, 0, 0)),
            pl.BlockSpec((1, CMP_HIDDEN, LANES), lambda a, i: (a, 0, 0)),
            pl.BlockSpec((1, 8, K), lambda a, i: (a, 0, 0)),
            pl.BlockSpec((1, LANES), lambda a, i: (0, 0)),
        ],
        out_specs=pl.BlockSpec((1, 1, n, LANES), lambda a, i: (a, i, 0, 0)),
        out_shape=jax.ShapeDtypeStruct((2, BK, n, LANES), BF16),
        compiler_params=pltpu.CompilerParams(
            dimension_semantics=("parallel", "parallel"),
            vmem_limit_bytes=_vmem_limit([n * K * 2, K * CMP_HIDDEN * 2, CMP_HIDDEN * LANES * 2, 8 * K * 2,
                                          n * LANES * 2], 8 * n * CMP_HIDDEN * 4)),
        name="nsa_compress",
    )(blocks, w1, w2, pos, k_gain_pad)


def _rank_select(vals, own_row, n_blocks, top):
    jidx = lax.broadcasted_iota(jnp.int32, vals.shape, 0)
    cnt = jnp.zeros(vals.shape, F32)
    for j in range(n_blocks):
        row = vals[j:j + 1, :]
        ahead = (row > vals) | ((row == vals) & (jidx > j))
        cnt = cnt + jnp.where(ahead, 1.0, 0.0)
    return (jidx < own_row) & (cnt < top)


def _cmp_attn_kernel(q_ref, k_ref, vt_ref, b_ref, ov_ref, o_ref, sel_ref, *, tq):
    qi = pl.program_id(1)
    k = k_ref[0, 0]
    vt = vt_ref[0]
    psum = None
    for g in range(NSA_GROUP):
        s = lax.dot_general(k, q_ref[0, g], _NT, preferred_element_type=F32) + b_ref[g]
        m = jnp.max(s, axis=0, keepdims=True)
        p = jnp.where(m > 0.5 * MASK_NEG, jnp.exp(s - m), 0.0)
        l = jnp.sum(p, axis=0, keepdims=True)
        probs = p / jnp.maximum(l, TINY)
        o_ref[0, g] = jnp.dot(vt, probs.astype(BF16), preferred_element_type=F32).T.astype(o_ref.dtype)
        psum = probs if psum is None else psum + probs
    imp = None
    for part in _split3(psum):
        term = jnp.dot(ov_ref[...], part, preferred_element_type=F32)
        imp = term if imp is None else imp + term
    n_sel = imp.shape[0]
    pos = qi * tq + lax.broadcasted_iota(jnp.int32, (1, tq), 1)
    own = pos // SEL_BLOCK
    jidx = lax.broadcasted_iota(jnp.int32, imp.shape, 0)
    vals = jnp.where(jidx < own, imp, -1.0)
    chosen = _rank_select(vals, own, n_sel, SEL_TOPK - 1) | (jidx == own)
    sel_ref[0, 0] = jnp.where(chosen, 0.0, SEL_NEG).astype(sel_ref.dtype)


def _cmp_attn(P, kv_cmp, vt_cmp, bias_c, overlap_t, batch, seq):
    tq = 256
    n_cmp = kv_cmp.shape[2]
    n_sel = overlap_t.shape[0]
    kern = functools.partial(_cmp_attn_kernel, tq=tq)
    return pl.pallas_call(
        kern,
        grid=(NSA_KV_HEADS, seq // tq, batch),
        in_specs=[
            pl.BlockSpec((1, NSA_GROUP, tq, LANES), lambda kh, qi, b: (b, kh, qi, 0)),
            pl.BlockSpec((1, 1, n_cmp, LANES), lambda kh, qi, b: (0, b * NSA_KV_HEADS + kh, 0, 0)),
            pl.BlockSpec((1, LANES, n_cmp), lambda kh, qi, b: (b * NSA_KV_HEADS + kh, 0, 0)),
            pl.BlockSpec((NSA_GROUP, n_cmp, tq), lambda kh, qi, b: (kh, 0, qi)),
            pl.BlockSpec((n_sel, n_cmp), lambda kh, qi, b: (0, 0)),
        ],
        out_specs=[
            pl.BlockSpec((1, NSA_GROUP, tq, LANES), lambda kh, qi, b: (b, kh, qi, 0)),
            pl.BlockSpec((1, 1, n_sel, tq), lambda kh, qi, b: (b, kh, 0, qi)),
        ],
        out_shape=[
            jax.ShapeDtypeStruct((batch, N_HEADS, seq, LANES), F32),
            jax.ShapeDtypeStruct((batch, NSA_KV_HEADS, n_sel, seq), BF16),
        ],
        compiler_params=pltpu.CompilerParams(
            dimension_semantics=("parallel", "parallel", "arbitrary"),
            vmem_limit_bytes=_vmem_limit([NSA_GROUP * tq * LANES * 2, 2 * n_cmp * LANES * 2,
                                          NSA_GROUP * tq * n_cmp * 4, NSA_GROUP * tq * LANES * 4, n_sel * tq * 2],
                                         16 * tq * n_cmp * 4)),
        name="nsa_cmp_attn",
    )(P, kv_cmp, vt_cmp, bias_c, overlap_t)


def _nsa_merge_kernel(c_ref, s_ref, w_ref, g_ref, o_ref):
    gates = jax.nn.sigmoid(g_ref[0])
    for h in range(N_HEADS):
        o = (gates[:, 3 * h:3 * h + 1] * c_ref[0, h] + gates[:, 3 * h + 1:3 * h + 2] * s_ref[0, h]
             + gates[:, 3 * h + 2:3 * h + 3] * w_ref[0, h])
        o_ref[0, h] = o.astype(o_ref.dtype)


def _nsa_merge(o_cmp, o_sel, o_win, gate_logits, batch, seq):
    ts = 256
    spec = pl.BlockSpec((1, N_HEADS, ts, LANES), lambda b, s: (b, 0, s, 0))
    return pl.pallas_call(
        _nsa_merge_kernel,
        grid=(batch, seq // ts),
        in_specs=[spec, spec, spec, pl.BlockSpec((1, ts, LANES), lambda b, s: (b, s, 0))],
        out_specs=spec,
        out_shape=jax.ShapeDtypeStruct((batch, N_HEADS, seq, LANES), BF16),
        compiler_params=pltpu.CompilerParams(
            dimension_semantics=("parallel", "parallel"),
            vmem_limit_bytes=_vmem_limit([3 * N_HEADS * ts * LANES * 4, ts * LANES * 4, N_HEADS * ts * LANES * 2],
                                         4 * ts * LANES * 4)),
        name="nsa_merge",
    )(o_cmp, o_sel, o_win, gate_logits.reshape(batch, seq, LANES))


def _nsa_mixer(P, gate_logits, cmp_pos, cmp_w1, cmp_w2, k_gain, rel_table, batch, seq):
    KH = NSA_KV_HEADS
    n_chunk = seq // CMP_STRIDE
    n_sel = seq // SEL_BLOCK

    raw = P[:, N_HEADS:N_HEADS + 4].reshape(batch, 2, 2, seq, 2, HEAD_DIM)
    raw = raw.transpose(1, 0, 2, 4, 3, 5).reshape(2, batch * KH, n_chunk, CMP_STRIDE * HEAD_DIM)
    blocks = jnp.concatenate([raw, jnp.roll(raw, -1, axis=2)], axis=-1)
    pos = jnp.broadcast_to(cmp_pos.reshape(2, 1, CMP_BLOCK * HEAD_DIM), (2, 8, CMP_BLOCK * HEAD_DIM)).astype(BF16)
    w2 = jnp.pad(cmp_w2, ((0, 0), (0, 0), (0, LANES - HEAD_DIM))).astype(BF16)
    k_gain_pad = jnp.pad(k_gain, (0, LANES - HEAD_DIM)).reshape(1, LANES)
    kv_cmp = _compress(blocks, cmp_w1.astype(BF16), w2, pos, k_gain_pad)

    d = np.arange(2 * seq) - (seq + CMP_BLOCK - 1)
    w = _dist_values(rel_table, d, d >= 0)
    bias_c = _skew(w, n_chunk, CMP_STRIDE, seq, 1, seq, mask_last_row=True)[:, 0]
    c_start = np.arange(n_chunk)[None, :] * CMP_STRIDE
    s_start = np.arange(n_sel)[:, None] * SEL_BLOCK
    overlap_t = ((c_start < s_start + SEL_BLOCK) & (c_start + CMP_BLOCK > s_start)
                 & (np.arange(n_chunk)[None, :] < n_chunk - 1))
    overlap_t = jnp.asarray(overlap_t.astype(np.float32), BF16)
    o_cmp, sel_neg = _cmp_attn(P, kv_cmp, jnp.swapaxes(kv_cmp[1], -1, -2), bias_c, overlap_t, batch, seq)

    lane = jnp.arange(LANES)
    sel_q = jnp.transpose(sel_neg, (0, 1, 3, 2))
    sel_q = jnp.pad(sel_q, ((0, 0), (0, 0), (0, 0), (HEAD_DIM, LANES - HEAD_DIM - n_sel)))
    sel_q = jnp.repeat(sel_q, NSA_GROUP, axis=1)
    q_aug = jnp.where(lane < HEAD_DIM, P[:, :N_HEADS], sel_q)
    onehot = (np.arange(seq)[:, None] // SEL_BLOCK == np.arange(LANES - HEAD_DIM)[None, :]).astype(np.float32)
    onehot = jnp.asarray(np.pad(onehot, ((0, 0), (HEAD_DIM, 0))), BF16)
    k_aug = jnp.where(lane < HEAD_DIM, P[:, 20:24], onehot[None, None])
    bias_sel = _tile_bias(rel_table, DENSE_TILE, 10, 1, None)
    o_sel = _dense_attn(q_aug, 0, k_aug, 0, _value_tiles(P[:, 24:28], DENSE_TILE), 0, bias_sel, n_kv_heads=KH, group=NSA_GROUP, tile=DENSE_TILE,
                        n_kv=1, q_chunk=DENSE_TILE, n_back=None, diag_causal=False, emit_stats=False,
                        out_dtype=F32, name="nsa_sel_attn")
    bias_win = _tile_bias(rel_table, DENSE_TILE, 3, 1, NSA_WINDOW - 1)
    o_win = _dense_attn(q_aug, 0, P, 28, _value_tiles(P[:, 32:36], DENSE_TILE), 0, bias_win, n_kv_heads=KH, group=NSA_GROUP, tile=DENSE_TILE,
                        n_kv=1, q_chunk=DENSE_TILE, n_back=2, diag_causal=False, emit_stats=False,
                        out_dtype=F32, name="nsa_win_attn")
    return _nsa_merge(o_cmp, o_sel, o_win, gate_logits, batch, seq)


def _fox_decay_kernel(f_ref, bf_ref, tri_ref, eq_ref, ek_ref, carry):
    @pl.when(pl.program_id(1) == 0)
    def _():
        carry[...] = jnp.zeros_like(carry)

    z = f_ref[0] + bf_ref[...]
    log_f = jnp.minimum(z, 0.0) - jnp.log1p(jnp.exp(-jnp.abs(z)))
    c = carry[0:1, :]
    for part in _split3(log_f):
        c = c + jnp.dot(tri_ref[...], part, preferred_element_type=F32)
    ts = c.shape[0]
    carry[...] = jnp.broadcast_to(c[ts - 1:ts, :], carry.shape)
    lane = lax.broadcasted_iota(jnp.int32, (ts, LANES), 1) - HEAD_DIM
    for h in range(N_HEADS):
        c1, c2, c3 = _split3(c[:, h:h + 1])
        c1, c2, c3 = c1.astype(F32), c2.astype(F32), c3.astype(F32)
        eq = jnp.where(lane == 0, c1, jnp.where(lane == 1, c2, jnp.where(lane == 2, c3,
                       jnp.where((lane >= 3) & (lane < 6), 1.0, 0.0))))
        ek = jnp.where(lane == 3, -c1, jnp.where(lane == 4, -c2, jnp.where(lane == 5, -c3,
                       jnp.where((lane >= 0) & (lane < 3), 1.0, 0.0))))
        eq_ref[0, h] = eq.astype(eq_ref.dtype)
        ek_ref[0, h] = ek.astype(ek_ref.dtype)


def _fox_decay(f_logits, b_f, batch, seq):
    ts = 256
    tri = jnp.asarray(np.tril(np.ones((ts, ts), np.float32)), BF16)
    spec = pl.BlockSpec((1, N_HEADS, ts, LANES), lambda b, s: (b, 0, s, 0))
    return pl.pallas_call(
        _fox_decay_kernel,
        grid=(batch, seq // ts),
        in_specs=[
            pl.BlockSpec((1, ts, LANES), lambda b, s: (b, s, 0)),
            pl.BlockSpec((1, LANES), lambda b, s: (0, 0)),
            pl.BlockSpec((ts, ts), lambda b, s: (0, 0)),
        ],
        out_specs=[spec, spec],
        out_shape=[jax.ShapeDtypeStruct((batch, N_HEADS, seq, LANES), BF16)] * 2,
        scratch_shapes=[pltpu.VMEM((8, LANES), F32)],
        compiler_params=pltpu.CompilerParams(
            dimension_semantics=("parallel", "arbitrary"),
            vmem_limit_bytes=_vmem_limit([ts * LANES * 4, ts * ts * 2, 2 * N_HEADS * ts * LANES * 2],
                                         16 * ts * LANES * 4)),
        name="fox_decay",
    )(f_logits.reshape(batch, seq, LANES), jnp.pad(b_f, (0, LANES - N_HEADS)).reshape(1, LANES), tri)


def _fox_mixer(P, f_logits, b_f, batch, seq):
    eq, ek = _fox_decay(f_logits, b_f, batch, seq)
    lane = jnp.arange(LANES)
    q_aug = jnp.where(lane < HEAD_DIM, P[:, :N_HEADS], eq)
    k_aug = jnp.where(lane < HEAD_DIM, P[:, N_HEADS:2 * N_HEADS], ek)
    vt = _value_tiles(P[:, 2 * N_HEADS:], DENSE_TILE)
    return _dense_attn(q_aug, 0, k_aug, 0, vt, 0, None, n_kv_heads=N_HEADS, group=1, tile=DENSE_TILE,
                       n_kv=4, q_chunk=2 * DENSE_TILE, n_back=None, diag_causal=True, emit_stats=False,
                       out_dtype=BF16, name="fox_attn")


def _moba_gate_kernel(q_ref, k_ref, sel_ref, *, n_blk):
    k = k_ref[0, 0].astype(F32)
    S = k.shape[0]
    k_mean = jnp.sum(k.reshape(n_blk, S // n_blk, LANES), axis=1) * (1.0 / (S // n_blk))
    hi = k_mean.astype(BF16)
    lo = (k_mean - hi.astype(F32)).astype(BF16)
    q = q_ref[0, 0]
    gate = (lax.dot_general(hi, q, _NT, preferred_element_type=F32)
            + lax.dot_general(lo, q, _NT, preferred_element_type=F32))
    own = lax.broadcasted_iota(jnp.int32, (1, S), 1) // (S // n_blk)
    jidx = lax.broadcasted_iota(jnp.int32, gate.shape, 0)
    vals = jnp.where(jidx < own, gate, -jnp.inf)
    chosen = _rank_select(vals, own, n_blk, MOBA_TOPK) | (jidx == own)
    sel_ref[0, 0] = jnp.where(chosen, 0.0, SEL_NEG).astype(sel_ref.dtype)


def _moba_gate(P, batch, seq):
    n_blk = seq // MOBA_BLOCK
    return pl.pallas_call(
        functools.partial(_moba_gate_kernel, n_blk=n_blk),
        grid=(batch, N_HEADS),
        in_specs=[
            pl.BlockSpec((1, 1, seq, LANES), lambda b, h: (b, h, 0, 0)),
            pl.BlockSpec((1, 1, seq, LANES), lambda b, h: (b, N_HEADS + h, 0, 0)),
        ],
        out_specs=pl.BlockSpec((1, 1, n_blk, seq), lambda b, h: (b, h, 0, 0)),
        out_shape=jax.ShapeDtypeStruct((batch, N_HEADS, n_blk, seq), BF16),
        compiler_params=pltpu.CompilerParams(
            dimension_semantics=("parallel", "parallel"),
            vmem_limit_bytes=_vmem_limit([2 * seq * LANES * 2, n_blk * seq * 2], 2 * seq * LANES * 4 + 24 * n_blk * seq * 4)),
        name="moba_gate",
    )(P, P)


def _moba_mixer(P, rel_table, batch, seq):
    n_blk = seq // MOBA_BLOCK
    sel_neg = _moba_gate(P, batch, seq)
    lane = jnp.arange(LANES)
    sel_q = jnp.pad(jnp.transpose(sel_neg, (0, 1, 3, 2)),
                    ((0, 0), (0, 0), (0, 0), (HEAD_DIM, LANES - HEAD_DIM - n_blk)))
    q_aug = jnp.where(lane < HEAD_DIM, P[:, :N_HEADS], sel_q)
    onehot = (np.arange(seq)[:, None] // MOBA_BLOCK == np.arange(LANES - HEAD_DIM)[None, :]).astype(np.float32)
    onehot = jnp.asarray(np.pad(onehot, ((0, 0), (HEAD_DIM, 0))), BF16)
    k_aug = jnp.where(lane < HEAD_DIM, P[:, N_HEADS:2 * N_HEADS], onehot[None, None])
    bias = _tile_bias(rel_table, DENSE_TILE, 10, 1, None)
    vt = _value_tiles(P[:, 2 * N_HEADS:], DENSE_TILE)
    return _dense_attn(q_aug, 0, k_aug, 0, vt, 0, bias, n_kv_heads=N_HEADS, group=1, tile=DENSE_TILE,
                       n_kv=4, q_chunk=2 * DENSE_TILE, n_back=None, diag_causal=False, emit_stats=False,
                       out_dtype=BF16, name="moba_attn")


def _pad_head_cols(w, n_heads):
    w = w.reshape(w.shape[0], n_heads, HEAD_DIM)
    return jnp.pad(w, ((0, 0), (0, 0), (0, LANES - HEAD_DIM))).reshape(w.shape[0], n_heads * LANES)


def _head_vec(gain, n_heads, scale):
    v = jnp.pad(gain * scale, (0, LANES - HEAD_DIM))
    return jnp.tile(v, n_heads)


def _qkv_columns(w_q, w_k, w_v, q_gain, k_gain):
    w = jnp.concatenate([_pad_head_cols(w_q, N_HEADS), _pad_head_cols(w_k, N_HEADS), _pad_head_cols(w_v, N_HEADS)],
                        axis=1).astype(BF16)
    ones = jnp.ones((N_HEADS * LANES,), F32)
    gain = jnp.concatenate([_head_vec(q_gain, N_HEADS, QK_SCALE), _head_vec(k_gain, N_HEADS, 1.0), ones])
    flag = jnp.concatenate([ones, ones, 0.0 * ones])
    return w, gain.reshape(1, -1), flag.reshape(1, -1)


def _dilated_layer(x2, norm_g, w, qg, kg, rel_table, batch, seq):
    D = D_MODEL
    w_pad, gain, flag = _qkv_columns(w[:, :D], w[:, D:2 * D], w[:, 2 * D:], qg, kg)
    P = _proj(x2, norm_g, w_pad, gain, flag, None, batch, seq)
    return _dilated_mixer(P, rel_table, batch, seq)


def _nsa_layer(x2, norm_g, w, cmp_pos, cmp_w1, cmp_w2, qg, kg, rel_table, batch, seq):
    D = D_MODEL
    kvd = NSA_KV_HEADS * HEAD_DIM
    seg = [w[:, D + i * kvd:D + (i + 1) * kvd] for i in range(6)]
    w_pad = jnp.concatenate([_pad_head_cols(w[:, :D], N_HEADS), seg[0], seg[1]]
                            + [_pad_head_cols(s, NSA_KV_HEADS) for s in seg[2:]], axis=1).astype(BF16)
    one4 = jnp.ones((NSA_KV_HEADS * LANES,), F32)
    kvec = _head_vec(kg, NSA_KV_HEADS, 1.0)
    gain = jnp.concatenate([_head_vec(qg, N_HEADS, QK_SCALE), one4, kvec, one4, kvec, one4]).reshape(1, -1)
    flag = jnp.concatenate([jnp.ones((N_HEADS * LANES,), F32), 0 * one4, one4, 0 * one4, one4,
                            0 * one4]).reshape(1, -1)
    w_aux = jnp.pad(w[:, D + 6 * kvd:], ((0, 0), (0, LANES - 3 * N_HEADS))).astype(BF16)
    P, gate_logits = _proj(x2, norm_g, w_pad, gain, flag, w_aux, batch, seq)
    return _nsa_mixer(P, gate_logits, cmp_pos, cmp_w1, cmp_w2, kg, rel_table, batch, seq)


def _fox_layer(x2, norm_g, w, b_f, qg, kg, batch, seq):
    D = D_MODEL
    w_pad, gain, flag = _qkv_columns(w[:, :D], w[:, D:2 * D], w[:, 2 * D:3 * D], qg, kg)
    w_aux = jnp.pad(w[:, 3 * D:], ((0, 0), (0, LANES - N_HEADS))).astype(BF16)
    P, f_logits = _proj(x2, norm_g, w_pad, gain, flag, w_aux, batch, seq)
    return _fox_mixer(P, f_logits, b_f, batch, seq)


def _moba_layer(x2, norm_g, w, qg, kg, rel_table, batch, seq):
    D = D_MODEL
    w_pad, gain, flag = _qkv_columns(w[:, :D], w[:, D:2 * D], w[:, 2 * D:], qg, kg)
    P = _proj(x2, norm_g, w_pad, gain, flag, None, batch, seq)
    return _moba_mixer(P, rel_table, batch, seq)


def kernel(x, rel_table, attn_norm, mlp_norm, q_gain, k_gain, w_out, mlp_w_up, mlp_w_down, dsa_w_in, nsa_w_in,
           nsa_cmp_pos, nsa_cmp_w1, nsa_cmp_w2, fox_w_in, fox_b_f, moba_w_in):
    batch, seq, _ = x.shape
    depth = attn_norm.shape[0]
    x2 = x.reshape(batch * seq, D_MODEL)
    D = D_MODEL
    for layer in range(depth):
        kind, r = layer % 4, layer // 4
        qg, kg, ng = q_gain[layer], k_gain[layer], attn_norm[layer]
        if kind == 0:
            mixed = _dilated_layer(x2, ng, dsa_w_in[r], qg, kg, rel_table, batch, seq)
        elif kind == 1:
            mixed = _nsa_layer(x2, ng, nsa_w_in[r], nsa_cmp_pos[r], nsa_cmp_w1[r], nsa_cmp_w2[r], qg, kg,
                               rel_table, batch, seq)
        elif kind == 2:
            mixed = _fox_layer(x2, ng, fox_w_in[r], fox_b_f[r], qg, kg, batch, seq)
        else:
            mixed = _moba_layer(x2, ng, moba_w_in[r], qg, kg, rel_table, batch, seq)
        w_out_pad = jnp.pad(w_out[layer].reshape(N_HEADS, HEAD_DIM, D), ((0, 0), (0, LANES - HEAD_DIM), (0, 0)))
        x2 = _out_mlp(x2, mixed, w_out_pad.astype(BF16), mlp_norm[layer], mlp_w_up[layer].astype(BF16),
                      mlp_w_down[layer].astype(BF16), batch, seq)
    return x2.reshape(batch, seq, D_MODEL)
```

```python
import functools
import math

import jax
import jax.numpy as jnp
import numpy as np
from jax import lax
from jax.experimental import pallas as pl
from jax.experimental.pallas import tpu as pltpu

F32 = jnp.float32
BF16 = jnp.bfloat16

D_MODEL = 1024
N_HEADS = 16
HEAD_DIM = 64
D_FF = 4 * D_MODEL
RMS_EPS = 1e-6
MASK_NEG = -1e30
TINY = 1e-30
QK_SCALE = HEAD_DIM ** -0.5

REL_BUCKETS = 32
REL_MAX_EXACT = REL_BUCKETS // 2
REL_MAX_DIST = 2048

DILATED_PATTERNS = ((128, 1), (512, 4), (2048, 16))
BAND_BLOCK = 128

NSA_KV_HEADS = 4
NSA_GROUP = N_HEADS // NSA_KV_HEADS
CMP_STRIDE = 16
CMP_BLOCK = 2 * CMP_STRIDE
CMP_HIDDEN = 256
SEL_BLOCK = 64
SEL_TOPK = 16
NSA_WINDOW = 512

MOBA_BLOCK = 256
MOBA_TOPK = 3

LANES = 128
VMEM_BYTES_V7X = 64 * 1024 * 1024
VMEM_HEADROOM = 6 * 1024 * 1024

DENSE_TILE = 256
SEL_NEG = -(2.0 ** 100)

_NT = (((1,), (1,)), ((), ()))


def _vmem_limit(block_bytes, temp_bytes):
    need = 2 * int(sum(block_bytes)) + int(temp_bytes) + (2 << 20)
    return int(min(need, VMEM_BYTES_V7X - VMEM_HEADROOM))


def _split3(v):
    a = v.astype(BF16)
    r = v - a.astype(F32)
    b = r.astype(BF16)
    c = (r - b.astype(F32)).astype(BF16)
    return a, b, c


def _proj_kernel(*refs, has_aux):
    if has_aux:
        x_ref, g_ref, w_ref, cg_ref, cf_ref, bd_ref, wa_ref, o_ref, oa_ref, h_scr = refs
    else:
        x_ref, g_ref, w_ref, cg_ref, cf_ref, bd_ref, o_ref, h_scr = refs

    @pl.when(pl.program_id(1) == 0)
    def _():
        x = x_ref[...]
        ms = jnp.mean(x * x, axis=-1, keepdims=True)
        h_scr[...] = (x * lax.rsqrt(ms + RMS_EPS) * g_ref[...]).astype(BF16)
        if has_aux:
            oa_ref[...] = jnp.dot(h_scr[...], wa_ref[...], preferred_element_type=F32)

    y = jnp.dot(h_scr[...], w_ref[...], preferred_element_type=F32)
    ms = jnp.dot((y * y).astype(BF16), bd_ref[...], preferred_element_type=F32)
    rs = lax.rsqrt(ms + RMS_EPS)
    out = y * jnp.where(cf_ref[...] > 0.5, rs, 1.0) * cg_ref[...]
    lane = lax.broadcasted_iota(jnp.int32, (out.shape[0], LANES), 1)
    for pair in range(out.shape[1] // LANES):
        both = out[:, pair * LANES:(pair + 1) * LANES]
        o_ref[0, 2 * pair] = jnp.where(lane < HEAD_DIM, both, 0.0).astype(o_ref.dtype)
        o_ref[0, 2 * pair + 1] = jnp.where(lane < HEAD_DIM, pltpu.roll(both, HEAD_DIM, 1), 0.0).astype(o_ref.dtype)


def _proj(x2, norm_g, w, col_gain, col_flag, w_aux, batch, seq):
    T = x2.shape[0]
    N = w.shape[1]
    tm, tn = 1024, 2 * LANES
    spb = seq // tm
    bd = np.zeros((tn, tn), np.float32)
    for a in range(tn // HEAD_DIM):
        bd[a * HEAD_DIM:(a + 1) * HEAD_DIM, a * HEAD_DIM:(a + 1) * HEAD_DIM] = 1.0 / HEAD_DIM
    bd = jnp.asarray(bd, BF16)
    has_aux = w_aux is not None
    in_specs = [
        pl.BlockSpec((tm, D_MODEL), lambda i, j: (i, 0)),
        pl.BlockSpec((1, D_MODEL), lambda i, j: (0, 0)),
        pl.BlockSpec((D_MODEL, tn), lambda i, j: (0, j)),
        pl.BlockSpec((1, tn), lambda i, j: (0, j)),
        pl.BlockSpec((1, tn), lambda i, j: (0, j)),
        pl.BlockSpec((tn, tn), lambda i, j: (0, 0)),
    ]
    args = [x2, norm_g.reshape(1, D_MODEL), w, col_gain, col_flag, bd]
    hpt = tn // HEAD_DIM
    out_shape = [jax.ShapeDtypeStruct((batch, N // HEAD_DIM, seq, LANES), BF16)]
    out_specs = [pl.BlockSpec((1, hpt, tm, LANES), lambda i, j: (i // spb, j, i % spb, 0))]
    if has_aux:
        in_specs.append(pl.BlockSpec((D_MODEL, LANES), lambda i, j: (0, 0)))
        args.append(w_aux)
        out_shape.append(jax.ShapeDtypeStruct((T, LANES), F32))
        out_specs.append(pl.BlockSpec((tm, LANES), lambda i, j: (i, 0)))
    blocks = [tm * D_MODEL * 4, D_MODEL * tn * 2, hpt * tm * LANES * 2, tn * tn * 2, tm * LANES * 4,
              D_MODEL * LANES * 2]
    res = pl.pallas_call(
        functools.partial(_proj_kernel, has_aux=has_aux),
        grid=(T // tm, N // tn),
        in_specs=in_specs,
        out_specs=out_specs,
        out_shape=out_shape,
        scratch_shapes=[pltpu.VMEM((tm, D_MODEL), BF16)],
        compiler_params=pltpu.CompilerParams(
            dimension_semantics=("parallel", "arbitrary"),
            vmem_limit_bytes=_vmem_limit(blocks, tm * D_MODEL * 2 + 6 * tm * tn * 4)),
        name="proj",
    )(*args)
    return res if has_aux else res[0]


def _out_mlp_kernel(x_ref, m_ref, wo_ref, g_ref, wu_ref, wd_ref, o_ref, h_scr):
    @pl.when(pl.program_id(1) == 0)
    def _():
        x1 = x_ref[...] + jnp.dot(m_ref[...], wo_ref[...], preferred_element_type=F32)
        o_ref[...] = x1
        ms = jnp.mean(x1 * x1, axis=-1, keepdims=True)
        h_scr[...] = (x1 * lax.rsqrt(ms + RMS_EPS) * g_ref[...]).astype(BF16)

    u = jnp.dot(h_scr[...], wu_ref[...], preferred_element_type=F32)
    a = jnp.square(jnp.maximum(u, 0.0)).astype(BF16)
    o_ref[...] += jnp.dot(a, wd_ref[...], preferred_element_type=F32)


def _out_mlp(x2, mixed, w_out, norm_g, w_up, w_down):
    T = x2.shape[0]
    tm, tf = 1024, 512
    blocks = [tm * D_MODEL * 4, tm * D_MODEL * 2, D_MODEL * D_MODEL * 2,
              D_MODEL * tf * 2, tf * D_MODEL * 2, tm * D_MODEL * 4]
    return pl.pallas_call(
        _out_mlp_kernel,
        grid=(T // tm, D_FF // tf),
        in_specs=[
            pl.BlockSpec((tm, D_MODEL), lambda i, f: (i, 0)),
            pl.BlockSpec((tm, D_MODEL), lambda i, f: (i, 0)),
            pl.BlockSpec((D_MODEL, D_MODEL), lambda i, f: (0, 0)),
            pl.BlockSpec((1, D_MODEL), lambda i, f: (0, 0)),
            pl.BlockSpec((D_MODEL, tf), lambda i, f: (0, f)),
            pl.BlockSpec((tf, D_MODEL), lambda i, f: (f, 0)),
        ],
        out_specs=pl.BlockSpec((tm, D_MODEL), lambda i, f: (i, 0)),
        out_shape=jax.ShapeDtypeStruct((T, D_MODEL), F32),
        scratch_shapes=[pltpu.VMEM((tm, D_MODEL), BF16)],
        compiler_params=pltpu.CompilerParams(
            dimension_semantics=("parallel", "arbitrary"),
            vmem_limit_bytes=_vmem_limit(blocks, tm * D_MODEL * 2 + 3 * tm * tf * 4 + 2 * tm * D_MODEL * 4)),
        name="out_mlp",
    )(x2, mixed, w_out, norm_g.reshape(1, D_MODEL), w_up, w_down)


def _dense_attn_kernel(*refs, tile, n_kv, group, n_qt, n_back, has_bias, n_bias, diag_causal, emit_stats):
    if has_bias:
        q_ref, k_ref, vt_ref, b_ref, o_ref, acc_scr = refs
    else:
        q_ref, k_ref, vt_ref, o_ref, acc_scr = refs
        b_ref = None
    chunk = pl.program_id(2)
    n_streams = n_kv * group
    channel = lax.broadcasted_iota(jnp.int32, (LANES, tile), 0)
    if diag_causal:
        visible = (lax.broadcasted_iota(jnp.int32, (tile, tile), 0)
                   <= lax.broadcasted_iota(jnp.int32, (tile, tile), 1))

    def one_query_tile(t, _):
        qi = chunk * n_qt + t
        row0 = pl.multiple_of(t * tile, tile)
        qs = [q_ref[0, h, pl.ds(row0, tile), :].astype(F32).T.astype(BF16) for h in range(n_streams)]

        def tile_step(d, carry):
            first = carry is None
            start = pl.multiple_of((qi - d) * tile, tile)
            scores = []
            for h in range(n_streams):
                s = jnp.dot(k_ref[0, h // group, pl.ds(start, tile), :], qs[h],
                            preferred_element_type=F32)
                if has_bias:
                    s = s + b_ref[h, jnp.minimum(d, n_bias - 1)]
                scores.append(s)
            stats = []
            for h in range(n_streams):
                s = scores[h]
                if first:
                    if diag_causal:
                        s = jnp.where(visible, s, MASK_NEG)
                    m = jnp.max(s, axis=0, keepdims=True)
                    p = jnp.exp(s - m)
                    stats.append((m, None, jnp.sum(p, axis=0, keepdims=True), p.astype(BF16)))
                else:
                    m0, l0 = carry[h]
                    m = jnp.maximum(m0, jnp.max(s, axis=0, keepdims=True))
                    alpha = jnp.exp(m0 - m)
                    p = jnp.exp(s - m)
                    stats.append((m, alpha, alpha * l0 + jnp.sum(p, axis=0, keepdims=True), p.astype(BF16)))
            for h in range(n_streams):
                alpha, p = stats[h][1], stats[h][3]
                pv = jnp.dot(vt_ref[0, h // group, qi - d], p, preferred_element_type=F32)
                acc_scr[h] = pv if first else alpha * acc_scr[h] + pv
            return tuple((m, l) for m, _, l, _ in stats)

        carry = tile_step(0, None)
        n_steps = qi + 1 if n_back is None else jnp.minimum(qi, n_back) + 1
        carry = lax.fori_loop(1, n_steps, tile_step, carry)
        if emit_stats:
            for h in range(n_streams):
                m, l = carry[h]
                out_t = jnp.where(channel == HEAD_DIM, m, jnp.where(channel == HEAD_DIM + 1, l, acc_scr[h]))
                o_ref[0, h, pl.ds(row0, tile), :] = out_t.T.astype(o_ref.dtype)
        else:
            outs = [acc_scr[h][:HEAD_DIM] / jnp.maximum(carry[h][1], TINY) for h in range(n_streams)]
            for pair in range(n_streams // 2):
                both = jnp.concatenate([outs[2 * pair], outs[2 * pair + 1]], axis=0)
                o_ref[0, pl.ds(row0, tile), pair * LANES:(pair + 1) * LANES] = both.T.astype(o_ref.dtype)
        return 0

    lax.fori_loop(0, n_qt, one_query_tile, 0)


def _value_tiles(v, tile):
    B, H, S, _ = v.shape
    return jnp.swapaxes(v.reshape(B, H, S // tile, tile, LANES), -1, -2)


def _dense_attn(q_arr, q_off, k_arr, k_off, vt_arr, v_off, bias, *, n_kv_heads, group, tile, n_kv, q_chunk,
                n_back, diag_causal, emit_stats, out_dtype, name):
    B, _, S, _ = q_arr.shape
    n_q_heads = n_kv_heads * group
    nq = n_kv * group
    assert q_off % nq == 0 and k_off % n_kv == 0 and v_off % n_kv == 0
    assert S % q_chunk == 0 and q_chunk % tile == 0 and n_kv_heads % n_kv == 0
    n_qt = q_chunk // tile
    has_bias = bias is not None
    n_bias = bias.shape[1] if has_bias else 0
    in_specs = [
        pl.BlockSpec((1, nq, q_chunk, LANES), lambda u, b, c: (b, q_off // nq + u, c, 0)),
        pl.BlockSpec((1, n_kv, S, LANES), lambda u, b, c: (b, k_off // n_kv + u, 0, 0)),
        pl.BlockSpec((1, n_kv, S // tile, LANES, tile), lambda u, b, c: (b, v_off // n_kv + u, 0, 0, 0)),
    ]
    args = [q_arr, k_arr, vt_arr]
    out_bytes = jnp.dtype(out_dtype).itemsize
    blocks = [nq * q_chunk * LANES * 2, 2 * n_kv * S * LANES * 2, nq * q_chunk * LANES * out_bytes]
    if has_bias:
        in_specs.append(pl.BlockSpec((nq, n_bias, tile, tile), lambda u, b, c: (u, 0, 0, 0)))
        args.append(bias)
        blocks.append(nq * n_bias * tile * tile * 4)
    kern = functools.partial(_dense_attn_kernel, tile=tile, n_kv=n_kv, group=group, n_qt=n_qt, n_back=n_back,
                             has_bias=has_bias, n_bias=n_bias, diag_causal=diag_causal, emit_stats=emit_stats)
    if emit_stats:
        out_specs = pl.BlockSpec((1, nq, q_chunk, LANES), lambda u, b, c: (b, u, c, 0))
        out_shape = jax.ShapeDtypeStruct((B, n_q_heads, S, LANES), out_dtype)
    else:
        assert nq % 2 == 0
        out_specs = pl.BlockSpec((1, q_chunk, nq * HEAD_DIM), lambda u, b, c: (b, c, u))
        out_shape = jax.ShapeDtypeStruct((B, S, n_q_heads * HEAD_DIM), out_dtype)
    return pl.pallas_call(
        kern,
        grid=(n_kv_heads // n_kv, B, S // q_chunk),
        in_specs=in_specs,
        out_specs=out_specs,
        out_shape=out_shape,
        scratch_shapes=[pltpu.VMEM((nq, LANES, tile), F32)],
        compiler_params=pltpu.CompilerParams(
            dimension_semantics=("parallel", "parallel", "arbitrary"),
            vmem_limit_bytes=_vmem_limit(blocks, nq * (4 * tile * tile + 3 * LANES * tile) * 4)),
        name=name,
    )(*args)


def _bucket_np(dist):
    d = np.maximum(dist, 0)
    df = np.maximum(d.astype(np.float32), np.float32(1.0))
    large = REL_MAX_EXACT + (np.log(df / np.float32(REL_MAX_EXACT))
                             / np.float32(math.log(REL_MAX_DIST / REL_MAX_EXACT))
                             * np.float32(REL_BUCKETS - REL_MAX_EXACT)).astype(np.int32)
    large = np.minimum(large, REL_BUCKETS - 1)
    return np.where(d < REL_MAX_EXACT, d, large).astype(np.int32)


def _dist_values(rel_table, dist, ok):
    vals = rel_table[jnp.asarray(_bucket_np(dist))]
    return jnp.where(jnp.asarray(ok)[:, None], vals, MASK_NEG).T


def _skew_kernel(w_ref, o_ref, *, step, col0, mask_last_row):
    _, n_out, n_rows, width = o_ref.shape
    x = jnp.broadcast_to(w_ref[0], (n_rows, w_ref.shape[2]))
    x = pltpu.roll(x, 0, 1, stride=step, stride_axis=0)
    if mask_last_row:
        row = lax.broadcasted_iota(jnp.int32, x.shape, 0)
        x = jnp.where(row == n_rows - 1, MASK_NEG, x)
    for a in range(n_out):
        o_ref[0, a] = x[:, col0 + a * width:col0 + (a + 1) * width]


def _skew(w, n_rows, step, col0, n_out, width, mask_last_row=False):
    H, W = w.shape
    assert W % LANES == 0 and col0 % LANES == 0 and width % LANES == 0 and col0 + n_out * width <= W
    assert col0 - step * (n_rows - 1) >= 0
    return pl.pallas_call(
        functools.partial(_skew_kernel, step=step, col0=col0, mask_last_row=mask_last_row),
        grid=(H,),
        in_specs=[pl.BlockSpec((1, 1, W), lambda h: (h, 0, 0))],
        out_specs=pl.BlockSpec((1, n_out, n_rows, width), lambda h: (h, 0, 0, 0)),
        out_shape=jax.ShapeDtypeStruct((H, n_out, n_rows, width), F32),
        compiler_params=pltpu.CompilerParams(
            dimension_semantics=("parallel",),
            vmem_limit_bytes=_vmem_limit([W * 4, n_out * n_rows * width * 4], 3 * n_rows * W * 4)),
        name="bias_skew",
    )(w.reshape(H, 1, W))


def _tile_bias(rel_table, tile, n_tiles, dist_scale, max_dist):
    d = np.arange((n_tiles + 1) * tile) - tile
    ok = d >= 0
    if max_dist is not None:
        ok &= d <= max_dist
    w = _dist_values(rel_table, d * dist_scale, ok)
    return _skew(w, tile, 1, tile, n_tiles, tile)


def _merge3_kernel(a_ref, b_ref, c_ref, o_ref):
    merged = []
    for h in range(2):
        parts = [r[0, h] for r in (a_ref, b_ref, c_ref)]
        ms = [p[:, HEAD_DIM:HEAD_DIM + 1] for p in parts]
        ls = [p[:, HEAD_DIM + 1:HEAD_DIM + 2] for p in parts]
        m_all = jnp.maximum(jnp.maximum(ms[0], ms[1]), ms[2])
        ws = [jnp.exp(m - m_all) for m in ms]
        num = ws[0] * parts[0] + ws[1] * parts[1] + ws[2] * parts[2]
        den = ws[0] * ls[0] + ws[1] * ls[1] + ws[2] * ls[2]
        merged.append(num / jnp.maximum(den, TINY))
    lane = lax.broadcasted_iota(jnp.int32, merged[0].shape, 1)
    o_ref[0] = jnp.where(lane < HEAD_DIM, merged[0], pltpu.roll(merged[1], HEAD_DIM, 1)).astype(o_ref.dtype)


def _merge3(parts):
    B, H, S, _ = parts[0].shape
    ts = 2048
    spec = pl.BlockSpec((1, 2, ts, LANES), lambda b, p, s: (b, p, s, 0))
    return pl.pallas_call(
        _merge3_kernel,
        grid=(B, H // 2, S // ts),
        in_specs=[spec, spec, spec],
        out_specs=pl.BlockSpec((1, ts, LANES), lambda b, p, s: (b, s, p)),
        out_shape=jax.ShapeDtypeStruct((B, S, H * HEAD_DIM), BF16),
        compiler_params=pltpu.CompilerParams(
            dimension_semantics=("parallel", "parallel", "parallel"),
            vmem_limit_bytes=_vmem_limit([6 * ts * LANES * 4, ts * LANES * 2], 12 * ts * LANES * 4)),
        name="dilated_merge",
    )(*parts)


def _dilated_mixer(P, rel_table, batch, seq):
    parts = []
    for window, dil in DILATED_PATTERNS:
        n = seq // dil
        if dil == 1:
            Pr = P
        else:
            Pr = P.reshape(batch, 3 * N_HEADS, n, dil, LANES).transpose(0, 3, 1, 2, 4)
            Pr = Pr.reshape(batch * dil, 3 * N_HEADS, n, LANES)
        bias = _tile_bias(rel_table, BAND_BLOCK, 2, dil, window // dil)
        vt = _value_tiles(Pr[:, 2 * N_HEADS:], BAND_BLOCK)
        o = _dense_attn(Pr, 0, Pr, N_HEADS, vt, 0, bias, n_kv_heads=N_HEADS, group=1, tile=BAND_BLOCK,
                        n_kv=4, q_chunk=min(n, 1024), n_back=1, diag_causal=False, emit_stats=True,
                        out_dtype=F32, name=f"dilated_attn_d{dil}")
        if dil > 1:
            o = o.reshape(batch, dil, N_HEADS, n, LANES).transpose(0, 2, 3, 1, 4).reshape(batch, N_HEADS, seq, LANES)
        parts.append(o)
    return _merge3(parts)


def _compress_kernel(x_ref, w1_ref, w2_ref, pos_ref, g_ref, o_ref):
    posw = jnp.dot(pos_ref[0], w1_ref[0], preferred_element_type=F32)[0:1]
    pre = jnp.dot(x_ref[0, 0], w1_ref[0], preferred_element_type=F32) + posw
    hid = 0.5 * pre * (1.0 + jnp.tanh(math.sqrt(2.0 / math.pi) * (pre + 0.044715 * (pre * pre * pre))))
    out = jnp.dot(hid.astype(BF16), w2_ref[0], preferred_element_type=F32)
    ms = jnp.sum(out * out, axis=-1, keepdims=True) * (1.0 / HEAD_DIM)
    normed = out * lax.rsqrt(ms + RMS_EPS) * g_ref[...]
    o_ref[0, 0] = jnp.where(pl.program_id(0) == 0, normed, out).astype(o_ref.dtype)


def _compress(blocks, w1, w2, pos, k_gain_pad):
    _, BK, n, K = blocks.shape
    return pl.pallas_call(
        _compress_kernel,
        grid=(2, BK),
        in_specs=[
            pl.BlockSpec((1, 1, n, K), lambda a, i: (a, i, 0, 0)),
            pl.BlockSpec((1, K, CMP_HIDDEN), lambda a, i: (a, 0, 0)),
            pl.BlockSpec((1, CMP_HIDDEN, LANES), lambda a, i: (a, 0, 0)),
            pl.BlockSpec((1, 8, K), lambda a, i: (a, 0, 0)),
            pl.BlockSpec((1, LANES), lambda a, i: (0, 0)),
        ],
        out_specs=pl.BlockSpec((1, 1, n, LANES), lambda a, i: (a, i, 0, 0)),
        out_shape=jax.ShapeDtypeStruct((2, BK, n, LANES), BF16),
        compiler_params=pltpu.CompilerParams(
            dimension_semantics=("parallel", "parallel"),
            vmem_limit_bytes=_vmem_limit([n * K * 2, K * CMP_HIDDEN * 2, CMP_HIDDEN * LANES * 2, 8 * K * 2,
                                          n * LANES * 2], 8 * n * CMP_HIDDEN * 4)),
        name="nsa_compress",
    )(blocks, w1, w2, pos, k_gain_pad)


def _rank_select(vals, own_row, n_blocks, top):
    jidx = lax.broadcasted_iota(jnp.int32, vals.shape, 0)
    cnt = jnp.zeros(vals.shape, F32)
    for j in range(n_blocks):
        row = vals[j:j + 1, :]
        ahead = (row > vals) | ((row == vals) & (jidx > j))
        cnt = cnt + jnp.where(ahead, 1.0, 0.0)
    return (jidx < own_row) & (cnt < top)


def _cmp_attn_kernel(q_ref, k_ref, vt_ref, b_ref, ov_ref, o_ref, qa_ref, *, tq):
    qi = pl.program_id(1)
    k = k_ref[0, 0]
    vt = vt_ref[0]
    psum = None
    outs = []
    for g in range(NSA_GROUP):
        s = lax.dot_general(k, q_ref[0, g], _NT, preferred_element_type=F32) + b_ref[g]
        m = jnp.max(s, axis=0, keepdims=True)
        p = jnp.where(m > 0.5 * MASK_NEG, jnp.exp(s - m), 0.0)
        l = jnp.sum(p, axis=0, keepdims=True)
        probs = p / jnp.maximum(l, TINY)
        outs.append(jnp.dot(vt, probs.astype(BF16), preferred_element_type=F32)[:HEAD_DIM])
        psum = probs if psum is None else psum + probs
    for pair in range(NSA_GROUP // 2):
        both = jnp.concatenate([outs[2 * pair], outs[2 * pair + 1]], axis=0)
        o_ref[0, :, pair * LANES:(pair + 1) * LANES] = both.T.astype(o_ref.dtype)
    imp = None
    for part in _split3(psum):
        term = jnp.dot(ov_ref[...], part, preferred_element_type=F32)
        imp = term if imp is None else imp + term
    n_sel = imp.shape[0]
    pos = qi * tq + lax.broadcasted_iota(jnp.int32, (1, tq), 1)
    own = pos // SEL_BLOCK
    jidx = lax.broadcasted_iota(jnp.int32, imp.shape, 0)
    vals = jnp.where(jidx < own, imp, -1.0)
    chosen = _rank_select(vals, own, n_sel, SEL_TOPK - 1) | (jidx == own)
    sel_cols = jnp.concatenate([jnp.zeros((HEAD_DIM, tq), F32), jnp.where(chosen, 0.0, SEL_NEG)], axis=0).T
    for g in range(NSA_GROUP):
        qa_ref[0, g] = (q_ref[0, g].astype(F32) + sel_cols).astype(qa_ref.dtype)


def _cmp_attn(P, kv_cmp, vt_cmp, bias_c, overlap_t, batch, seq):
    assert seq // SEL_BLOCK == LANES - HEAD_DIM
    tq = 256
    n_cmp = kv_cmp.shape[2]
    n_sel = overlap_t.shape[0]
    kern = functools.partial(_cmp_attn_kernel, tq=tq)
    return pl.pallas_call(
        kern,
        grid=(NSA_KV_HEADS, seq // tq, batch),
        in_specs=[
            pl.BlockSpec((1, NSA_GROUP, tq, LANES), lambda kh, qi, b: (b, kh, qi, 0)),
            pl.BlockSpec((1, 1, n_cmp, LANES), lambda kh, qi, b: (0, b * NSA_KV_HEADS + kh, 0, 0)),
            pl.BlockSpec((1, LANES, n_cmp), lambda kh, qi, b: (b * NSA_KV_HEADS + kh, 0, 0)),
            pl.BlockSpec((NSA_GROUP, n_cmp, tq), lambda kh, qi, b: (kh, 0, qi)),
            pl.BlockSpec((n_sel, n_cmp), lambda kh, qi, b: (0, 0)),
        ],
        out_specs=[
            pl.BlockSpec((1, tq, NSA_GROUP * HEAD_DIM), lambda kh, qi, b: (b, qi, kh)),
            pl.BlockSpec((1, NSA_GROUP, tq, LANES), lambda kh, qi, b: (b, kh, qi, 0)),
        ],
        out_shape=[
            jax.ShapeDtypeStruct((batch, seq, N_HEADS * HEAD_DIM), F32),
            jax.ShapeDtypeStruct((batch, N_HEADS, seq, LANES), BF16),
        ],
        compiler_params=pltpu.CompilerParams(
            dimension_semantics=("parallel", "parallel", "arbitrary"),
            vmem_limit_bytes=_vmem_limit([NSA_GROUP * tq * LANES * 2, 2 * n_cmp * LANES * 2,
                                          NSA_GROUP * tq * n_cmp * 4, NSA_GROUP * tq * LANES * 4, n_sel * tq * 2],
                                         16 * tq * n_cmp * 4)),
        name="nsa_cmp_attn",
    )(P, kv_cmp, vt_cmp, bias_c, overlap_t)


def _nsa_merge_kernel(c_ref, s_ref, w_ref, g_ref, o_ref):
    gates = jax.nn.sigmoid(g_ref[0])
    lane = lax.broadcasted_iota(jnp.int32, gates.shape, 1)
    for pair in range(N_HEADS // 2):
        cols = slice(pair * LANES, (pair + 1) * LANES)

        def gate(branch):
            a, b = 3 * (2 * pair) + branch, 3 * (2 * pair + 1) + branch
            return jnp.where(lane < HEAD_DIM, gates[:, a:a + 1], gates[:, b:b + 1])

        o = gate(0) * c_ref[0, :, cols] + gate(1) * s_ref[0, :, cols] + gate(2) * w_ref[0, :, cols]
        o_ref[0, :, cols] = o.astype(o_ref.dtype)


def _nsa_merge(o_cmp, o_sel, o_win, gate_logits, batch, seq):
    ts = 256
    spec = pl.BlockSpec((1, ts, D_MODEL), lambda b, s: (b, s, 0))
    return pl.pallas_call(
        _nsa_merge_kernel,
        grid=(batch, seq // ts),
        in_specs=[spec, spec, spec, pl.BlockSpec((1, ts, LANES), lambda b, s: (b, s, 0))],
        out_specs=spec,
        out_shape=jax.ShapeDtypeStruct((batch, seq, D_MODEL), BF16),
        compiler_params=pltpu.CompilerParams(
            dimension_semantics=("parallel", "parallel"),
            vmem_limit_bytes=_vmem_limit([3 * ts * D_MODEL * 4, ts * LANES * 4, ts * D_MODEL * 2],
                                         8 * ts * LANES * 4)),
        name="nsa_merge",
    )(o_cmp, o_sel, o_win, gate_logits.reshape(batch, seq, LANES))


def _nsa_mixer(P, gate_logits, cmp_pos, cmp_w1, cmp_w2, k_gain, bias_sel, rel_table, batch, seq):
    KH = NSA_KV_HEADS
    n_chunk = seq // CMP_STRIDE
    n_sel = seq // SEL_BLOCK

    raw = P[:, N_HEADS:N_HEADS + 2 * KH, :, :HEAD_DIM].reshape(batch, 2, KH, n_chunk, CMP_STRIDE * HEAD_DIM)
    raw = raw.transpose(1, 0, 2, 3, 4).reshape(2, batch * KH, n_chunk, CMP_STRIDE * HEAD_DIM)
    blocks = jnp.concatenate([raw, jnp.roll(raw, -1, axis=2)], axis=-1)
    pos = jnp.broadcast_to(cmp_pos.reshape(2, 1, CMP_BLOCK * HEAD_DIM), (2, 8, CMP_BLOCK * HEAD_DIM)).astype(BF16)
    w2 = jnp.pad(cmp_w2, ((0, 0), (0, 0), (0, LANES - HEAD_DIM))).astype(BF16)
    k_gain_pad = jnp.pad(k_gain, (0, LANES - HEAD_DIM)).reshape(1, LANES)
    kv_cmp = _compress(blocks, cmp_w1.astype(BF16), w2, pos, k_gain_pad)

    d = np.arange(2 * seq) - (seq + CMP_BLOCK - 1)
    w = _dist_values(rel_table, d, d >= 0)
    bias_c = _skew(w, n_chunk, CMP_STRIDE, seq, 1, seq, mask_last_row=True)[:, 0]
    c_start = np.arange(n_chunk)[None, :] * CMP_STRIDE
    s_start = np.arange(n_sel)[:, None] * SEL_BLOCK
    overlap_t = ((c_start < s_start + SEL_BLOCK) & (c_start + CMP_BLOCK > s_start)
                 & (np.arange(n_chunk)[None, :] < n_chunk - 1))
    overlap_t = jnp.asarray(overlap_t.astype(np.float32), BF16)
    o_cmp, q_aug = _cmp_attn(P, kv_cmp, jnp.swapaxes(kv_cmp[1], -1, -2), bias_c, overlap_t, batch, seq)

    lane = jnp.arange(LANES)
    onehot = (np.arange(seq)[:, None] // SEL_BLOCK == np.arange(LANES - HEAD_DIM)[None, :]).astype(np.float32)
    onehot = jnp.asarray(np.pad(onehot, ((0, 0), (HEAD_DIM, 0))), BF16)
    k_aug = jnp.where(lane < HEAD_DIM, P[:, 24:28], onehot[None, None])
    o_sel = _dense_attn(q_aug, 0, k_aug, 0, _value_tiles(P[:, 28:32], DENSE_TILE), 0, bias_sel, n_kv_heads=KH,
                        group=NSA_GROUP, tile=DENSE_TILE, n_kv=1, q_chunk=DENSE_TILE, n_back=None,
                        diag_causal=False, emit_stats=False, out_dtype=F32, name="nsa_sel_attn")
    bias_win = _tile_bias(rel_table, DENSE_TILE, 3, 1, NSA_WINDOW - 1)
    o_win = _dense_attn(q_aug, 0, P, 32, _value_tiles(P[:, 36:40], DENSE_TILE), 0, bias_win, n_kv_heads=KH,
                        group=NSA_GROUP, tile=DENSE_TILE, n_kv=1, q_chunk=DENSE_TILE, n_back=2,
                        diag_causal=False, emit_stats=False, out_dtype=F32, name="nsa_win_attn")
    return _nsa_merge(o_cmp, o_sel, o_win, gate_logits, batch, seq)


def _fox_decay_kernel(f_ref, bf_ref, tri_ref, q_ref, k_ref, eq_ref, ek_ref, carry):
    @pl.when(pl.program_id(1) == 0)
    def _():
        carry[...] = jnp.zeros_like(carry)

    z = f_ref[0] + bf_ref[...]
    log_f = jnp.minimum(z, 0.0) - jnp.log1p(jnp.exp(-jnp.abs(z)))
    c = carry[0:1, :]
    for part in _split3(log_f):
        c = c + jnp.dot(tri_ref[...], part, preferred_element_type=F32)
    ts = c.shape[0]
    carry[...] = jnp.broadcast_to(c[ts - 1:ts, :], carry.shape)
    lane = lax.broadcasted_iota(jnp.int32, (ts, LANES), 1) - HEAD_DIM
    for h in range(N_HEADS):
        c1, c2, c3 = _split3(c[:, h:h + 1])
        c1, c2, c3 = c1.astype(F32), c2.astype(F32), c3.astype(F32)
        eq = jnp.where(lane == 0, c1, jnp.where(lane == 1, c2, jnp.where(lane == 2, c3,
                       jnp.where((lane >= 3) & (lane < 6), 1.0, 0.0))))
        ek = jnp.where(lane == 3, -c1, jnp.where(lane == 4, -c2, jnp.where(lane == 5, -c3,
                       jnp.where((lane >= 0) & (lane < 3), 1.0, 0.0))))
        eq_ref[0, h] = (q_ref[0, h].astype(F32) + eq).astype(eq_ref.dtype)
        ek_ref[0, h] = (k_ref[0, h].astype(F32) + ek).astype(ek_ref.dtype)


def _fox_decay(P, f_logits, b_f, batch, seq):
    ts = 256
    tri = jnp.asarray(np.tril(np.ones((ts, ts), np.float32)), BF16)
    spec = pl.BlockSpec((1, N_HEADS, ts, LANES), lambda b, s: (b, 0, s, 0))
    return pl.pallas_call(
        _fox_decay_kernel,
        grid=(batch, seq // ts),
        in_specs=[
            pl.BlockSpec((1, ts, LANES), lambda b, s: (b, s, 0)),
            pl.BlockSpec((1, LANES), lambda b, s: (0, 0)),
            pl.BlockSpec((ts, ts), lambda b, s: (0, 0)),
            spec,
            pl.BlockSpec((1, N_HEADS, ts, LANES), lambda b, s: (b, 1, s, 0)),
        ],
        out_specs=[spec, spec],
        out_shape=[jax.ShapeDtypeStruct((batch, N_HEADS, seq, LANES), BF16)] * 2,
        scratch_shapes=[pltpu.VMEM((8, LANES), F32)],
        compiler_params=pltpu.CompilerParams(
            dimension_semantics=("parallel", "arbitrary"),
            vmem_limit_bytes=_vmem_limit([ts * LANES * 4, ts * ts * 2, 4 * N_HEADS * ts * LANES * 2],
                                         16 * ts * LANES * 4)),
        name="fox_decay",
    )(f_logits.reshape(batch, seq, LANES), jnp.pad(b_f, (0, LANES - N_HEADS)).reshape(1, LANES), tri, P, P)


def _fox_mixer(P, f_logits, b_f, batch, seq):
    q_aug, k_aug = _fox_decay(P, f_logits, b_f, batch, seq)
    vt = _value_tiles(P[:, 2 * N_HEADS:], DENSE_TILE)
    return _dense_attn(q_aug, 0, k_aug, 0, vt, 0, None, n_kv_heads=N_HEADS, group=1, tile=DENSE_TILE,
                       n_kv=4, q_chunk=2 * DENSE_TILE, n_back=None, diag_causal=True, emit_stats=False,
                       out_dtype=BF16, name="fox_attn")


def _moba_gate_kernel(q_ref, k_ref, qa_ref, *, n_blk):
    k = k_ref[0, 0].astype(F32)
    S = k.shape[0]
    k_mean = jnp.sum(k.reshape(n_blk, S // n_blk, LANES), axis=1) * (1.0 / (S // n_blk))
    hi = k_mean.astype(BF16)
    lo = (k_mean - hi.astype(F32)).astype(BF16)
    q = q_ref[0, 0]
    gate = (lax.dot_general(hi, q, _NT, preferred_element_type=F32)
            + lax.dot_general(lo, q, _NT, preferred_element_type=F32))
    own = lax.broadcasted_iota(jnp.int32, (1, S), 1) // (S // n_blk)
    jidx = lax.broadcasted_iota(jnp.int32, gate.shape, 0)
    vals = jnp.where(jidx < own, gate, -jnp.inf)
    chosen = _rank_select(vals, own, n_blk, MOBA_TOPK) | (jidx == own)
    sel_cols = jnp.concatenate([jnp.zeros((HEAD_DIM, S), F32), jnp.where(chosen, 0.0, SEL_NEG),
                                jnp.zeros((LANES - HEAD_DIM - n_blk, S), F32)], axis=0).T
    qa_ref[0, 0] = (q.astype(F32) + sel_cols).astype(qa_ref.dtype)


def _moba_gate(P, batch, seq):
    n_blk = seq // MOBA_BLOCK
    return pl.pallas_call(
        functools.partial(_moba_gate_kernel, n_blk=n_blk),
        grid=(batch, N_HEADS),
        in_specs=[
            pl.BlockSpec((1, 1, seq, LANES), lambda b, h: (b, h, 0, 0)),
            pl.BlockSpec((1, 1, seq, LANES), lambda b, h: (b, N_HEADS + h, 0, 0)),
        ],
        out_specs=pl.BlockSpec((1, 1, seq, LANES), lambda b, h: (b, h, 0, 0)),
        out_shape=jax.ShapeDtypeStruct((batch, N_HEADS, seq, LANES), BF16),
        compiler_params=pltpu.CompilerParams(
            dimension_semantics=("parallel", "parallel"),
            vmem_limit_bytes=_vmem_limit([3 * seq * LANES * 2], 5 * seq * LANES * 4 + 24 * n_blk * seq * 4)),
        name="moba_gate",
    )(P, P)


def _moba_mixer(P, bias, batch, seq):
    q_aug = _moba_gate(P, batch, seq)
    lane = jnp.arange(LANES)
    onehot = (np.arange(seq)[:, None] // MOBA_BLOCK == np.arange(LANES - HEAD_DIM)[None, :]).astype(np.float32)
    onehot = jnp.asarray(np.pad(onehot, ((0, 0), (HEAD_DIM, 0))), BF16)
    k_aug = jnp.where(lane < HEAD_DIM, P[:, N_HEADS:2 * N_HEADS], onehot[None, None])
    vt = _value_tiles(P[:, 2 * N_HEADS:], DENSE_TILE)
    return _dense_attn(q_aug, 0, k_aug, 0, vt, 0, bias, n_kv_heads=N_HEADS, group=1, tile=DENSE_TILE,
                       n_kv=4, q_chunk=2 * DENSE_TILE, n_back=None, diag_causal=False, emit_stats=False,
                       out_dtype=BF16, name="moba_attn")


def _column_scales(segments):
    gain = [jnp.tile(jnp.ones((HEAD_DIM,), F32) if g is None else g, n) for n, g in segments]
    flag = [jnp.full((n * HEAD_DIM,), 0.0 if g is None else 1.0, F32) for n, g in segments]
    return jnp.concatenate(gain).reshape(1, -1), jnp.concatenate(flag).reshape(1, -1)


def _qkv_scales(q_gain, k_gain):
    return _column_scales([(N_HEADS, q_gain * QK_SCALE), (N_HEADS, k_gain), (N_HEADS, None)])


def _dilated_layer(x2, norm_g, w, qg, kg, rel_table, batch, seq):
    gain, flag = _qkv_scales(qg, kg)
    P = _proj(x2, norm_g, w.astype(BF16), gain, flag, None, batch, seq)
    return _dilated_mixer(P, rel_table, batch, seq)


def _nsa_layer(x2, norm_g, w, cmp_pos, cmp_w1, cmp_w2, qg, kg, bias_sel, rel_table, batch, seq):
    KH = NSA_KV_HEADS
    n_main = D_MODEL + 6 * KH * HEAD_DIM
    gain, flag = _column_scales([(N_HEADS, qg * QK_SCALE), (KH, None), (KH, None), (KH, kg), (KH, None),
                                 (KH, kg), (KH, None)])
    w_aux = jnp.pad(w[:, n_main:], ((0, 0), (0, LANES - 3 * N_HEADS))).astype(BF16)
    P, gate_logits = _proj(x2, norm_g, w[:, :n_main].astype(BF16), gain, flag, w_aux, batch, seq)
    return _nsa_mixer(P, gate_logits, cmp_pos, cmp_w1, cmp_w2, kg, bias_sel, rel_table, batch, seq)


def _fox_layer(x2, norm_g, w, b_f, qg, kg, batch, seq):
    gain, flag = _qkv_scales(qg, kg)
    w_aux = jnp.pad(w[:, 3 * D_MODEL:], ((0, 0), (0, LANES - N_HEADS))).astype(BF16)
    P, f_logits = _proj(x2, norm_g, w[:, :3 * D_MODEL].astype(BF16), gain, flag, w_aux, batch, seq)
    return _fox_mixer(P, f_logits, b_f, batch, seq)


def _moba_layer(x2, norm_g, w, qg, kg, bias, batch, seq):
    gain, flag = _qkv_scales(qg, kg)
    P = _proj(x2, norm_g, w.astype(BF16), gain, flag, None, batch, seq)
    return _moba_mixer(P, bias, batch, seq)


def kernel(x, rel_table, attn_norm, mlp_norm, q_gain, k_gain, w_out, mlp_w_up, mlp_w_down, dsa_w_in, nsa_w_in,
           nsa_cmp_pos, nsa_cmp_w1, nsa_cmp_w2, fox_w_in, fox_b_f, moba_w_in):
    batch, seq, _ = x.shape
    depth = attn_norm.shape[0]
    x2 = x.reshape(batch * seq, D_MODEL)
    bias_dense = _tile_bias(rel_table, DENSE_TILE, 10, 1, None)
    for layer in range(depth):
        kind, r = layer % 4, layer // 4
        qg, kg, ng = q_gain[layer], k_gain[layer], attn_norm[layer]
        if kind == 0:
            mixed = _dilated_layer(x2, ng, dsa_w_in[r], qg, kg, rel_table, batch, seq)
        elif kind == 1:
            mixed = _nsa_layer(x2, ng, nsa_w_in[r], nsa_cmp_pos[r], nsa_cmp_w1[r], nsa_cmp_w2[r], qg, kg,
                               bias_dense, rel_table, batch, seq)
        elif kind == 2:
            mixed = _fox_layer(x2, ng, fox_w_in[r], fox_b_f[r], qg, kg, batch, seq)
        else:
            mixed = _moba_layer(x2, ng, moba_w_in[r], qg, kg, bias_dense, batch, seq)
        x2 = _out_mlp(x2, mixed.reshape(batch * seq, D_MODEL), w_out[layer].astype(BF16), mlp_norm[layer],
                      mlp_w_up[layer].astype(BF16), mlp_w_down[layer].astype(BF16))
    return x2.reshape(batch, seq, D_MODEL)
```

```python
import functools
import math

import jax
import jax.numpy as jnp
import numpy as np
from jax import lax
from jax.experimental import pallas as pl
from jax.experimental.pallas import tpu as pltpu

F32 = jnp.float32
BF16 = jnp.bfloat16

D_MODEL = 1024
N_HEADS = 16
HEAD_DIM = 64
D_FF = 4 * D_MODEL
RMS_EPS = 1e-6
MASK_NEG = -1e30
TINY = 1e-30
LOG2E = math.log2(math.e)
QK_SCALE = HEAD_DIM ** -0.5 * LOG2E

REL_BUCKETS = 32
REL_MAX_EXACT = REL_BUCKETS // 2
REL_MAX_DIST = 2048

DILATED_PATTERNS = ((128, 1), (512, 4), (2048, 16))
BAND_BLOCK = 128

NSA_KV_HEADS = 4
NSA_GROUP = N_HEADS // NSA_KV_HEADS
CMP_STRIDE = 16
CMP_BLOCK = 2 * CMP_STRIDE
CMP_HIDDEN = 256
SEL_BLOCK = 64
SEL_TOPK = 16
NSA_WINDOW = 512

MOBA_BLOCK = 256
MOBA_TOPK = 3

LANES = 128
VMEM_BYTES_V7X = 64 * 1024 * 1024
VMEM_HEADROOM = 6 * 1024 * 1024

DENSE_TILE = 256
SEL_NEG = -(2.0 ** 100)

_NT = (((1,), (1,)), ((), ()))


def _vmem_limit(block_bytes, temp_bytes):
    need = 2 * int(sum(block_bytes)) + int(temp_bytes) + (2 << 20)
    return int(min(need, VMEM_BYTES_V7X - VMEM_HEADROOM))


def _split3(v):
    a = v.astype(BF16)
    r = v - a.astype(F32)
    b = r.astype(BF16)
    c = (r - b.astype(F32)).astype(BF16)
    return a, b, c


def _proj_kernel(*refs, has_aux):
    if has_aux:
        x_ref, g_ref, w_ref, cg_ref, cf_ref, bd_ref, wa_ref, o_ref, oa_ref, h_scr = refs
    else:
        x_ref, g_ref, w_ref, cg_ref, cf_ref, bd_ref, o_ref, h_scr = refs

    @pl.when(pl.program_id(1) == 0)
    def _():
        x = x_ref[...]
        ms = jnp.mean(x * x, axis=-1, keepdims=True)
        h_scr[...] = (x * lax.rsqrt(ms + RMS_EPS) * g_ref[...]).astype(BF16)
        if has_aux:
            oa_ref[...] = jnp.dot(h_scr[...], wa_ref[...], preferred_element_type=F32)

    y = jnp.dot(h_scr[...], w_ref[...], preferred_element_type=F32)
    ms = jnp.dot((y * y).astype(BF16), bd_ref[...], preferred_element_type=F32)
    rs = lax.rsqrt(ms + RMS_EPS)
    out = y * jnp.where(cf_ref[...] > 0.5, rs, 1.0) * cg_ref[...]
    lane = lax.broadcasted_iota(jnp.int32, (out.shape[0], LANES), 1)
    for pair in range(out.shape[1] // LANES):
        both = out[:, pair * LANES:(pair + 1) * LANES]
        o_ref[0, 2 * pair] = jnp.where(lane < HEAD_DIM, both, 0.0).astype(o_ref.dtype)
        o_ref[0, 2 * pair + 1] = jnp.where(lane < HEAD_DIM, pltpu.roll(both, HEAD_DIM, 1), 0.0).astype(o_ref.dtype)


def _proj(x2, norm_g, w, col_gain, col_flag, w_aux, batch, seq):
    T = x2.shape[0]
    N = w.shape[1]
    tm, tn = 1024, 2 * LANES
    spb = seq // tm
    bd = np.zeros((tn, tn), np.float32)
    for a in range(tn // HEAD_DIM):
        bd[a * HEAD_DIM:(a + 1) * HEAD_DIM, a * HEAD_DIM:(a + 1) * HEAD_DIM] = 1.0 / HEAD_DIM
    bd = jnp.asarray(bd, BF16)
    has_aux = w_aux is not None
    in_specs = [
        pl.BlockSpec((tm, D_MODEL), lambda i, j: (i, 0)),
        pl.BlockSpec((1, D_MODEL), lambda i, j: (0, 0)),
        pl.BlockSpec((D_MODEL, tn), lambda i, j: (0, j)),
        pl.BlockSpec((1, tn), lambda i, j: (0, j)),
        pl.BlockSpec((1, tn), lambda i, j: (0, j)),
        pl.BlockSpec((tn, tn), lambda i, j: (0, 0)),
    ]
    args = [x2, norm_g.reshape(1, D_MODEL), w, col_gain, col_flag, bd]
    hpt = tn // HEAD_DIM
    out_shape = [jax.ShapeDtypeStruct((batch, N // HEAD_DIM, seq, LANES), BF16)]
    out_specs = [pl.BlockSpec((1, hpt, tm, LANES), lambda i, j: (i // spb, j, i % spb, 0))]
    if has_aux:
        in_specs.append(pl.BlockSpec((D_MODEL, LANES), lambda i, j: (0, 0)))
        args.append(w_aux)
        out_shape.append(jax.ShapeDtypeStruct((T, LANES), F32))
        out_specs.append(pl.BlockSpec((tm, LANES), lambda i, j: (i, 0)))
    blocks = [tm * D_MODEL * 4, D_MODEL * tn * 2, hpt * tm * LANES * 2, tn * tn * 2, tm * LANES * 4,
              D_MODEL * LANES * 2]
    res = pl.pallas_call(
        functools.partial(_proj_kernel, has_aux=has_aux),
        grid=(T // tm, N // tn),
        in_specs=in_specs,
        out_specs=out_specs,
        out_shape=out_shape,
        scratch_shapes=[pltpu.VMEM((tm, D_MODEL), BF16)],
        compiler_params=pltpu.CompilerParams(
            dimension_semantics=("parallel", "arbitrary"),
            vmem_limit_bytes=_vmem_limit(blocks, tm * D_MODEL * 2 + 6 * tm * tn * 4)),
        name="proj",
    )(*args)
    return res if has_aux else res[0]


def _out_mlp_kernel(x_ref, m_ref, wo_ref, g_ref, wu_ref, wd_ref, o_ref, h_scr):
    @pl.when(pl.program_id(1) == 0)
    def _():
        x1 = x_ref[...] + jnp.dot(m_ref[...], wo_ref[...], preferred_element_type=F32)
        o_ref[...] = x1
        ms = jnp.mean(x1 * x1, axis=-1, keepdims=True)
        h_scr[...] = (x1 * lax.rsqrt(ms + RMS_EPS) * g_ref[...]).astype(BF16)

    u = jnp.dot(h_scr[...], wu_ref[...], preferred_element_type=F32)
    a = jnp.square(jnp.maximum(u, 0.0)).astype(BF16)
    o_ref[...] += jnp.dot(a, wd_ref[...], preferred_element_type=F32)


def _out_mlp(x2, mixed, w_out, norm_g, w_up, w_down):
    T = x2.shape[0]
    tm, tf = 1024, 512
    blocks = [tm * D_MODEL * 4, tm * D_MODEL * 2, D_MODEL * D_MODEL * 2,
              D_MODEL * tf * 2, tf * D_MODEL * 2, tm * D_MODEL * 4]
    return pl.pallas_call(
        _out_mlp_kernel,
        grid=(T // tm, D_FF // tf),
        in_specs=[
            pl.BlockSpec((tm, D_MODEL), lambda i, f: (i, 0)),
            pl.BlockSpec((tm, D_MODEL), lambda i, f: (i, 0)),
            pl.BlockSpec((D_MODEL, D_MODEL), lambda i, f: (0, 0)),
            pl.BlockSpec((1, D_MODEL), lambda i, f: (0, 0)),
            pl.BlockSpec((D_MODEL, tf), lambda i, f: (0, f)),
            pl.BlockSpec((tf, D_MODEL), lambda i, f: (f, 0)),
        ],
        out_specs=pl.BlockSpec((tm, D_MODEL), lambda i, f: (i, 0)),
        out_shape=jax.ShapeDtypeStruct((T, D_MODEL), F32),
        scratch_shapes=[pltpu.VMEM((tm, D_MODEL), BF16)],
        compiler_params=pltpu.CompilerParams(
            dimension_semantics=("parallel", "arbitrary"),
            vmem_limit_bytes=_vmem_limit(blocks, tm * D_MODEL * 2 + 3 * tm * tf * 4 + 2 * tm * D_MODEL * 4)),
        name="out_mlp",
    )(x2, mixed, w_out, norm_g.reshape(1, D_MODEL), w_up, w_down)


def _dense_attn_kernel(*refs, tile, n_kv, group, n_qt, n_back, has_bias, n_bias, diag_causal):
    if has_bias:
        q_ref, k_ref, vt_ref, b_ref, o_ref, acc_scr, s_scr, p_scr, st_scr = refs
    else:
        q_ref, k_ref, vt_ref, o_ref, acc_scr, s_scr, p_scr, st_scr = refs
        b_ref = None
    chunk = pl.program_id(2)
    n_streams = n_kv * group
    if diag_causal:
        visible = (lax.broadcasted_iota(jnp.int32, (tile, tile), 0)
                   <= lax.broadcasted_iota(jnp.int32, (tile, tile), 1))

    def one_query_tile(t, _):
        qi = chunk * n_qt + t
        row0 = pl.multiple_of(t * tile, tile)
        qs = [q_ref[0, h, pl.ds(row0, tile), :].astype(F32).T.astype(BF16) for h in range(n_streams)]

        def issue_scores(d, slot):
            start = pl.multiple_of(jnp.maximum(qi - d, 0) * tile, tile)
            for h in range(n_streams):
                s = jnp.dot(k_ref[0, h // group, pl.ds(start, tile), :], qs[h],
                            preferred_element_type=F32)
                if has_bias:
                    s = s + b_ref[h, jnp.minimum(d, n_bias - 1)]
                s_scr[slot, h] = s

        def pv_matmuls(d, slot):
            return [jnp.dot(vt_ref[0, h // group, qi - d], p_scr[slot, h], preferred_element_type=F32)
                    for h in range(n_streams)]

        def accumulate(slot, pvs):
            for h in range(n_streams):
                acc_scr[h] = st_scr[2 + slot, h] * acc_scr[h] + pvs[h]

        def softmax(slot, first):
            for h in range(n_streams):
                s = s_scr[slot, h]
                if first:
                    if diag_causal:
                        s = jnp.where(visible, s, MASK_NEG)
                    m = jnp.max(s, axis=0, keepdims=True)
                    p = jnp.exp2(s - m)
                    st_scr[0, h] = m
                    st_scr[1, h] = jnp.sum(p, axis=0, keepdims=True)
                    st_scr[3 - slot, h] = jnp.zeros_like(m)
                else:
                    m0 = st_scr[0, h]
                    m = jnp.maximum(m0, jnp.max(s, axis=0, keepdims=True))
                    alpha = jnp.exp2(m0 - m)
                    p = jnp.exp2(s - m)
                    st_scr[0, h] = m
                    st_scr[1, h] = alpha * st_scr[1, h] + jnp.sum(p, axis=0, keepdims=True)
                    st_scr[3 - slot, h] = alpha
                p_scr[1 - slot, h] = p.astype(BF16)

        def tile_step(d, slot):
            pvs = pv_matmuls(d - 1, slot)
            issue_scores(d + 1, 1 - slot)
            softmax(slot, False)
            accumulate(slot, pvs)

        acc_scr[...] = jnp.zeros_like(acc_scr)
        issue_scores(0, 0)
        issue_scores(1, 1)
        softmax(0, True)
        n_steps = qi + 1 if n_back is None else jnp.minimum(qi, n_back) + 1

        def two_steps(j, _):
            tile_step(1 + 2 * j, 1)
            tile_step(2 + 2 * j, 0)
            return 0

        lax.fori_loop(0, (n_steps - 1) >> 1, two_steps, 0)
        odd_tail = ((n_steps - 1) & 1) == 1

        @pl.when(odd_tail)
        def _():
            tile_step(n_steps - 1, 1)
            accumulate(0, pv_matmuls(n_steps - 1, 0))

        @pl.when(jnp.logical_not(odd_tail))
        def _():
            accumulate(1, pv_matmuls(n_steps - 1, 1))

        outs = [acc_scr[h][:HEAD_DIM] / jnp.maximum(st_scr[1, h], TINY) for h in range(n_streams)]
        for pair in range(n_streams // 2):
            both = jnp.concatenate([outs[2 * pair], outs[2 * pair + 1]], axis=0)
            o_ref[0, pl.ds(row0, tile), pair * LANES:(pair + 1) * LANES] = both.T.astype(o_ref.dtype)
        return 0

    lax.fori_loop(0, n_qt, one_query_tile, 0)


def _value_tiles(v, tile):
    B, H, S, _ = v.shape
    return jnp.swapaxes(v.reshape(B, H, S // tile, tile, LANES), -1, -2)


def _dense_attn(q_arr, q_off, k_arr, k_off, vt_arr, v_off, bias, *, n_kv_heads, group, tile, n_kv, q_chunk,
                n_back, diag_causal, out_dtype, name):
    B, _, S, _ = q_arr.shape
    n_q_heads = n_kv_heads * group
    nq = n_kv * group
    assert q_off % nq == 0 and k_off % n_kv == 0 and v_off % n_kv == 0
    assert S % q_chunk == 0 and q_chunk % tile == 0 and n_kv_heads % n_kv == 0
    n_qt = q_chunk // tile
    has_bias = bias is not None
    n_bias = bias.shape[1] if has_bias else 0
    in_specs = [
        pl.BlockSpec((1, nq, q_chunk, LANES), lambda u, b, c: (b, q_off // nq + u, c, 0)),
        pl.BlockSpec((1, n_kv, S, LANES), lambda u, b, c: (b, k_off // n_kv + u, 0, 0)),
        pl.BlockSpec((1, n_kv, S // tile, LANES, tile), lambda u, b, c: (b, v_off // n_kv + u, 0, 0, 0)),
    ]
    args = [q_arr, k_arr, vt_arr]
    out_bytes = jnp.dtype(out_dtype).itemsize
    blocks = [nq * q_chunk * LANES * 2, 2 * n_kv * S * LANES * 2, nq * q_chunk * LANES * out_bytes]
    if has_bias:
        in_specs.append(pl.BlockSpec((nq, n_bias, tile, tile), lambda u, b, c: (u, 0, 0, 0)))
        args.append(bias)
        blocks.append(nq * n_bias * tile * tile * 4)
    kern = functools.partial(_dense_attn_kernel, tile=tile, n_kv=n_kv, group=group, n_qt=n_qt, n_back=n_back,
                             has_bias=has_bias, n_bias=n_bias, diag_causal=diag_causal)
    assert nq % 2 == 0
    return pl.pallas_call(
        kern,
        grid=(n_kv_heads // n_kv, B, S // q_chunk),
        in_specs=in_specs,
        out_specs=pl.BlockSpec((1, q_chunk, nq * HEAD_DIM), lambda u, b, c: (b, c, u)),
        out_shape=jax.ShapeDtypeStruct((B, S, n_q_heads * HEAD_DIM), out_dtype),
        scratch_shapes=[pltpu.VMEM((nq, LANES, tile), F32),
                        pltpu.VMEM((2, nq, tile, tile), F32),
                        pltpu.VMEM((2, nq, tile, tile), BF16),
                        pltpu.VMEM((4, nq, 1, tile), F32)],
        compiler_params=pltpu.CompilerParams(
            dimension_semantics=("parallel", "parallel", "arbitrary"),
            vmem_limit_bytes=_vmem_limit(blocks, nq * (4 * tile * tile + 3 * LANES * tile) * 4)),
        name=name,
    )(*args)


def _bucket_np(dist):
    d = np.maximum(dist, 0)
    df = np.maximum(d.astype(np.float32), np.float32(1.0))
    large = REL_MAX_EXACT + (np.log(df / np.float32(REL_MAX_EXACT))
                             / np.float32(math.log(REL_MAX_DIST / REL_MAX_EXACT))
                             * np.float32(REL_BUCKETS - REL_MAX_EXACT)).astype(np.int32)
    large = np.minimum(large, REL_BUCKETS - 1)
    return np.where(d < REL_MAX_EXACT, d, large).astype(np.int32)


def _dist_values(rel_table, dist, ok):
    vals = rel_table[jnp.asarray(_bucket_np(dist))] * LOG2E
    return jnp.where(jnp.asarray(ok)[:, None], vals, MASK_NEG).T


def _skew_kernel(w_ref, o_ref, *, step, col0, mask_last_row):
    _, n_out, n_rows, width = o_ref.shape
    x = jnp.broadcast_to(w_ref[0], (n_rows, w_ref.shape[2]))
    x = pltpu.roll(x, 0, 1, stride=step, stride_axis=0)
    if mask_last_row:
        row = lax.broadcasted_iota(jnp.int32, x.shape, 0)
        x = jnp.where(row == n_rows - 1, MASK_NEG, x)
    for a in range(n_out):
        o_ref[0, a] = x[:, col0 + a * width:col0 + (a + 1) * width]


def _skew(w, n_rows, step, col0, n_out, width, mask_last_row=False):
    H, W = w.shape
    assert W % LANES == 0 and col0 % LANES == 0 and width % LANES == 0 and col0 + n_out * width <= W
    assert col0 - step * (n_rows - 1) >= 0
    return pl.pallas_call(
        functools.partial(_skew_kernel, step=step, col0=col0, mask_last_row=mask_last_row),
        grid=(H,),
        in_specs=[pl.BlockSpec((1, 1, W), lambda h: (h, 0, 0))],
        out_specs=pl.BlockSpec((1, n_out, n_rows, width), lambda h: (h, 0, 0, 0)),
        out_shape=jax.ShapeDtypeStruct((H, n_out, n_rows, width), F32),
        compiler_params=pltpu.CompilerParams(
            dimension_semantics=("parallel",),
            vmem_limit_bytes=_vmem_limit([W * 4, n_out * n_rows * width * 4], 3 * n_rows * W * 4)),
        name="bias_skew",
    )(w.reshape(H, 1, W))


def _tile_bias(rel_table, tile, n_tiles, dist_scale, max_dist):
    d = np.arange((n_tiles + 1) * tile) - tile
    ok = d >= 0
    if max_dist is not None:
        ok &= d <= max_dist
    w = _dist_values(rel_table, d * dist_scale, ok)
    return _skew(w, tile, 1, tile, n_tiles, tile)


def _merge3_kernel(a_ref, b_ref, c_ref, o_ref):
    merged = []
    for h in range(2):
        parts = [r[0, h] for r in (a_ref, b_ref, c_ref)]
        ms = [p[:, HEAD_DIM:HEAD_DIM + 1] for p in parts]
        ls = [p[:, HEAD_DIM + 1:HEAD_DIM + 2] for p in parts]
        m_all = jnp.maximum(jnp.maximum(ms[0], ms[1]), ms[2])
        ws = [jnp.exp2(m - m_all) for m in ms]
        num = ws[0] * parts[0] + ws[1] * parts[1] + ws[2] * parts[2]
        den = ws[0] * ls[0] + ws[1] * ls[1] + ws[2] * ls[2]
        merged.append(num / jnp.maximum(den, TINY))
    lane = lax.broadcasted_iota(jnp.int32, merged[0].shape, 1)
    o_ref[0] = jnp.where(lane < HEAD_DIM, merged[0], pltpu.roll(merged[1], HEAD_DIM, 1)).astype(o_ref.dtype)


def _merge3(parts):
    B, H, S, _ = parts[0].shape
    ts = 2048
    spec = pl.BlockSpec((1, 2, ts, LANES), lambda b, p, s: (b, p, s, 0))
    return pl.pallas_call(
        _merge3_kernel,
        grid=(B, H // 2, S // ts),
        in_specs=[spec, spec, spec],
        out_specs=pl.BlockSpec((1, ts, LANES), lambda b, p, s: (b, s, p)),
        out_shape=jax.ShapeDtypeStruct((B, S, H * HEAD_DIM), BF16),
        compiler_params=pltpu.CompilerParams(
            dimension_semantics=("parallel", "parallel", "parallel"),
            vmem_limit_bytes=_vmem_limit([6 * ts * LANES * 4, ts * LANES * 2], 12 * ts * LANES * 4)),
        name="dilated_merge",
    )(*parts)


def _band_attn_kernel(q_ref, k_ref, vt_ref, b_ref, o_ref, *, tile, n_heads, n_qt):
    chunk = pl.program_id(2)
    channel = lax.broadcasted_iota(jnp.int32, (LANES, tile), 0)

    def one_query_tile(t, _):
        qi = chunk * n_qt + t
        row0 = pl.multiple_of(t * tile, tile)
        k0 = jnp.maximum(qi - 1, 0)
        start = pl.multiple_of(k0 * tile, tile)
        case = jnp.where(qi == 0, 1, 0)
        scores = []
        for h in range(n_heads):
            q_t = q_ref[0, h, pl.ds(row0, tile), :].astype(F32).T.astype(BF16)
            scores.append(jnp.dot(k_ref[0, h, pl.ds(start, 2 * tile), :], q_t, preferred_element_type=F32)
                          + b_ref[h, case])
        stats = []
        for h in range(n_heads):
            m = jnp.max(scores[h], axis=0, keepdims=True)
            p = jnp.exp2(scores[h] - m)
            stats.append((m, jnp.sum(p, axis=0, keepdims=True), p.astype(BF16)))
        for h in range(n_heads):
            m, l, p = stats[h]
            acc = (jnp.dot(vt_ref[0, h, k0], p[:tile], preferred_element_type=F32)
                   + jnp.dot(vt_ref[0, h, k0 + 1], p[tile:], preferred_element_type=F32))
            out_t = jnp.where(channel == HEAD_DIM, m, jnp.where(channel == HEAD_DIM + 1, l, acc))
            o_ref[0, h, pl.ds(row0, tile), :] = out_t.T
        return 0

    lax.fori_loop(0, n_qt, one_query_tile, 0)


def _band_attn(P, vt, bias2, n_heads, q_chunk, name):
    B, _, n, _ = P.shape
    tile = bias2.shape[-1]
    assert n >= 2 * tile and n % q_chunk == 0 and q_chunk % tile == 0 and N_HEADS % n_heads == 0
    kern = functools.partial(_band_attn_kernel, tile=tile, n_heads=n_heads, n_qt=q_chunk // tile)
    groups = N_HEADS // n_heads
    blocks = [n_heads * q_chunk * LANES * 2, 2 * n_heads * n * LANES * 2, n_heads * 2 * 2 * tile * tile * 4,
              n_heads * q_chunk * LANES * 4]
    return pl.pallas_call(
        kern,
        grid=(groups, B, n // q_chunk),
        in_specs=[
            pl.BlockSpec((1, n_heads, q_chunk, LANES), lambda u, b, c: (b, u, c, 0)),
            pl.BlockSpec((1, n_heads, n, LANES), lambda u, b, c: (b, groups + u, 0, 0)),
            pl.BlockSpec((1, n_heads, n // tile, LANES, tile), lambda u, b, c: (b, u, 0, 0, 0)),
            pl.BlockSpec((n_heads, 2, 2 * tile, tile), lambda u, b, c: (u, 0, 0, 0)),
        ],
        out_specs=pl.BlockSpec((1, n_heads, q_chunk, LANES), lambda u, b, c: (b, u, c, 0)),
        out_shape=jax.ShapeDtypeStruct((B, N_HEADS, n, LANES), F32),
        compiler_params=pltpu.CompilerParams(
            dimension_semantics=("parallel", "parallel", "arbitrary"),
            vmem_limit_bytes=_vmem_limit(blocks, n_heads * 6 * 2 * tile * tile * 4)),
        name=name,
    )(P, P, vt, bias2)


def _dilated_mixer(P, rel_table, batch, seq):
    parts = []
    for window, dil in DILATED_PATTERNS:
        n = seq // dil
        if dil == 1:
            Pr = P
        else:
            Pr = P.reshape(batch, 3 * N_HEADS, n, dil, LANES).transpose(0, 3, 1, 2, 4)
            Pr = Pr.reshape(batch * dil, 3 * N_HEADS, n, LANES)
        bias = _tile_bias(rel_table, BAND_BLOCK, 2, dil, window // dil)
        general = jnp.concatenate([bias[:, 1], bias[:, 0]], axis=1)
        first = jnp.concatenate([bias[:, 0], jnp.full_like(bias[:, 0], MASK_NEG)], axis=1)
        vt = _value_tiles(Pr[:, 2 * N_HEADS:], BAND_BLOCK)
        o = _band_attn(Pr, vt, jnp.stack([general, first], axis=1), N_HEADS // 2 if dil == 1 else N_HEADS,
                       min(n, 512), f"dilated_attn_d{dil}")
        if dil > 1:
            o = o.reshape(batch, dil, N_HEADS, n, LANES).transpose(0, 2, 3, 1, 4).reshape(batch, N_HEADS, seq, LANES)
        parts.append(o)
    return _merge3(parts)


def _compress_kernel(x_ref, w1_ref, w2_ref, pos_ref, g_ref, o_ref):
    posw = jnp.dot(pos_ref[0], w1_ref[0], preferred_element_type=F32)[0:1]
    pre = jnp.dot(x_ref[0, 0], w1_ref[0], preferred_element_type=F32) + posw
    hid = 0.5 * pre * (1.0 + jnp.tanh(math.sqrt(2.0 / math.pi) * (pre + 0.044715 * (pre * pre * pre))))
    out = jnp.dot(hid.astype(BF16), w2_ref[0], preferred_element_type=F32)
    ms = jnp.sum(out * out, axis=-1, keepdims=True) * (1.0 / HEAD_DIM)
    normed = out * lax.rsqrt(ms + RMS_EPS) * g_ref[...]
    o_ref[0, 0] = jnp.where(pl.program_id(0) == 0, normed, out).astype(o_ref.dtype)


def _compress(blocks, w1, w2, pos, k_gain_pad):
    _, BK, n, K = blocks.shape
    return pl.pallas_call(
        _compress_kernel,
        grid=(2, BK),
        in_specs=[
            pl.BlockSpec((1, 1, n, K), lambda a, i: (a, i, 0, 0)),
            pl.BlockSpec((1, K, CMP_HIDDEN), lambda a, i: (a, 0, 0)),
            pl.BlockSpec((1, CMP_HIDDEN, LANES), lambda a, i: (a, 0, 0)),
            pl.BlockSpec((1, 8, K), lambda a, i: (a, 0, 0)),
            pl.BlockSpec((1, LANES), lambda a, i: (0, 0)),
        ],
        out_specs=pl.BlockSpec((1, 1, n, LANES), lambda a, i: (a, i, 0, 0)),
        out_shape=jax.ShapeDtypeStruct((2, BK, n, LANES), BF16),
        compiler_params=pltpu.CompilerParams(
            dimension_semantics=("parallel", "parallel"),
            vmem_limit_bytes=_vmem_limit([n * K * 2, K * CMP_HIDDEN * 2, CMP_HIDDEN * LANES * 2, 8 * K * 2,
                                          n * LANES * 2], 8 * n * CMP_HIDDEN * 4)),
        name="nsa_compress",
    )(blocks, w1, w2, pos, k_gain_pad)


def _rank_select(vals, own_row, n_blocks, top):
    jidx = lax.broadcasted_iota(jnp.int32, vals.shape, 0)
    cnt = jnp.zeros(vals.shape, F32)
    for j in range(n_blocks):
        row = vals[j:j + 1, :]
        ahead = (row > vals) | ((row == vals) & (jidx > j))
        cnt = cnt + jnp.where(ahead, 1.0, 0.0)
    return (jidx < own_row) & (cnt < top)


def _cmp_attn_kernel(q_ref, k_ref, vt_ref, b_ref, ov_ref, o_ref, qa_ref, *, tq):
    qi = pl.program_id(1)
    k = k_ref[0, 0]
    vt = vt_ref[0]
    psum = None
    outs = []
    for g in range(NSA_GROUP):
        s = lax.dot_general(k, q_ref[0, g], _NT, preferred_element_type=F32) + b_ref[g]
        m = jnp.max(s, axis=0, keepdims=True)
        p = jnp.where(m > 0.5 * MASK_NEG, jnp.exp2(s - m), 0.0)
        l = jnp.sum(p, axis=0, keepdims=True)
        probs = p / jnp.maximum(l, TINY)
        outs.append(jnp.dot(vt, probs.astype(BF16), preferred_element_type=F32)[:HEAD_DIM])
        psum = probs if psum is None else psum + probs
    for pair in range(NSA_GROUP // 2):
        both = jnp.concatenate([outs[2 * pair], outs[2 * pair + 1]], axis=0)
        o_ref[0, :, pair * LANES:(pair + 1) * LANES] = both.T.astype(o_ref.dtype)
    imp = None
    for part in _split3(psum):
        term = jnp.dot(ov_ref[...], part, preferred_element_type=F32)
        imp = term if imp is None else imp + term
    n_sel = imp.shape[0]
    pos = qi * tq + lax.broadcasted_iota(jnp.int32, (1, tq), 1)
    own = pos // SEL_BLOCK
    jidx = lax.broadcasted_iota(jnp.int32, imp.shape, 0)
    vals = jnp.where(jidx < own, imp, -1.0)
    chosen = _rank_select(vals, own, n_sel, SEL_TOPK - 1) | (jidx == own)
    sel_cols = jnp.concatenate([jnp.zeros((HEAD_DIM, tq), F32), jnp.where(chosen, 0.0, SEL_NEG)], axis=0).T
    for g in range(NSA_GROUP):
        qa_ref[0, g] = (q_ref[0, g].astype(F32) + sel_cols).astype(qa_ref.dtype)


def _cmp_attn(P, kv_cmp, vt_cmp, bias_c, overlap_t, batch, seq):
    assert seq // SEL_BLOCK == LANES - HEAD_DIM
    tq = 256
    n_cmp = kv_cmp.shape[2]
    n_sel = overlap_t.shape[0]
    kern = functools.partial(_cmp_attn_kernel, tq=tq)
    return pl.pallas_call(
        kern,
        grid=(NSA_KV_HEADS, seq // tq, batch),
        in_specs=[
            pl.BlockSpec((1, NSA_GROUP, tq, LANES), lambda kh, qi, b: (b, kh, qi, 0)),
            pl.BlockSpec((1, 1, n_cmp, LANES), lambda kh, qi, b: (0, b * NSA_KV_HEADS + kh, 0, 0)),
            pl.BlockSpec((1, LANES, n_cmp), lambda kh, qi, b: (b * NSA_KV_HEADS + kh, 0, 0)),
            pl.BlockSpec((NSA_GROUP, n_cmp, tq), lambda kh, qi, b: (kh, 0, qi)),
            pl.BlockSpec((n_sel, n_cmp), lambda kh, qi, b: (0, 0)),
        ],
        out_specs=[
            pl.BlockSpec((1, tq, NSA_GROUP * HEAD_DIM), lambda kh, qi, b: (b, qi, kh)),
            pl.BlockSpec((1, NSA_GROUP, tq, LANES), lambda kh, qi, b: (b, kh, qi, 0)),
        ],
        out_shape=[
            jax.ShapeDtypeStruct((batch, seq, N_HEADS * HEAD_DIM), F32),
            jax.ShapeDtypeStruct((batch, N_HEADS, seq, LANES), BF16),
        ],
        compiler_params=pltpu.CompilerParams(
            dimension_semantics=("parallel", "parallel", "arbitrary"),
            vmem_limit_bytes=_vmem_limit([NSA_GROUP * tq * LANES * 2, 2 * n_cmp * LANES * 2,
                                          NSA_GROUP * tq * n_cmp * 4, NSA_GROUP * tq * LANES * 4, n_sel * tq * 2],
                                         16 * tq * n_cmp * 4)),
        name="nsa_cmp_attn",
    )(P, kv_cmp, vt_cmp, bias_c, overlap_t)


def _nsa_merge_kernel(c_ref, s_ref, w_ref, g_ref, o_ref):
    gates = jax.nn.sigmoid(g_ref[0])
    lane = lax.broadcasted_iota(jnp.int32, gates.shape, 1)
    for pair in range(N_HEADS // 2):
        cols = slice(pair * LANES, (pair + 1) * LANES)

        def gate(branch):
            a, b = 3 * (2 * pair) + branch, 3 * (2 * pair + 1) + branch
            return jnp.where(lane < HEAD_DIM, gates[:, a:a + 1], gates[:, b:b + 1])

        o = gate(0) * c_ref[0, :, cols] + gate(1) * s_ref[0, :, cols] + gate(2) * w_ref[0, :, cols]
        o_ref[0, :, cols] = o.astype(o_ref.dtype)


def _nsa_merge(o_cmp, o_sel, o_win, gate_logits, batch, seq):
    ts = 256
    spec = pl.BlockSpec((1, ts, D_MODEL), lambda b, s: (b, s, 0))
    return pl.pallas_call(
        _nsa_merge_kernel,
        grid=(batch, seq // ts),
        in_specs=[spec, spec, spec, pl.BlockSpec((1, ts, LANES), lambda b, s: (b, s, 0))],
        out_specs=spec,
        out_shape=jax.ShapeDtypeStruct((batch, seq, D_MODEL), BF16),
        compiler_params=pltpu.CompilerParams(
            dimension_semantics=("parallel", "parallel"),
            vmem_limit_bytes=_vmem_limit([3 * ts * D_MODEL * 4, ts * LANES * 4, ts * D_MODEL * 2],
                                         8 * ts * LANES * 4)),
        name="nsa_merge",
    )(o_cmp, o_sel, o_win, gate_logits.reshape(batch, seq, LANES))


def _nsa_mixer(P, gate_logits, cmp_pos, cmp_w1, cmp_w2, k_gain, bias_sel, rel_table, batch, seq):
    KH = NSA_KV_HEADS
    n_chunk = seq // CMP_STRIDE
    n_sel = seq // SEL_BLOCK

    raw = P[:, N_HEADS:N_HEADS + 2 * KH, :, :HEAD_DIM].reshape(batch, 2, KH, n_chunk, CMP_STRIDE * HEAD_DIM)
    raw = raw.transpose(1, 0, 2, 3, 4).reshape(2, batch * KH, n_chunk, CMP_STRIDE * HEAD_DIM)
    blocks = jnp.concatenate([raw, jnp.roll(raw, -1, axis=2)], axis=-1)
    pos = jnp.broadcast_to(cmp_pos.reshape(2, 1, CMP_BLOCK * HEAD_DIM), (2, 8, CMP_BLOCK * HEAD_DIM)).astype(BF16)
    w2 = jnp.pad(cmp_w2, ((0, 0), (0, 0), (0, LANES - HEAD_DIM))).astype(BF16)
    k_gain_pad = jnp.pad(k_gain, (0, LANES - HEAD_DIM)).reshape(1, LANES)
    kv_cmp = _compress(blocks, cmp_w1.astype(BF16), w2, pos, k_gain_pad)

    d = np.arange(2 * seq) - (seq + CMP_BLOCK - 1)
    w = _dist_values(rel_table, d, d >= 0)
    bias_c = _skew(w, n_chunk, CMP_STRIDE, seq, 1, seq, mask_last_row=True)[:, 0]
    c_start = np.arange(n_chunk)[None, :] * CMP_STRIDE
    s_start = np.arange(n_sel)[:, None] * SEL_BLOCK
    overlap_t = ((c_start < s_start + SEL_BLOCK) & (c_start + CMP_BLOCK > s_start)
                 & (np.arange(n_chunk)[None, :] < n_chunk - 1))
    overlap_t = jnp.asarray(overlap_t.astype(np.float32), BF16)
    o_cmp, q_aug = _cmp_attn(P, kv_cmp, jnp.swapaxes(kv_cmp[1], -1, -2), bias_c, overlap_t, batch, seq)

    lane = jnp.arange(LANES)
    onehot = (np.arange(seq)[:, None] // SEL_BLOCK == np.arange(LANES - HEAD_DIM)[None, :]).astype(np.float32)
    onehot = jnp.asarray(np.pad(onehot, ((0, 0), (HEAD_DIM, 0))), BF16)
    k_aug = jnp.where(lane < HEAD_DIM, P[:, 24:28], onehot[None, None])
    o_sel = _dense_attn(q_aug, 0, k_aug, 0, _value_tiles(P[:, 28:32], DENSE_TILE), 0, bias_sel, n_kv_heads=KH,
                        group=NSA_GROUP, tile=DENSE_TILE, n_kv=1, q_chunk=DENSE_TILE, n_back=None,
                        diag_causal=False,out_dtype=F32, name="nsa_sel_attn")
    bias_win = _tile_bias(rel_table, DENSE_TILE, 3, 1, NSA_WINDOW - 1)
    o_win = _dense_attn(q_aug, 0, P, 32, _value_tiles(P[:, 36:40], DENSE_TILE), 0, bias_win, n_kv_heads=KH,
                        group=NSA_GROUP, tile=DENSE_TILE, n_kv=1, q_chunk=DENSE_TILE, n_back=2,
                        diag_causal=False,out_dtype=F32, name="nsa_win_attn")
    return _nsa_merge(o_cmp, o_sel, o_win, gate_logits, batch, seq)


def _fox_decay_kernel(f_ref, bf_ref, tri_ref, q_ref, k_ref, eq_ref, ek_ref, carry):
    @pl.when(pl.program_id(1) == 0)
    def _():
        carry[...] = jnp.zeros_like(carry)

    z = f_ref[0] + bf_ref[...]
    log_f = jnp.minimum(z, 0.0) - jnp.log1p(jnp.exp(-jnp.abs(z)))
    c = carry[0:1, :]
    for part in _split3(log_f):
        c = c + jnp.dot(tri_ref[...], part, preferred_element_type=F32)
    ts = c.shape[0]
    carry[...] = jnp.broadcast_to(c[ts - 1:ts, :], carry.shape)
    lane = lax.broadcasted_iota(jnp.int32, (ts, LANES), 1) - HEAD_DIM
    c = c * LOG2E
    for h in range(N_HEADS):
        c1, c2, c3 = _split3(c[:, h:h + 1])
        c1, c2, c3 = c1.astype(F32), c2.astype(F32), c3.astype(F32)
        eq = jnp.where(lane == 0, c1, jnp.where(lane == 1, c2, jnp.where(lane == 2, c3,
                       jnp.where((lane >= 3) & (lane < 6), 1.0, 0.0))))
        ek = jnp.where(lane == 3, -c1, jnp.where(lane == 4, -c2, jnp.where(lane == 5, -c3,
                       jnp.where((lane >= 0) & (lane < 3), 1.0, 0.0))))
        eq_ref[0, h] = (q_ref[0, h].astype(F32) + eq).astype(eq_ref.dtype)
        ek_ref[0, h] = (k_ref[0, h].astype(F32) + ek).astype(ek_ref.dtype)


def _fox_decay(P, f_logits, b_f, batch, seq):
    ts = 256
    tri = jnp.asarray(np.tril(np.ones((ts, ts), np.float32)), BF16)
    spec = pl.BlockSpec((1, N_HEADS, ts, LANES), lambda b, s: (b, 0, s, 0))
    return pl.pallas_call(
        _fox_decay_kernel,
        grid=(batch, seq // ts),
        in_specs=[
            pl.BlockSpec((1, ts, LANES), lambda b, s: (b, s, 0)),
            pl.BlockSpec((1, LANES), lambda b, s: (0, 0)),
            pl.BlockSpec((ts, ts), lambda b, s: (0, 0)),
            spec,
            pl.BlockSpec((1, N_HEADS, ts, LANES), lambda b, s: (b, 1, s, 0)),
        ],
        out_specs=[spec, spec],
        out_shape=[jax.ShapeDtypeStruct((batch, N_HEADS, seq, LANES), BF16)] * 2,
        scratch_shapes=[pltpu.VMEM((8, LANES), F32)],
        compiler_params=pltpu.CompilerParams(
            dimension_semantics=("parallel", "arbitrary"),
            vmem_limit_bytes=_vmem_limit([ts * LANES * 4, ts * ts * 2, 4 * N_HEADS * ts * LANES * 2],
                                         16 * ts * LANES * 4)),
        name="fox_decay",
    )(f_logits.reshape(batch, seq, LANES), jnp.pad(b_f, (0, LANES - N_HEADS)).reshape(1, LANES), tri, P, P)


def _fox_mixer(P, f_logits, b_f, batch, seq):
    q_aug, k_aug = _fox_decay(P, f_logits, b_f, batch, seq)
    vt = _value_tiles(P[:, 2 * N_HEADS:], DENSE_TILE)
    return _dense_attn(q_aug, 0, k_aug, 0, vt, 0, None, n_kv_heads=N_HEADS, group=1, tile=DENSE_TILE,
                       n_kv=4, q_chunk=2 * DENSE_TILE, n_back=None, diag_causal=True,
                       out_dtype=BF16, name="fox_attn")


def _moba_gate_kernel(q_ref, k_ref, qa_ref, *, n_blk):
    k = k_ref[0, 0].astype(F32)
    S = k.shape[0]
    k_mean = jnp.sum(k.reshape(n_blk, S // n_blk, LANES), axis=1) * (1.0 / (S // n_blk))
    hi = k_mean.astype(BF16)
    lo = (k_mean - hi.astype(F32)).astype(BF16)
    q = q_ref[0, 0]
    gate = (lax.dot_general(hi, q, _NT, preferred_element_type=F32)
            + lax.dot_general(lo, q, _NT, preferred_element_type=F32))
    own = lax.broadcasted_iota(jnp.int32, (1, S), 1) // (S // n_blk)
    jidx = lax.broadcasted_iota(jnp.int32, gate.shape, 0)
    vals = jnp.where(jidx < own, gate, -jnp.inf)
    chosen = _rank_select(vals, own, n_blk, MOBA_TOPK) | (jidx == own)
    sel_cols = jnp.concatenate([jnp.zeros((HEAD_DIM, S), F32), jnp.where(chosen, 0.0, SEL_NEG),
                                jnp.zeros((LANES - HEAD_DIM - n_blk, S), F32)], axis=0).T
    qa_ref[0, 0] = (q.astype(F32) + sel_cols).astype(qa_ref.dtype)


def _moba_gate(P, batch, seq):
    n_blk = seq // MOBA_BLOCK
    return pl.pallas_call(
        functools.partial(_moba_gate_kernel, n_blk=n_blk),
        grid=(batch, N_HEADS),
        in_specs=[
            pl.BlockSpec((1, 1, seq, LANES), lambda b, h: (b, h, 0, 0)),
            pl.BlockSpec((1, 1, seq, LANES), lambda b, h: (b, N_HEADS + h, 0, 0)),
        ],
        out_specs=pl.BlockSpec((1, 1, seq, LANES), lambda b, h: (b, h, 0, 0)),
        out_shape=jax.ShapeDtypeStruct((batch, N_HEADS, seq, LANES), BF16),
        compiler_params=pltpu.CompilerParams(
            dimension_semantics=("parallel", "parallel"),
            vmem_limit_bytes=_vmem_limit([3 * seq * LANES * 2], 5 * seq * LANES * 4 + 24 * n_blk * seq * 4)),
        name="moba_gate",
    )(P, P)


def _moba_mixer(P, bias, batch, seq):
    q_aug = _moba_gate(P, batch, seq)
    lane = jnp.arange(LANES)
    onehot = (np.arange(seq)[:, None] // MOBA_BLOCK == np.arange(LANES - HEAD_DIM)[None, :]).astype(np.float32)
    onehot = jnp.asarray(np.pad(onehot, ((0, 0), (HEAD_DIM, 0))), BF16)
    k_aug = jnp.where(lane < HEAD_DIM, P[:, N_HEADS:2 * N_HEADS], onehot[None, None])
    vt = _value_tiles(P[:, 2 * N_HEADS:], DENSE_TILE)
    return _dense_attn(q_aug, 0, k_aug, 0, vt, 0, bias, n_kv_heads=N_HEADS, group=1, tile=DENSE_TILE,
                       n_kv=4, q_chunk=2 * DENSE_TILE, n_back=None, diag_causal=False,
                       out_dtype=BF16, name="moba_attn")


def _column_scales(segments):
    gain = [jnp.tile(jnp.ones((HEAD_DIM,), F32) if g is None else g, n) for n, g in segments]
    flag = [jnp.full((n * HEAD_DIM,), 0.0 if g is None else 1.0, F32) for n, g in segments]
    return jnp.concatenate(gain).reshape(1, -1), jnp.concatenate(flag).reshape(1, -1)


def _qkv_scales(q_gain, k_gain):
    return _column_scales([(N_HEADS, q_gain * QK_SCALE), (N_HEADS, k_gain), (N_HEADS, None)])


def _dilated_layer(x2, norm_g, w, qg, kg, rel_table, batch, seq):
    gain, flag = _qkv_scales(qg, kg)
    P = _proj(x2, norm_g, w.astype(BF16), gain, flag, None, batch, seq)
    return _dilated_mixer(P, rel_table, batch, seq)


def _nsa_layer(x2, norm_g, w, cmp_pos, cmp_w1, cmp_w2, qg, kg, bias_sel, rel_table, batch, seq):
    KH = NSA_KV_HEADS
    n_main = D_MODEL + 6 * KH * HEAD_DIM
    gain, flag = _column_scales([(N_HEADS, qg * QK_SCALE), (KH, None), (KH, None), (KH, kg), (KH, None),
                                 (KH, kg), (KH, None)])
    w_aux = jnp.pad(w[:, n_main:], ((0, 0), (0, LANES - 3 * N_HEADS))).astype(BF16)
    P, gate_logits = _proj(x2, norm_g, w[:, :n_main].astype(BF16), gain, flag, w_aux, batch, seq)
    return _nsa_mixer(P, gate_logits, cmp_pos, cmp_w1, cmp_w2, kg, bias_sel, rel_table, batch, seq)


def _fox_layer(x2, norm_g, w, b_f, qg, kg, batch, seq):
    gain, flag = _qkv_scales(qg, kg)
    w_aux = jnp.pad(w[:, 3 * D_MODEL:], ((0, 0), (0, LANES - N_HEADS))).astype(BF16)
    P, f_logits = _proj(x2, norm_g, w[:, :3 * D_MODEL].astype(BF16), gain, flag, w_aux, batch, seq)
    return _fox_mixer(P, f_logits, b_f, batch, seq)


def _moba_layer(x2, norm_g, w, qg, kg, bias, batch, seq):
    gain, flag = _qkv_scales(qg, kg)
    P = _proj(x2, norm_g, w.astype(BF16), gain, flag, None, batch, seq)
    return _moba_mixer(P, bias, batch, seq)


def kernel(x, rel_table, attn_norm, mlp_norm, q_gain, k_gain, w_out, mlp_w_up, mlp_w_down, dsa_w_in, nsa_w_in,
           nsa_cmp_pos, nsa_cmp_w1, nsa_cmp_w2, fox_w_in, fox_b_f, moba_w_in):
    batch, seq, _ = x.shape
    depth = attn_norm.shape[0]
    x2 = x.reshape(batch * seq, D_MODEL)
    bias_dense = _tile_bias(rel_table, DENSE_TILE, 10, 1, None)
    for layer in range(depth):
        kind, r = layer % 4, layer // 4
        qg, kg, ng = q_gain[layer], k_gain[layer], attn_norm[layer]
        if kind == 0:
            mixed = _dilated_layer(x2, ng, dsa_w_in[r], qg, kg, rel_table, batch, seq)
        elif kind == 1:
            mixed = _nsa_layer(x2, ng, nsa_w_in[r], nsa_cmp_pos[r], nsa_cmp_w1[r], nsa_cmp_w2[r], qg, kg,
                               bias_dense, rel_table, batch, seq)
        elif kind == 2:
            mixed = _fox_layer(x2, ng, fox_w_in[r], fox_b_f[r], qg, kg, batch, seq)
        else:
            mixed = _moba_layer(x2, ng, moba_w_in[r], qg, kg, bias_dense, batch, seq)
        x2 = _out_mlp(x2, mixed.reshape(batch * seq, D_MODEL), w_out[layer].astype(BF16), mlp_norm[layer],
                      mlp_w_up[layer].astype(BF16), mlp_w_down[layer].astype(BF16))
    return x2.reshape(batch, seq, D_MODEL)
```

```python
import functools
import math

import jax
import jax.numpy as jnp
import numpy as np
from jax import lax
from jax.experimental import pallas as pl
from jax.experimental.pallas import tpu as pltpu

F32 = jnp.float32
BF16 = jnp.bfloat16

D_MODEL = 1024
N_HEADS = 16
HEAD_DIM = 64
D_FF = 4 * D_MODEL
RMS_EPS = 1e-6
MASK_NEG = -1e30
TINY = 1e-30
LOG2E = math.log2(math.e)
QK_SCALE = HEAD_DIM ** -0.5 * LOG2E

REL_BUCKETS = 32
REL_MAX_EXACT = REL_BUCKETS // 2
REL_MAX_DIST = 2048

DILATED_PATTERNS = ((128, 1), (512, 4), (2048, 16))
BAND_BLOCK = 128

NSA_KV_HEADS = 4
NSA_GROUP = N_HEADS // NSA_KV_HEADS
CMP_STRIDE = 16
CMP_BLOCK = 2 * CMP_STRIDE
CMP_HIDDEN = 256
SEL_BLOCK = 64
SEL_TOPK = 16
NSA_WINDOW = 512

MOBA_BLOCK = 256
MOBA_TOPK = 3

LANES = 128
VMEM_BYTES_V7X = 64 * 1024 * 1024
VMEM_HEADROOM = 6 * 1024 * 1024

DENSE_TILE = 256
SEL_NEG = -(2.0 ** 100)

_NT = (((1,), (1,)), ((), ()))


def _vmem_limit(block_bytes, temp_bytes):
    need = 2 * int(sum(block_bytes)) + int(temp_bytes) + (2 << 20)
    return int(min(need, VMEM_BYTES_V7X - VMEM_HEADROOM))


def _split3(v):
    a = v.astype(BF16)
    r = v - a.astype(F32)
    b = r.astype(BF16)
    c = (r - b.astype(F32)).astype(BF16)
    return a, b, c


def _proj_kernel(*refs, has_aux):
    if has_aux:
        x_ref, g_ref, w_ref, cg_ref, cf_ref, bd_ref, wa_ref, o_ref, oa_ref, h_scr = refs
    else:
        x_ref, g_ref, w_ref, cg_ref, cf_ref, bd_ref, o_ref, h_scr = refs

    @pl.when(pl.program_id(1) == 0)
    def _():
        x = x_ref[...]
        ms = jnp.mean(x * x, axis=-1, keepdims=True)
        h_scr[...] = (x * lax.rsqrt(ms + RMS_EPS) * g_ref[...]).astype(BF16)
        if has_aux:
            oa_ref[...] = jnp.dot(h_scr[...], wa_ref[...], preferred_element_type=F32)

    y_all = jnp.dot(h_scr[...], w_ref[...], preferred_element_type=F32)
    nb = bd_ref.shape[0]
    lane = lax.broadcasted_iota(jnp.int32, (y_all.shape[0], LANES), 1)
    for c in range(y_all.shape[1] // nb):
        y = y_all[:, c * nb:(c + 1) * nb]
        ms = jnp.dot((y * y).astype(BF16), bd_ref[...], preferred_element_type=F32)
        rs = lax.rsqrt(ms + RMS_EPS)
        out = y * jnp.where(cf_ref[:, c * nb:(c + 1) * nb] > 0.5, rs, 1.0) * cg_ref[:, c * nb:(c + 1) * nb]
        for pair in range(nb // LANES):
            both = out[:, pair * LANES:(pair + 1) * LANES]
            slot = (c * nb) // HEAD_DIM + 2 * pair
            o_ref[0, slot] = jnp.where(lane < HEAD_DIM, both, 0.0).astype(o_ref.dtype)
            o_ref[0, slot + 1] = jnp.where(lane < HEAD_DIM, pltpu.roll(both, HEAD_DIM, 1), 0.0).astype(o_ref.dtype)


def _proj_vt_kernel(x_ref, g_ref, w_ref, o_ref, h_scr):
    @pl.when(pl.program_id(1) == 0)
    def _():
        x = x_ref[...]
        ms = jnp.mean(x * x, axis=-1, keepdims=True)
        h_scr[...] = (x * lax.rsqrt(ms + RMS_EPS) * g_ref[...]).astype(BF16)

    y = jnp.dot(h_scr[...], w_ref[...], preferred_element_type=F32)
    tile = o_ref.shape[-1]
    lane = lax.broadcasted_iota(jnp.int32, (y.shape[0], LANES), 1)
    for pair in range(y.shape[1] // LANES):
        both = y[:, pair * LANES:(pair + 1) * LANES]
        for half, head in enumerate((both, pltpu.roll(both, HEAD_DIM, 1))):
            head_t = jnp.where(lane < HEAD_DIM, head, 0.0).T
            for kt in range(y.shape[0] // tile):
                o_ref[0, 2 * pair + half, kt] = head_t[:, kt * tile:(kt + 1) * tile].astype(o_ref.dtype)


def _proj_vt(x2, norm_g, w, batch, seq, tile):
    T = x2.shape[0]
    N = w.shape[1]
    tm, tn = 1024, 2 * LANES
    spb = seq // tm
    hpt = tn // HEAD_DIM
    blocks = [tm * D_MODEL * 4, D_MODEL * tn * 2, hpt * tm * LANES * 2]
    return pl.pallas_call(
        _proj_vt_kernel,
        grid=(T // tm, N // tn),
        in_specs=[
            pl.BlockSpec((tm, D_MODEL), lambda i, j: (i, 0)),
            pl.BlockSpec((1, D_MODEL), lambda i, j: (0, 0)),
            pl.BlockSpec((D_MODEL, tn), lambda i, j: (0, j)),
        ],
        out_specs=pl.BlockSpec((1, hpt, tm // tile, LANES, tile), lambda i, j: (i // spb, j, i % spb, 0, 0)),
        out_shape=jax.ShapeDtypeStruct((batch, N // HEAD_DIM, seq // tile, LANES, tile), BF16),
        scratch_shapes=[pltpu.VMEM((tm, D_MODEL), BF16)],
        compiler_params=pltpu.CompilerParams(
            dimension_semantics=("parallel", "arbitrary"),
            vmem_limit_bytes=_vmem_limit(blocks, tm * D_MODEL * 2 + 8 * tm * tn * 4)),
        name="proj_vt",
    )(x2, norm_g.reshape(1, D_MODEL), w)


def _proj(x2, norm_g, w, col_gain, col_flag, w_aux, batch, seq):
    T = x2.shape[0]
    N = w.shape[1]
    tm, nb = 1024, 2 * LANES
    tn = 2 * nb if N % (2 * nb) == 0 else nb
    spb = seq // tm
    bd = np.zeros((nb, nb), np.float32)
    for a in range(nb // HEAD_DIM):
        bd[a * HEAD_DIM:(a + 1) * HEAD_DIM, a * HEAD_DIM:(a + 1) * HEAD_DIM] = 1.0 / HEAD_DIM
    bd = jnp.asarray(bd, BF16)
    has_aux = w_aux is not None
    in_specs = [
        pl.BlockSpec((tm, D_MODEL), lambda i, j: (i, 0)),
        pl.BlockSpec((1, D_MODEL), lambda i, j: (0, 0)),
        pl.BlockSpec((D_MODEL, tn), lambda i, j: (0, j)),
        pl.BlockSpec((1, tn), lambda i, j: (0, j)),
        pl.BlockSpec((1, tn), lambda i, j: (0, j)),
        pl.BlockSpec((nb, nb), lambda i, j: (0, 0)),
    ]
    args = [x2, norm_g.reshape(1, D_MODEL), w, col_gain, col_flag, bd]
    hpt = tn // HEAD_DIM
    out_shape = [jax.ShapeDtypeStruct((batch, N // HEAD_DIM, seq, LANES), BF16)]
    out_specs = [pl.BlockSpec((1, hpt, tm, LANES), lambda i, j: (i // spb, j, i % spb, 0))]
    if has_aux:
        in_specs.append(pl.BlockSpec((D_MODEL, LANES), lambda i, j: (0, 0)))
        args.append(w_aux)
        out_shape.append(jax.ShapeDtypeStruct((T, LANES), F32))
        out_specs.append(pl.BlockSpec((tm, LANES), lambda i, j: (i, 0)))
    blocks = [tm * D_MODEL * 4, D_MODEL * tn * 2, hpt * tm * LANES * 2, nb * nb * 2, tm * LANES * 4,
              D_MODEL * LANES * 2]
    res = pl.pallas_call(
        functools.partial(_proj_kernel, has_aux=has_aux),
        grid=(T // tm, N // tn),
        in_specs=in_specs,
        out_specs=out_specs,
        out_shape=out_shape,
        scratch_shapes=[pltpu.VMEM((tm, D_MODEL), BF16)],
        compiler_params=pltpu.CompilerParams(
            dimension_semantics=("parallel", "arbitrary"),
            vmem_limit_bytes=_vmem_limit(blocks, tm * D_MODEL * 2 + 6 * tm * tn * 4)),
        name="proj",
    )(*args)
    return res if has_aux else res[0]


def _out_mlp_kernel(x_ref, m_ref, wo_ref, g_ref, wu_ref, wd_ref, o_ref, h_scr):
    @pl.when(pl.program_id(1) == 0)
    def _():
        x1 = x_ref[...] + jnp.dot(m_ref[...], wo_ref[...], preferred_element_type=F32)
        o_ref[...] = x1
        ms = jnp.mean(x1 * x1, axis=-1, keepdims=True)
        h_scr[...] = (x1 * lax.rsqrt(ms + RMS_EPS) * g_ref[...]).astype(BF16)

    u = jnp.dot(h_scr[...], wu_ref[...], preferred_element_type=F32)
    a = jnp.square(jnp.maximum(u, 0.0)).astype(BF16)
    o_ref[...] += jnp.dot(a, wd_ref[...], preferred_element_type=F32)


def _out_mlp(x2, mixed, w_out, norm_g, w_up, w_down):
    T = x2.shape[0]
    tm, tf = 1024, 1024
    blocks = [tm * D_MODEL * 4, tm * D_MODEL * 2, D_MODEL * D_MODEL * 2,
              D_MODEL * tf * 2, tf * D_MODEL * 2, tm * D_MODEL * 4]
    return pl.pallas_call(
        _out_mlp_kernel,
        grid=(T // tm, D_FF // tf),
        in_specs=[
            pl.BlockSpec((tm, D_MODEL), lambda i, f: (i, 0)),
            pl.BlockSpec((tm, D_MODEL), lambda i, f: (i, 0)),
            pl.BlockSpec((D_MODEL, D_MODEL), lambda i, f: (0, 0)),
            pl.BlockSpec((1, D_MODEL), lambda i, f: (0, 0)),
            pl.BlockSpec((D_MODEL, tf), lambda i, f: (0, f)),
            pl.BlockSpec((tf, D_MODEL), lambda i, f: (f, 0)),
        ],
        out_specs=pl.BlockSpec((tm, D_MODEL), lambda i, f: (i, 0)),
        out_shape=jax.ShapeDtypeStruct((T, D_MODEL), F32),
        scratch_shapes=[pltpu.VMEM((tm, D_MODEL), BF16)],
        compiler_params=pltpu.CompilerParams(
            dimension_semantics=("parallel", "arbitrary"),
            vmem_limit_bytes=_vmem_limit(blocks, tm * D_MODEL * 2 + 3 * tm * tf * 4 + 2 * tm * D_MODEL * 4)),
        name="out_mlp",
    )(x2, mixed, w_out, norm_g.reshape(1, D_MODEL), w_up, w_down)


def _dense_attn_kernel(*refs, tile, n_kv, group, n_qt, n_back, has_bias, n_bias, diag_causal):
    if has_bias:
        q_ref, k_ref, vt_ref, b_ref, o_ref, acc_scr, s_scr, p_scr, st_scr = refs
    else:
        q_ref, k_ref, vt_ref, o_ref, acc_scr, s_scr, p_scr, st_scr = refs
        b_ref = None
    chunk = pl.program_id(2)
    n_streams = n_kv * group
    if diag_causal:
        visible = (lax.broadcasted_iota(jnp.int32, (tile, tile), 0)
                   <= lax.broadcasted_iota(jnp.int32, (tile, tile), 1))

    def one_query_tile(t, _):
        qi = chunk * n_qt + t
        row0 = pl.multiple_of(t * tile, tile)
        qs = [q_ref[0, h, pl.ds(row0, tile), :].astype(F32).T.astype(BF16) for h in range(n_streams)]

        def issue_scores(d, slot):
            start = pl.multiple_of(jnp.maximum(qi - d, 0) * tile, tile)
            for h in range(n_streams):
                s = jnp.dot(k_ref[0, h // group, pl.ds(start, tile), :], qs[h],
                            preferred_element_type=F32)
                if has_bias:
                    s = s + b_ref[h, jnp.minimum(d, n_bias - 1)]
                s_scr[slot, h] = s

        def pv_matmuls(d, slot):
            return [jnp.dot(vt_ref[0, h // group, qi - d], p_scr[slot, h], preferred_element_type=F32)
                    for h in range(n_streams)]

        def accumulate(slot, pvs):
            for h in range(n_streams):
                acc_scr[h] = st_scr[2 + slot, h] * acc_scr[h] + pvs[h]

        def softmax(slot, first):
            for h in range(n_streams):
                s = s_scr[slot, h]
                if first:
                    if diag_causal:
                        s = jnp.where(visible, s, MASK_NEG)
                    m = jnp.max(s, axis=0, keepdims=True)
                    p = jnp.exp2(s - m)
                    st_scr[0, h] = m
                    st_scr[1, h] = jnp.sum(p, axis=0, keepdims=True)
                    st_scr[3 - slot, h] = jnp.zeros_like(m)
                else:
                    m0 = st_scr[0, h]
                    m = jnp.maximum(m0, jnp.max(s, axis=0, keepdims=True))
                    alpha = jnp.exp2(m0 - m)
                    p = jnp.exp2(s - m)
                    st_scr[0, h] = m
                    st_scr[1, h] = alpha * st_scr[1, h] + jnp.sum(p, axis=0, keepdims=True)
                    st_scr[3 - slot, h] = alpha
                p_scr[1 - slot, h] = p.astype(BF16)

        def tile_step(d, slot):
            pvs = pv_matmuls(d - 1, slot)
            issue_scores(d + 1, 1 - slot)
            softmax(slot, False)
            accumulate(slot, pvs)

        acc_scr[...] = jnp.zeros_like(acc_scr)
        issue_scores(0, 0)
        issue_scores(1, 1)
        softmax(0, True)
        n_steps = qi + 1 if n_back is None else jnp.minimum(qi, n_back) + 1

        def two_steps(j, _):
            tile_step(1 + 2 * j, 1)
            tile_step(2 + 2 * j, 0)
            return 0

        lax.fori_loop(0, (n_steps - 1) >> 1, two_steps, 0)
        odd_tail = ((n_steps - 1) & 1) == 1

        @pl.when(odd_tail)
        def _():
            tile_step(n_steps - 1, 1)
            accumulate(0, pv_matmuls(n_steps - 1, 0))

        @pl.when(jnp.logical_not(odd_tail))
        def _():
            accumulate(1, pv_matmuls(n_steps - 1, 1))

        outs = [acc_scr[h][:HEAD_DIM] / jnp.maximum(st_scr[1, h], TINY) for h in range(n_streams)]
        for pair in range(n_streams // 2):
            both = jnp.concatenate([outs[2 * pair], outs[2 * pair + 1]], axis=0)
            o_ref[0, pl.ds(row0, tile), pair * LANES:(pair + 1) * LANES] = both.T.astype(o_ref.dtype)
        return 0

    lax.fori_loop(0, n_qt, one_query_tile, 0)


def _value_tiles(v, tile):
    B, H, S, _ = v.shape
    return jnp.swapaxes(v.reshape(B, H, S // tile, tile, LANES), -1, -2)


def _dense_attn(q_arr, q_off, k_arr, k_off, vt_arr, v_off, bias, *, n_kv_heads, group, tile, n_kv, q_chunk,
                n_back, diag_causal, out_dtype, name):
    B, _, S, _ = q_arr.shape
    n_q_heads = n_kv_heads * group
    nq = n_kv * group
    assert q_off % nq == 0 and k_off % n_kv == 0 and v_off % n_kv == 0
    assert S % q_chunk == 0 and q_chunk % tile == 0 and n_kv_heads % n_kv == 0
    n_qt = q_chunk // tile
    has_bias = bias is not None
    n_bias = bias.shape[1] if has_bias else 0
    in_specs = [
        pl.BlockSpec((1, nq, q_chunk, LANES), lambda u, b, c: (b, q_off // nq + u, c, 0)),
        pl.BlockSpec((1, n_kv, S, LANES), lambda u, b, c: (b, k_off // n_kv + u, 0, 0)),
        pl.BlockSpec((1, n_kv, S // tile, LANES, tile), lambda u, b, c: (b, v_off // n_kv + u, 0, 0, 0)),
    ]
    args = [q_arr, k_arr, vt_arr]
    out_bytes = jnp.dtype(out_dtype).itemsize
    blocks = [nq * q_chunk * LANES * 2, 2 * n_kv * S * LANES * 2, nq * q_chunk * LANES * out_bytes]
    if has_bias:
        in_specs.append(pl.BlockSpec((nq, n_bias, tile, tile), lambda u, b, c: (u, 0, 0, 0)))
        args.append(bias)
        blocks.append(nq * n_bias * tile * tile * 4)
    kern = functools.partial(_dense_attn_kernel, tile=tile, n_kv=n_kv, group=group, n_qt=n_qt, n_back=n_back,
                             has_bias=has_bias, n_bias=n_bias, diag_causal=diag_causal)
    assert nq % 2 == 0
    return pl.pallas_call(
        kern,
        grid=(n_kv_heads // n_kv, B, S // q_chunk),
        in_specs=in_specs,
        out_specs=pl.BlockSpec((1, q_chunk, nq * HEAD_DIM), lambda u, b, c: (b, c, u)),
        out_shape=jax.ShapeDtypeStruct((B, S, n_q_heads * HEAD_DIM), out_dtype),
        scratch_shapes=[pltpu.VMEM((nq, LANES, tile), F32),
                        pltpu.VMEM((2, nq, tile, tile), F32),
                        pltpu.VMEM((2, nq, tile, tile), BF16),
                        pltpu.VMEM((4, nq, 1, tile), F32)],
        compiler_params=pltpu.CompilerParams(
            dimension_semantics=("parallel", "parallel", "arbitrary"),
            vmem_limit_bytes=_vmem_limit(blocks, nq * (4 * tile * tile + 3 * LANES * tile) * 4)),
        name=name,
    )(*args)


def _bucket_np(dist):
    d = np.maximum(dist, 0)
    df = np.maximum(d.astype(np.float32), np.float32(1.0))
    large = REL_MAX_EXACT + (np.log(df / np.float32(REL_MAX_EXACT))
                             / np.float32(math.log(REL_MAX_DIST / REL_MAX_EXACT))
                             * np.float32(REL_BUCKETS - REL_MAX_EXACT)).astype(np.int32)
    large = np.minimum(large, REL_BUCKETS - 1)
    return np.where(d < REL_MAX_EXACT, d, large).astype(np.int32)


def _dist_values(rel_table, dist, ok):
    vals = rel_table[jnp.asarray(_bucket_np(dist))] * LOG2E
    return jnp.where(jnp.asarray(ok)[:, None], vals, MASK_NEG).T


def _skew_kernel(w_ref, o_ref, *, step, col0, mask_last_row):
    _, n_out, n_rows, width = o_ref.shape
    x = jnp.broadcast_to(w_ref[0], (n_rows, w_ref.shape[2]))
    x = pltpu.roll(x, 0, 1, stride=step, stride_axis=0)
    if mask_last_row:
        row = lax.broadcasted_iota(jnp.int32, x.shape, 0)
        x = jnp.where(row == n_rows - 1, MASK_NEG, x)
    for a in range(n_out):
        o_ref[0, a] = x[:, col0 + a * width:col0 + (a + 1) * width]


def _skew(w, n_rows, step, col0, n_out, width, mask_last_row=False):
    H, W = w.shape
    assert W % LANES == 0 and col0 % LANES == 0 and width % LANES == 0 and col0 + n_out * width <= W
    assert col0 - step * (n_rows - 1) >= 0
    return pl.pallas_call(
        functools.partial(_skew_kernel, step=step, col0=col0, mask_last_row=mask_last_row),
        grid=(H,),
        in_specs=[pl.BlockSpec((1, 1, W), lambda h: (h, 0, 0))],
        out_specs=pl.BlockSpec((1, n_out, n_rows, width), lambda h: (h, 0, 0, 0)),
        out_shape=jax.ShapeDtypeStruct((H, n_out, n_rows, width), F32),
        compiler_params=pltpu.CompilerParams(
            dimension_semantics=("parallel",),
            vmem_limit_bytes=_vmem_limit([W * 4, n_out * n_rows * width * 4], 3 * n_rows * W * 4)),
        name="bias_skew",
    )(w.reshape(H, 1, W))


def _tile_bias(rel_table, tile, n_tiles, dist_scale, max_dist):
    d = np.arange((n_tiles + 1) * tile) - tile
    ok = d >= 0
    if max_dist is not None:
        ok &= d <= max_dist
    w = _dist_values(rel_table, d * dist_scale, ok)
    return _skew(w, tile, 1, tile, n_tiles, tile)


def _merge3_kernel(a_ref, b_ref, c_ref, o_ref):
    merged = []
    for h in range(2):
        parts = [r[0, h] for r in (a_ref, b_ref, c_ref)]
        ms = [p[:, HEAD_DIM:HEAD_DIM + 1] for p in parts]
        ls = [p[:, HEAD_DIM + 1:HEAD_DIM + 2] for p in parts]
        m_all = jnp.maximum(jnp.maximum(ms[0], ms[1]), ms[2])
        ws = [jnp.exp2(m - m_all) for m in ms]
        num = ws[0] * parts[0] + ws[1] * parts[1] + ws[2] * parts[2]
        den = ws[0] * ls[0] + ws[1] * ls[1] + ws[2] * ls[2]
        merged.append(num / jnp.maximum(den, TINY))
    lane = lax.broadcasted_iota(jnp.int32, merged[0].shape, 1)
    o_ref[0] = jnp.where(lane < HEAD_DIM, merged[0], pltpu.roll(merged[1], HEAD_DIM, 1)).astype(o_ref.dtype)


def _merge3(parts):
    B, H, S, _ = parts[0].shape
    ts = 2048
    spec = pl.BlockSpec((1, 2, ts, LANES), lambda b, p, s: (b, p, s, 0))
    return pl.pallas_call(
        _merge3_kernel,
        grid=(B, H // 2, S // ts),
        in_specs=[spec, spec, spec],
        out_specs=pl.BlockSpec((1, ts, LANES), lambda b, p, s: (b, s, p)),
        out_shape=jax.ShapeDtypeStruct((B, S, H * HEAD_DIM), BF16),
        compiler_params=pltpu.CompilerParams(
            dimension_semantics=("parallel", "parallel", "parallel"),
            vmem_limit_bytes=_vmem_limit([6 * ts * LANES * 4, ts * LANES * 2], 12 * ts * LANES * 4)),
        name="dilated_merge",
    )(*parts)


def _band_attn_kernel(q_ref, k_ref, vt_ref, b_ref, o_ref, *, tile, n_heads, n_qt):
    chunk = pl.program_id(2)
    channel = lax.broadcasted_iota(jnp.int32, (LANES, tile), 0)

    def one_query_tile(t, _):
        qi = chunk * n_qt + t
        row0 = pl.multiple_of(t * tile, tile)
        k0 = jnp.maximum(qi - 1, 0)
        start = pl.multiple_of(k0 * tile, tile)
        case = jnp.where(qi == 0, 1, 0)
        scores = []
        for h in range(n_heads):
            q_t = q_ref[0, h, pl.ds(row0, tile), :].astype(F32).T.astype(BF16)
            scores.append(jnp.dot(k_ref[0, h, pl.ds(start, 2 * tile), :], q_t, preferred_element_type=F32)
                          + b_ref[h, case])
        stats = []
        for h in range(n_heads):
            m = jnp.max(scores[h], axis=0, keepdims=True)
            p = jnp.exp2(scores[h] - m)
            stats.append((m, jnp.sum(p, axis=0, keepdims=True), p.astype(BF16)))
        for h in range(n_heads):
            m, l, p = stats[h]
            acc = (jnp.dot(vt_ref[0, h, k0], p[:tile], preferred_element_type=F32)
                   + jnp.dot(vt_ref[0, h, k0 + 1], p[tile:], preferred_element_type=F32))
            out_t = jnp.where(channel == HEAD_DIM, m, jnp.where(channel == HEAD_DIM + 1, l, acc))
            o_ref[0, h, pl.ds(row0, tile), :] = out_t.T
        return 0

    lax.fori_loop(0, n_qt, one_query_tile, 0)


def _band_attn(P, vt, bias2, n_heads, q_chunk, name):
    B, _, n, _ = P.shape
    tile = bias2.shape[-1]
    assert n >= 2 * tile and n % q_chunk == 0 and q_chunk % tile == 0 and N_HEADS % n_heads == 0
    kern = functools.partial(_band_attn_kernel, tile=tile, n_heads=n_heads, n_qt=q_chunk // tile)
    groups = N_HEADS // n_heads
    blocks = [n_heads * q_chunk * LANES * 2, 2 * n_heads * n * LANES * 2, n_heads * 2 * 2 * tile * tile * 4,
              n_heads * q_chunk * LANES * 4]
    return pl.pallas_call(
        kern,
        grid=(groups, B, n // q_chunk),
        in_specs=[
            pl.BlockSpec((1, n_heads, q_chunk, LANES), lambda u, b, c: (b, u, c, 0)),
            pl.BlockSpec((1, n_heads, n, LANES), lambda u, b, c: (b, groups + u, 0, 0)),
            pl.BlockSpec((1, n_heads, n // tile, LANES, tile), lambda u, b, c: (b, u, 0, 0, 0)),
            pl.BlockSpec((n_heads, 2, 2 * tile, tile), lambda u, b, c: (u, 0, 0, 0)),
        ],
        out_specs=pl.BlockSpec((1, n_heads, q_chunk, LANES), lambda u, b, c: (b, u, c, 0)),
        out_shape=jax.ShapeDtypeStruct((B, N_HEADS, n, LANES), F32),
        compiler_params=pltpu.CompilerParams(
            dimension_semantics=("parallel", "parallel", "arbitrary"),
            vmem_limit_bytes=_vmem_limit(blocks, n_heads * 6 * 2 * tile * tile * 4)),
        name=name,
    )(P, P, vt, bias2)


def _dilated_mixer(P, rel_table, batch, seq):
    parts = []
    for window, dil in DILATED_PATTERNS:
        n = seq // dil
        if dil == 1:
            Pr = P
        else:
            Pr = P.reshape(batch, 3 * N_HEADS, n, dil, LANES).transpose(0, 3, 1, 2, 4)
            Pr = Pr.reshape(batch * dil, 3 * N_HEADS, n, LANES)
        bias = _tile_bias(rel_table, BAND_BLOCK, 2, dil, window // dil)
        general = jnp.concatenate([bias[:, 1], bias[:, 0]], axis=1)
        first = jnp.concatenate([bias[:, 0], jnp.full_like(bias[:, 0], MASK_NEG)], axis=1)
        vt = _value_tiles(Pr[:, 2 * N_HEADS:], BAND_BLOCK)
        o = _band_attn(Pr, vt, jnp.stack([general, first], axis=1), N_HEADS // 2 if dil == 1 else N_HEADS,
                       min(n, 512), f"dilated_attn_d{dil}")
        if dil > 1:
            o = o.reshape(batch, dil, N_HEADS, n, LANES).transpose(0, 2, 3, 1, 4).reshape(batch, N_HEADS, seq, LANES)
        parts.append(o)
    return _merge3(parts)


def _compress_kernel(x_ref, w1_ref, w2_ref, pos_ref, g_ref, o_ref):
    posw = jnp.dot(pos_ref[0], w1_ref[0], preferred_element_type=F32)[0:1]
    pre = jnp.dot(x_ref[0, 0], w1_ref[0], preferred_element_type=F32) + posw
    hid = 0.5 * pre * (1.0 + jnp.tanh(math.sqrt(2.0 / math.pi) * (pre + 0.044715 * (pre * pre * pre))))
    out = jnp.dot(hid.astype(BF16), w2_ref[0], preferred_element_type=F32)
    ms = jnp.sum(out * out, axis=-1, keepdims=True) * (1.0 / HEAD_DIM)
    normed = out * lax.rsqrt(ms + RMS_EPS) * g_ref[...]
    o_ref[0, 0] = jnp.where(pl.program_id(0) == 0, normed, out).astype(o_ref.dtype)


def _compress(blocks, w1, w2, pos, k_gain_pad):
    _, BK, n, K = blocks.shape
    return pl.pallas_call(
        _compress_kernel,
        grid=(2, BK),
        in_specs=[
            pl.BlockSpec((1, 1, n, K), lambda a, i: (a, i, 0, 0)),
            pl.BlockSpec((1, K, CMP_HIDDEN), lambda a, i: (a, 0, 0)),
            pl.BlockSpec((1, CMP_HIDDEN, LANES), lambda a, i: (a, 0, 0)),
            pl.BlockSpec((1, 8, K), lambda a, i: (a, 0, 0)),
            pl.BlockSpec((1, LANES), lambda a, i: (0, 0)),
        ],
        out_specs=pl.BlockSpec((1, 1, n, LANES), lambda a, i: (a, i, 0, 0)),
        out_shape=jax.ShapeDtypeStruct((2, BK, n, LANES), BF16),
        compiler_params=pltpu.CompilerParams(
            dimension_semantics=("parallel", "parallel"),
            vmem_limit_bytes=_vmem_limit([n * K * 2, K * CMP_HIDDEN * 2, CMP_HIDDEN * LANES * 2, 8 * K * 2,
                                          n * LANES * 2], 8 * n * CMP_HIDDEN * 4)),
        name="nsa_compress",
    )(blocks, w1, w2, pos, k_gain_pad)


def _rank_select(vals, own_row, n_blocks, top):
    jidx = lax.broadcasted_iota(jnp.int32, vals.shape, 0)
    cnt = jnp.zeros(vals.shape, F32)
    for j in range(n_blocks):
        row = vals[j:j + 1, :]
        ahead = (row > vals) | ((row == vals) & (jidx > j))
        cnt = cnt + jnp.where(ahead, 1.0, 0.0)
    return (jidx < own_row) & (cnt < top)


def _cmp_attn_kernel(q_ref, k_ref, vt_ref, b_ref, ov_ref, o_ref, qa_ref, *, tq):
    qi = pl.program_id(1)
    k = k_ref[0, 0]
    vt = vt_ref[0]
    psum = None
    outs = []
    for g in range(NSA_GROUP):
        s = lax.dot_general(k, q_ref[0, g], _NT, preferred_element_type=F32) + b_ref[g]
        m = jnp.max(s, axis=0, keepdims=True)
        p = jnp.where(m > 0.5 * MASK_NEG, jnp.exp2(s - m), 0.0)
        l = jnp.sum(p, axis=0, keepdims=True)
        probs = p / jnp.maximum(l, TINY)
        outs.append(jnp.dot(vt, probs.astype(BF16), preferred_element_type=F32)[:HEAD_DIM])
        psum = probs if psum is None else psum + probs
    for pair in range(NSA_GROUP // 2):
        both = jnp.concatenate([outs[2 * pair], outs[2 * pair + 1]], axis=0)
        o_ref[0, :, pair * LANES:(pair + 1) * LANES] = both.T.astype(o_ref.dtype)
    imp = None
    for part in _split3(psum):
        term = jnp.dot(ov_ref[...], part, preferred_element_type=F32)
        imp = term if imp is None else imp + term
    n_sel = imp.shape[0]
    pos = qi * tq + lax.broadcasted_iota(jnp.int32, (1, tq), 1)
    own = pos // SEL_BLOCK
    jidx = lax.broadcasted_iota(jnp.int32, imp.shape, 0)
    vals = jnp.where(jidx < own, imp, -1.0)
    chosen = _rank_select(vals, own, n_sel, SEL_TOPK - 1) | (jidx == own)
    sel_cols = jnp.concatenate([jnp.zeros((HEAD_DIM, tq), F32), jnp.where(chosen, 0.0, SEL_NEG)], axis=0).T
    for g in range(NSA_GROUP):
        qa_ref[0, g] = (q_ref[0, g].astype(F32) + sel_cols).astype(qa_ref.dtype)


def _cmp_attn(P, kv_cmp, vt_cmp, bias_c, overlap_t, batch, seq):
    assert seq // SEL_BLOCK == LANES - HEAD_DIM
    tq = 256
    n_cmp = kv_cmp.shape[2]
    n_sel = overlap_t.shape[0]
    kern = functools.partial(_cmp_attn_kernel, tq=tq)
    return pl.pallas_call(
        kern,
        grid=(NSA_KV_HEADS, seq // tq, batch),
        in_specs=[
            pl.BlockSpec((1, NSA_GROUP, tq, LANES), lambda kh, qi, b: (b, kh, qi, 0)),
            pl.BlockSpec((1, 1, n_cmp, LANES), lambda kh, qi, b: (0, b * NSA_KV_HEADS + kh, 0, 0)),
            pl.BlockSpec((1, LANES, n_cmp), lambda kh, qi, b: (b * NSA_KV_HEADS + kh, 0, 0)),
            pl.BlockSpec((NSA_GROUP, n_cmp, tq), lambda kh, qi, b: (kh, 0, qi)),
            pl.BlockSpec((n_sel, n_cmp), lambda kh, qi, b: (0, 0)),
        ],
        out_specs=[
            pl.BlockSpec((1, tq, NSA_GROUP * HEAD_DIM), lambda kh, qi, b: (b, qi, kh)),
            pl.BlockSpec((1, NSA_GROUP, tq, LANES), lambda kh, qi, b: (b, kh, qi, 0)),
        ],
        out_shape=[
            jax.ShapeDtypeStruct((batch, seq, N_HEADS * HEAD_DIM), F32),
            jax.ShapeDtypeStruct((batch, N_HEADS, seq, LANES), BF16),
        ],
        compiler_params=pltpu.CompilerParams(
            dimension_semantics=("parallel", "parallel", "arbitrary"),
            vmem_limit_bytes=_vmem_limit([NSA_GROUP * tq * LANES * 2, 2 * n_cmp * LANES * 2,
                                          NSA_GROUP * tq * n_cmp * 4, NSA_GROUP * tq * LANES * 4, n_sel * tq * 2],
                                         16 * tq * n_cmp * 4)),
        name="nsa_cmp_attn",
    )(P, kv_cmp, vt_cmp, bias_c, overlap_t)


def _nsa_merge_kernel(c_ref, s_ref, w_ref, g_ref, o_ref):
    gates = jax.nn.sigmoid(g_ref[0])
    lane = lax.broadcasted_iota(jnp.int32, gates.shape, 1)
    for pair in range(N_HEADS // 2):
        cols = slice(pair * LANES, (pair + 1) * LANES)

        def gate(branch):
            a, b = 3 * (2 * pair) + branch, 3 * (2 * pair + 1) + branch
            return jnp.where(lane < HEAD_DIM, gates[:, a:a + 1], gates[:, b:b + 1])

        o = gate(0) * c_ref[0, :, cols] + gate(1) * s_ref[0, :, cols] + gate(2) * w_ref[0, :, cols]
        o_ref[0, :, cols] = o.astype(o_ref.dtype)


def _nsa_merge(o_cmp, o_sel, o_win, gate_logits, batch, seq):
    ts = 256
    spec = pl.BlockSpec((1, ts, D_MODEL), lambda b, s: (b, s, 0))
    return pl.pallas_call(
        _nsa_merge_kernel,
        grid=(batch, seq // ts),
        in_specs=[spec, spec, spec, pl.BlockSpec((1, ts, LANES), lambda b, s: (b, s, 0))],
        out_specs=spec,
        out_shape=jax.ShapeDtypeStruct((batch, seq, D_MODEL), BF16),
        compiler_params=pltpu.CompilerParams(
            dimension_semantics=("parallel", "parallel"),
            vmem_limit_bytes=_vmem_limit([3 * ts * D_MODEL * 4, ts * LANES * 4, ts * D_MODEL * 2],
                                         8 * ts * LANES * 4)),
        name="nsa_merge",
    )(o_cmp, o_sel, o_win, gate_logits.reshape(batch, seq, LANES))


def _nsa_mixer(P, vt, gate_logits, cmp_pos, cmp_w1, cmp_w2, k_gain, bias_sel, rel_table, batch, seq):
    KH = NSA_KV_HEADS
    n_chunk = seq // CMP_STRIDE
    n_sel = seq // SEL_BLOCK

    raw = P[:, N_HEADS:N_HEADS + 2 * KH, :, :HEAD_DIM].reshape(batch, 2, KH, n_chunk, CMP_STRIDE * HEAD_DIM)
    raw = raw.transpose(1, 0, 2, 3, 4).reshape(2, batch * KH, n_chunk, CMP_STRIDE * HEAD_DIM)
    blocks = jnp.concatenate([raw, jnp.roll(raw, -1, axis=2)], axis=-1)
    pos = jnp.broadcast_to(cmp_pos.reshape(2, 1, CMP_BLOCK * HEAD_DIM), (2, 8, CMP_BLOCK * HEAD_DIM)).astype(BF16)
    w2 = jnp.pad(cmp_w2, ((0, 0), (0, 0), (0, LANES - HEAD_DIM))).astype(BF16)
    k_gain_pad = jnp.pad(k_gain, (0, LANES - HEAD_DIM)).reshape(1, LANES)
    kv_cmp = _compress(blocks, cmp_w1.astype(BF16), w2, pos, k_gain_pad)

    d = np.arange(2 * seq) - (seq + CMP_BLOCK - 1)
    w = _dist_values(rel_table, d, d >= 0)
    bias_c = _skew(w, n_chunk, CMP_STRIDE, seq, 1, seq, mask_last_row=True)[:, 0]
    c_start = np.arange(n_chunk)[None, :] * CMP_STRIDE
    s_start = np.arange(n_sel)[:, None] * SEL_BLOCK
    overlap_t = ((c_start < s_start + SEL_BLOCK) & (c_start + CMP_BLOCK > s_start)
                 & (np.arange(n_chunk)[None, :] < n_chunk - 1))
    overlap_t = jnp.asarray(overlap_t.astype(np.float32), BF16)
    o_cmp, q_aug = _cmp_attn(P, kv_cmp, jnp.swapaxes(kv_cmp[1], -1, -2), bias_c, overlap_t, batch, seq)

    lane = jnp.arange(LANES)
    onehot = (np.arange(seq)[:, None] // SEL_BLOCK == np.arange(LANES - HEAD_DIM)[None, :]).astype(np.float32)
    onehot = jnp.asarray(np.pad(onehot, ((0, 0), (HEAD_DIM, 0))), BF16)
    k_aug = jnp.where(lane < HEAD_DIM, P[:, 24:28], onehot[None, None])
    o_sel = _dense_attn(q_aug, 0, k_aug, 0, vt, 0, bias_sel, n_kv_heads=KH,
                        group=NSA_GROUP, tile=DENSE_TILE, n_kv=1, q_chunk=DENSE_TILE, n_back=None,
                        diag_causal=False, out_dtype=F32, name="nsa_sel_attn")
    bias_win = _tile_bias(rel_table, DENSE_TILE, 3, 1, NSA_WINDOW - 1)
    o_win = _dense_attn(q_aug, 0, P, 28, vt, KH, bias_win, n_kv_heads=KH,
                        group=NSA_GROUP, tile=DENSE_TILE, n_kv=1, q_chunk=DENSE_TILE, n_back=2,
                        diag_causal=False, out_dtype=F32, name="nsa_win_attn")
    return _nsa_merge(o_cmp, o_sel, o_win, gate_logits, batch, seq)


def _fox_decay_kernel(f_ref, bf_ref, tri_ref, q_ref, k_ref, eq_ref, ek_ref, carry):
    @pl.when(pl.program_id(1) == 0)
    def _():
        carry[...] = jnp.zeros_like(carry)

    z = f_ref[0] + bf_ref[...]
    log_f = jnp.minimum(z, 0.0) - jnp.log1p(jnp.exp(-jnp.abs(z)))
    c = carry[0:1, :]
    for part in _split3(log_f):
        c = c + jnp.dot(tri_ref[...], part, preferred_element_type=F32)
    ts = c.shape[0]
    carry[...] = jnp.broadcast_to(c[ts - 1:ts, :], carry.shape)
    lane = lax.broadcasted_iota(jnp.int32, (ts, LANES), 1) - HEAD_DIM
    c = c * LOG2E
    for h in range(N_HEADS):
        c1, c2, c3 = _split3(c[:, h:h + 1])
        c1, c2, c3 = c1.astype(F32), c2.astype(F32), c3.astype(F32)
        eq = jnp.where(lane == 0, c1, jnp.where(lane == 1, c2, jnp.where(lane == 2, c3,
                       jnp.where((lane >= 3) & (lane < 6), 1.0, 0.0))))
        ek = jnp.where(lane == 3, -c1, jnp.where(lane == 4, -c2, jnp.where(lane == 5, -c3,
                       jnp.where((lane >= 0) & (lane < 3), 1.0, 0.0))))
        eq_ref[0, h] = (q_ref[0, h].astype(F32) + eq).astype(eq_ref.dtype)
        ek_ref[0, h] = (k_ref[0, h].astype(F32) + ek).astype(ek_ref.dtype)


def _fox_decay(P, f_logits, b_f, batch, seq):
    ts = 256
    tri = jnp.asarray(np.tril(np.ones((ts, ts), np.float32)), BF16)
    spec = pl.BlockSpec((1, N_HEADS, ts, LANES), lambda b, s: (b, 0, s, 0))
    return pl.pallas_call(
        _fox_decay_kernel,
        grid=(batch, seq // ts),
        in_specs=[
            pl.BlockSpec((1, ts, LANES), lambda b, s: (b, s, 0)),
            pl.BlockSpec((1, LANES), lambda b, s: (0, 0)),
            pl.BlockSpec((ts, ts), lambda b, s: (0, 0)),
            spec,
            pl.BlockSpec((1, N_HEADS, ts, LANES), lambda b, s: (b, 1, s, 0)),
        ],
        out_specs=[spec, spec],
        out_shape=[jax.ShapeDtypeStruct((batch, N_HEADS, seq, LANES), BF16)] * 2,
        scratch_shapes=[pltpu.VMEM((8, LANES), F32)],
        compiler_params=pltpu.CompilerParams(
            dimension_semantics=("parallel", "arbitrary"),
            vmem_limit_bytes=_vmem_limit([ts * LANES * 4, ts * ts * 2, 4 * N_HEADS * ts * LANES * 2],
                                         16 * ts * LANES * 4)),
        name="fox_decay",
    )(f_logits.reshape(batch, seq, LANES), jnp.pad(b_f, (0, LANES - N_HEADS)).reshape(1, LANES), tri, P, P)


def _fox_mixer(P, vt, f_logits, b_f, batch, seq):
    q_aug, k_aug = _fox_decay(P, f_logits, b_f, batch, seq)
    return _dense_attn(q_aug, 0, k_aug, 0, vt, 0, None, n_kv_heads=N_HEADS, group=1, tile=DENSE_TILE,
                       n_kv=4, q_chunk=2 * DENSE_TILE, n_back=None, diag_causal=True,
                       out_dtype=BF16, name="fox_attn")


def _moba_gate_kernel(q_ref, k_ref, qa_ref, *, n_blk):
    k = k_ref[0, 0].astype(F32)
    S = k.shape[0]
    k_mean = jnp.sum(k.reshape(n_blk, S // n_blk, LANES), axis=1) * (1.0 / (S // n_blk))
    hi = k_mean.astype(BF16)
    lo = (k_mean - hi.astype(F32)).astype(BF16)
    q = q_ref[0, 0]
    gate = (lax.dot_general(hi, q, _NT, preferred_element_type=F32)
            + lax.dot_general(lo, q, _NT, preferred_element_type=F32))
    own = lax.broadcasted_iota(jnp.int32, (1, S), 1) // (S // n_blk)
    jidx = lax.broadcasted_iota(jnp.int32, gate.shape, 0)
    vals = jnp.where(jidx < own, gate, -jnp.inf)
    chosen = _rank_select(vals, own, n_blk, MOBA_TOPK) | (jidx == own)
    sel_cols = jnp.concatenate([jnp.zeros((HEAD_DIM, S), F32), jnp.where(chosen, 0.0, SEL_NEG),
                                jnp.zeros((LANES - HEAD_DIM - n_blk, S), F32)], axis=0).T
    qa_ref[0, 0] = (q.astype(F32) + sel_cols).astype(qa_ref.dtype)


def _moba_gate(P, batch, seq):
    n_blk = seq // MOBA_BLOCK
    return pl.pallas_call(
        functools.partial(_moba_gate_kernel, n_blk=n_blk),
        grid=(batch, N_HEADS),
        in_specs=[
            pl.BlockSpec((1, 1, seq, LANES), lambda b, h: (b, h, 0, 0)),
            pl.BlockSpec((1, 1, seq, LANES), lambda b, h: (b, N_HEADS + h, 0, 0)),
        ],
        out_specs=pl.BlockSpec((1, 1, seq, LANES), lambda b, h: (b, h, 0, 0)),
        out_shape=jax.ShapeDtypeStruct((batch, N_HEADS, seq, LANES), BF16),
        compiler_params=pltpu.CompilerParams(
            dimension_semantics=("parallel", "parallel"),
            vmem_limit_bytes=_vmem_limit([3 * seq * LANES * 2], 5 * seq * LANES * 4 + 24 * n_blk * seq * 4)),
        name="moba_gate",
    )(P, P)


def _moba_mixer(P, vt, bias, batch, seq):
    q_aug = _moba_gate(P, batch, seq)
    lane = jnp.arange(LANES)
    onehot = (np.arange(seq)[:, None] // MOBA_BLOCK == np.arange(LANES - HEAD_DIM)[None, :]).astype(np.float32)
    onehot = jnp.asarray(np.pad(onehot, ((0, 0), (HEAD_DIM, 0))), BF16)
    k_aug = jnp.where(lane < HEAD_DIM, P[:, N_HEADS:2 * N_HEADS], onehot[None, None])
    return _dense_attn(q_aug, 0, k_aug, 0, vt, 0, bias, n_kv_heads=N_HEADS, group=1, tile=DENSE_TILE,
                       n_kv=4, q_chunk=2 * DENSE_TILE, n_back=None, diag_causal=False,
                       out_dtype=BF16, name="moba_attn")


def _column_scales(segments):
    gain = [jnp.tile(jnp.ones((HEAD_DIM,), F32) if g is None else g, n) for n, g in segments]
    flag = [jnp.full((n * HEAD_DIM,), 0.0 if g is None else 1.0, F32) for n, g in segments]
    return jnp.concatenate(gain).reshape(1, -1), jnp.concatenate(flag).reshape(1, -1)


def _qkv_scales(q_gain, k_gain):
    return _column_scales([(N_HEADS, q_gain * QK_SCALE), (N_HEADS, k_gain), (N_HEADS, None)])


def _dilated_layer(x2, norm_g, w, qg, kg, rel_table, batch, seq):
    gain, flag = _qkv_scales(qg, kg)
    P = _proj(x2, norm_g, w.astype(BF16), gain, flag, None, batch, seq)
    return _dilated_mixer(P, rel_table, batch, seq)


def _nsa_layer(x2, norm_g, w, cmp_pos, cmp_w1, cmp_w2, qg, kg, bias_sel, rel_table, batch, seq):
    KH = NSA_KV_HEADS
    kvd = KH * HEAD_DIM
    seg = [w[:, D_MODEL + i * kvd:D_MODEL + (i + 1) * kvd] for i in range(6)]
    gain, flag = _column_scales([(N_HEADS, qg * QK_SCALE), (KH, None), (KH, None), (KH, kg), (KH, kg)])
    w_main = jnp.concatenate([w[:, :D_MODEL], seg[0], seg[1], seg[2], seg[4]], axis=1).astype(BF16)
    w_aux = jnp.pad(w[:, D_MODEL + 6 * kvd:], ((0, 0), (0, LANES - 3 * N_HEADS))).astype(BF16)
    P, gate_logits = _proj(x2, norm_g, w_main, gain, flag, w_aux, batch, seq)
    vt = _proj_vt(x2, norm_g, jnp.concatenate([seg[3], seg[5]], axis=1).astype(BF16), batch, seq, DENSE_TILE)
    return _nsa_mixer(P, vt, gate_logits, cmp_pos, cmp_w1, cmp_w2, kg, bias_sel, rel_table, batch, seq)


def _qk_scales(q_gain, k_gain):
    return _column_scales([(N_HEADS, q_gain * QK_SCALE), (N_HEADS, k_gain)])


def _fox_layer(x2, norm_g, w, b_f, qg, kg, batch, seq):
    D = D_MODEL
    gain, flag = _qk_scales(qg, kg)
    w_aux = jnp.pad(w[:, 3 * D:], ((0, 0), (0, LANES - N_HEADS))).astype(BF16)
    P, f_logits = _proj(x2, norm_g, w[:, :2 * D].astype(BF16), gain, flag, w_aux, batch, seq)
    vt = _proj_vt(x2, norm_g, w[:, 2 * D:3 * D].astype(BF16), batch, seq, DENSE_TILE)
    return _fox_mixer(P, vt, f_logits, b_f, batch, seq)


def _moba_layer(x2, norm_g, w, qg, kg, bias, batch, seq):
    D = D_MODEL
    gain, flag = _qk_scales(qg, kg)
    P = _proj(x2, norm_g, w[:, :2 * D].astype(BF16), gain, flag, None, batch, seq)
    vt = _proj_vt(x2, norm_g, w[:, 2 * D:].astype(BF16), batch, seq, DENSE_TILE)
    return _moba_mixer(P, vt, bias, batch, seq)


def kernel(x, rel_table, attn_norm, mlp_norm, q_gain, k_gain, w_out, mlp_w_up, mlp_w_down, dsa_w_in, nsa_w_in,
           nsa_cmp_pos, nsa_cmp_w1, nsa_cmp_w2, fox_w_in, fox_b_f, moba_w_in):
    batch, seq, _ = x.shape
    depth = attn_norm.shape[0]
    x2 = x.reshape(batch * seq, D_MODEL)
    bias_dense = _tile_bias(rel_table, DENSE_TILE, 10, 1, None)
    for layer in range(depth):
        kind, r = layer % 4, layer // 4
        qg, kg, ng = q_gain[layer], k_gain[layer], attn_norm[layer]
        if kind == 0:
            mixed = _dilated_layer(x2, ng, dsa_w_in[r], qg, kg, rel_table, batch, seq)
        elif kind == 1:
            mixed = _nsa_layer(x2, ng, nsa_w_in[r], nsa_cmp_pos[r], nsa_cmp_w1[r], nsa_cmp_w2[r], qg, kg,
                               bias_dense, rel_table, batch, seq)
        elif kind == 2:
            mixed = _fox_layer(x2, ng, fox_w_in[r], fox_b_f[r], qg, kg, batch, seq)
        else:
            mixed = _moba_layer(x2, ng, moba_w_in[r], qg, kg, bias_dense, batch, seq)
        x2 = _out_mlp(x2, mixed.reshape(batch * seq, D_MODEL), w_out[layer].astype(BF16), mlp_norm[layer],
                      mlp_w_up[layer].astype(BF16), mlp_w_down[layer].astype(BF16))
    return x2.reshape(batch, seq, D_MODEL)
```

```python
import functools
import math

import jax
import jax.numpy as jnp
import numpy as np
from jax import lax
from jax.experimental import pallas as pl
from jax.experimental.pallas import tpu as pltpu

F32 = jnp.float32
BF16 = jnp.bfloat16

D_MODEL = 1024
N_HEADS = 16
HEAD_DIM = 64
D_FF = 4 * D_MODEL
RMS_EPS = 1e-6
MASK_NEG = -1e30
TINY = 1e-30
LOG2E = math.log2(math.e)
QK_SCALE = HEAD_DIM ** -0.5 * LOG2E

REL_BUCKETS = 32
REL_MAX_EXACT = REL_BUCKETS // 2
REL_MAX_DIST = 2048

DILATED_PATTERNS = ((128, 1), (512, 4), (2048, 16))
BAND_BLOCK = 128

NSA_KV_HEADS = 4
NSA_GROUP = N_HEADS // NSA_KV_HEADS
CMP_STRIDE = 16
CMP_BLOCK = 2 * CMP_STRIDE
CMP_HIDDEN = 256
SEL_BLOCK = 64
SEL_TOPK = 16
NSA_WINDOW = 512

MOBA_BLOCK = 256
MOBA_TOPK = 3

LANES = 128
VMEM_BYTES_V7X = 64 * 1024 * 1024
VMEM_HEADROOM = 6 * 1024 * 1024

DENSE_TILE = 256
SEL_NEG = -(2.0 ** 100)

_NT = (((1,), (1,)), ((), ()))


def _vmem_limit(block_bytes, temp_bytes):
    need = 2 * int(sum(block_bytes)) + int(temp_bytes) + (2 << 20)
    return int(min(need, VMEM_BYTES_V7X - VMEM_HEADROOM))


def _split3(v):
    a = v.astype(BF16)
    r = v - a.astype(F32)
    b = r.astype(BF16)
    c = (r - b.astype(F32)).astype(BF16)
    return a, b, c


def _proj_kernel(*refs, has_aux):
    if has_aux:
        x_ref, g_ref, w_ref, cg_ref, cf_ref, bd_ref, wa_ref, o_ref, oa_ref, h_scr = refs
    else:
        x_ref, g_ref, w_ref, cg_ref, cf_ref, bd_ref, o_ref, h_scr = refs

    @pl.when(pl.program_id(1) == 0)
    def _():
        x = x_ref[...]
        ms = jnp.mean(x * x, axis=-1, keepdims=True)
        h_scr[...] = (x * lax.rsqrt(ms + RMS_EPS) * g_ref[...]).astype(BF16)
        if has_aux:
            oa_ref[...] = jnp.dot(h_scr[...], wa_ref[...], preferred_element_type=F32)

    y_all = jnp.dot(h_scr[...], w_ref[...], preferred_element_type=F32)
    nb = bd_ref.shape[0]
    lane = lax.broadcasted_iota(jnp.int32, (y_all.shape[0], LANES), 1)
    for c in range(y_all.shape[1] // nb):
        y = y_all[:, c * nb:(c + 1) * nb]
        ms = jnp.dot((y * y).astype(BF16), bd_ref[...], preferred_element_type=F32)
        rs = lax.rsqrt(ms + RMS_EPS)
        out = y * jnp.where(cf_ref[:, c * nb:(c + 1) * nb] > 0.5, rs, 1.0) * cg_ref[:, c * nb:(c + 1) * nb]
        for pair in range(nb // LANES):
            both = out[:, pair * LANES:(pair + 1) * LANES]
            slot = (c * nb) // HEAD_DIM + 2 * pair
            o_ref[0, slot] = jnp.where(lane < HEAD_DIM, both, 0.0).astype(o_ref.dtype)
            o_ref[0, slot + 1] = jnp.where(lane < HEAD_DIM, pltpu.roll(both, HEAD_DIM, 1), 0.0).astype(o_ref.dtype)


def _proj_vt_kernel(x_ref, g_ref, w_ref, o_ref, h_scr):
    @pl.when(pl.program_id(1) == 0)
    def _():
        x = x_ref[...]
        ms = jnp.mean(x * x, axis=-1, keepdims=True)
        h_scr[...] = (x * lax.rsqrt(ms + RMS_EPS) * g_ref[...]).astype(BF16)

    y = jnp.dot(h_scr[...], w_ref[...], preferred_element_type=F32)
    tile = o_ref.shape[-1]
    lane = lax.broadcasted_iota(jnp.int32, (y.shape[0], LANES), 1)
    for pair in range(y.shape[1] // LANES):
        both = y[:, pair * LANES:(pair + 1) * LANES]
        for half, head in enumerate((both, pltpu.roll(both, HEAD_DIM, 1))):
            head_t = jnp.where(lane < HEAD_DIM, head, 0.0).T
            for kt in range(y.shape[0] // tile):
                o_ref[0, 2 * pair + half, kt] = head_t[:, kt * tile:(kt + 1) * tile].astype(o_ref.dtype)


def _proj_vt(x2, norm_g, w, batch, seq, tile):
    T = x2.shape[0]
    N = w.shape[1]
    tm, tn = 1024, 2 * LANES
    spb = seq // tm
    hpt = tn // HEAD_DIM
    blocks = [tm * D_MODEL * 4, D_MODEL * tn * 2, hpt * tm * LANES * 2]
    return pl.pallas_call(
        _proj_vt_kernel,
        grid=(T // tm, N // tn),
        in_specs=[
            pl.BlockSpec((tm, D_MODEL), lambda i, j: (i, 0)),
            pl.BlockSpec((1, D_MODEL), lambda i, j: (0, 0)),
            pl.BlockSpec((D_MODEL, tn), lambda i, j: (0, j)),
        ],
        out_specs=pl.BlockSpec((1, hpt, tm // tile, LANES, tile), lambda i, j: (i // spb, j, i % spb, 0, 0)),
        out_shape=jax.ShapeDtypeStruct((batch, N // HEAD_DIM, seq // tile, LANES, tile), BF16),
        scratch_shapes=[pltpu.VMEM((tm, D_MODEL), BF16)],
        compiler_params=pltpu.CompilerParams(
            dimension_semantics=("parallel", "arbitrary"),
            vmem_limit_bytes=_vmem_limit(blocks, tm * D_MODEL * 2 + 8 * tm * tn * 4)),
        name="proj_vt",
    )(x2, norm_g.reshape(1, D_MODEL), w)


def _proj(x2, norm_g, w, col_gain, col_flag, w_aux, batch, seq):
    T = x2.shape[0]
    N = w.shape[1]
    tm, nb = 1024, 2 * LANES
    tn = 2 * nb if N % (2 * nb) == 0 else nb
    spb = seq // tm
    bd = np.zeros((nb, nb), np.float32)
    for a in range(nb // HEAD_DIM):
        bd[a * HEAD_DIM:(a + 1) * HEAD_DIM, a * HEAD_DIM:(a + 1) * HEAD_DIM] = 1.0 / HEAD_DIM
    bd = jnp.asarray(bd, BF16)
    has_aux = w_aux is not None
    in_specs = [
        pl.BlockSpec((tm, D_MODEL), lambda i, j: (i, 0)),
        pl.BlockSpec((1, D_MODEL), lambda i, j: (0, 0)),
        pl.BlockSpec((D_MODEL, tn), lambda i, j: (0, j)),
        pl.BlockSpec((1, tn), lambda i, j: (0, j)),
        pl.BlockSpec((1, tn), lambda i, j: (0, j)),
        pl.BlockSpec((nb, nb), lambda i, j: (0, 0)),
    ]
    args = [x2, norm_g.reshape(1, D_MODEL), w, col_gain, col_flag, bd]
    hpt = tn // HEAD_DIM
    out_shape = [jax.ShapeDtypeStruct((batch, N // HEAD_DIM, seq, LANES), BF16)]
    out_specs = [pl.BlockSpec((1, hpt, tm, LANES), lambda i, j: (i // spb, j, i % spb, 0))]
    if has_aux:
        in_specs.append(pl.BlockSpec((D_MODEL, LANES), lambda i, j: (0, 0)))
        args.append(w_aux)
        out_shape.append(jax.ShapeDtypeStruct((T, LANES), F32))
        out_specs.append(pl.BlockSpec((tm, LANES), lambda i, j: (i, 0)))
    blocks = [tm * D_MODEL * 4, D_MODEL * tn * 2, hpt * tm * LANES * 2, nb * nb * 2, tm * LANES * 4,
              D_MODEL * LANES * 2]
    res = pl.pallas_call(
        functools.partial(_proj_kernel, has_aux=has_aux),
        grid=(T // tm, N // tn),
        in_specs=in_specs,
        out_specs=out_specs,
        out_shape=out_shape,
        scratch_shapes=[pltpu.VMEM((tm, D_MODEL), BF16)],
        compiler_params=pltpu.CompilerParams(
            dimension_semantics=("parallel", "arbitrary"),
            vmem_limit_bytes=_vmem_limit(blocks, tm * D_MODEL * 2 + 6 * tm * tn * 4)),
        name="proj",
    )(*args)
    return res if has_aux else res[0]


def _out_mlp_kernel(x_ref, m_ref, wo_ref, g_ref, wu_ref, wd_ref, o_ref, h_scr):
    @pl.when(pl.program_id(1) == 0)
    def _():
        x1 = x_ref[...] + jnp.dot(m_ref[...], wo_ref[...], preferred_element_type=F32)
        o_ref[...] = x1
        ms = jnp.mean(x1 * x1, axis=-1, keepdims=True)
        h_scr[...] = (x1 * lax.rsqrt(ms + RMS_EPS) * g_ref[...]).astype(BF16)

    u = jnp.dot(h_scr[...], wu_ref[...], preferred_element_type=F32)
    a = jnp.square(jnp.maximum(u, 0.0)).astype(BF16)
    o_ref[...] += jnp.dot(a, wd_ref[...], preferred_element_type=F32)


def _out_mlp(x2, mixed, w_out, norm_g, w_up, w_down):
    T = x2.shape[0]
    tm, tf = 1024, 1024
    blocks = [tm * D_MODEL * 4, tm * D_MODEL * 2, D_MODEL * D_MODEL * 2,
              D_MODEL * tf * 2, tf * D_MODEL * 2, tm * D_MODEL * 4]
    return pl.pallas_call(
        _out_mlp_kernel,
        grid=(T // tm, D_FF // tf),
        in_specs=[
            pl.BlockSpec((tm, D_MODEL), lambda i, f: (i, 0)),
            pl.BlockSpec((tm, D_MODEL), lambda i, f: (i, 0)),
            pl.BlockSpec((D_MODEL, D_MODEL), lambda i, f: (0, 0)),
            pl.BlockSpec((1, D_MODEL), lambda i, f: (0, 0)),
            pl.BlockSpec((D_MODEL, tf), lambda i, f: (0, f)),
            pl.BlockSpec((tf, D_MODEL), lambda i, f: (f, 0)),
        ],
        out_specs=pl.BlockSpec((tm, D_MODEL), lambda i, f: (i, 0)),
        out_shape=jax.ShapeDtypeStruct((T, D_MODEL), F32),
        scratch_shapes=[pltpu.VMEM((tm, D_MODEL), BF16)],
        compiler_params=pltpu.CompilerParams(
            dimension_semantics=("parallel", "arbitrary"),
            vmem_limit_bytes=_vmem_limit(blocks, tm * D_MODEL * 2 + 3 * tm * tf * 4 + 2 * tm * D_MODEL * 4)),
        name="out_mlp",
    )(x2, mixed, w_out, norm_g.reshape(1, D_MODEL), w_up, w_down)


def _dense_attn_kernel(*refs, tile, n_kv, group, n_qt, n_back, has_bias, n_bias, diag_causal):
    if has_bias:
        q_ref, k_ref, vt_ref, b_ref, o_ref, acc_scr, s_scr, p_scr, st_scr = refs
    else:
        q_ref, k_ref, vt_ref, o_ref, acc_scr, s_scr, p_scr, st_scr = refs
        b_ref = None
    chunk = pl.program_id(2)
    n_streams = n_kv * group
    if diag_causal:
        visible = (lax.broadcasted_iota(jnp.int32, (tile, tile), 0)
                   <= lax.broadcasted_iota(jnp.int32, (tile, tile), 1))

    def one_query_tile(t, _):
        qi = chunk * n_qt + t
        row0 = pl.multiple_of(t * tile, tile)
        qs = [q_ref[0, h, pl.ds(row0, tile), :].astype(F32).T.astype(BF16) for h in range(n_streams)]

        def issue_scores(d, slot):
            start = pl.multiple_of(jnp.maximum(qi - d, 0) * tile, tile)
            for h in range(n_streams):
                s = jnp.dot(k_ref[0, h // group, pl.ds(start, tile), :], qs[h],
                            preferred_element_type=F32)
                if has_bias:
                    s = s + b_ref[h, jnp.minimum(d, n_bias - 1)]
                s_scr[slot, h] = s

        def pv_matmuls(d, slot):
            return [jnp.dot(vt_ref[0, h // group, qi - d], p_scr[slot, h], preferred_element_type=F32)
                    for h in range(n_streams)]

        def accumulate(slot, pvs):
            for h in range(n_streams):
                acc_scr[h] = st_scr[2 + slot, h] * acc_scr[h] + pvs[h]

        def softmax(slot, first):
            for h in range(n_streams):
                s = s_scr[slot, h]
                if first:
                    if diag_causal:
                        s = jnp.where(visible, s, MASK_NEG)
                    m = jnp.max(s, axis=0, keepdims=True)
                    p = jnp.exp2(s - m)
                    st_scr[0, h] = m
                    st_scr[1, h] = jnp.sum(p, axis=0, keepdims=True)
                    st_scr[3 - slot, h] = jnp.zeros_like(m)
                else:
                    m0 = st_scr[0, h]
                    m = jnp.maximum(m0, jnp.max(s, axis=0, keepdims=True))
                    alpha = jnp.exp2(m0 - m)
                    p = jnp.exp2(s - m)
                    st_scr[0, h] = m
                    st_scr[1, h] = alpha * st_scr[1, h] + jnp.sum(p, axis=0, keepdims=True)
                    st_scr[3 - slot, h] = alpha
                p_scr[1 - slot, h] = p.astype(BF16)

        def tile_step(d, slot):
            pvs = pv_matmuls(d - 1, slot)
            issue_scores(d + 1, 1 - slot)
            softmax(slot, False)
            accumulate(slot, pvs)

        acc_scr[...] = jnp.zeros_like(acc_scr)
        issue_scores(0, 0)
        issue_scores(1, 1)
        softmax(0, True)
        n_steps = qi + 1 if n_back is None else jnp.minimum(qi, n_back) + 1

        def two_steps(j, _):
            tile_step(1 + 2 * j, 1)
            tile_step(2 + 2 * j, 0)
            return 0

        lax.fori_loop(0, (n_steps - 1) >> 1, two_steps, 0)
        odd_tail = ((n_steps - 1) & 1) == 1

        @pl.when(odd_tail)
        def _():
            tile_step(n_steps - 1, 1)
            accumulate(0, pv_matmuls(n_steps - 1, 0))

        @pl.when(jnp.logical_not(odd_tail))
        def _():
            accumulate(1, pv_matmuls(n_steps - 1, 1))

        outs = [acc_scr[h][:HEAD_DIM] / jnp.maximum(st_scr[1, h], TINY) for h in range(n_streams)]
        for pair in range(n_streams // 2):
            both = jnp.concatenate([outs[2 * pair], outs[2 * pair + 1]], axis=0)
            o_ref[0, pl.ds(row0, tile), pair * LANES:(pair + 1) * LANES] = both.T.astype(o_ref.dtype)
        return 0

    lax.fori_loop(0, n_qt, one_query_tile, 0)


def _dense_attn(q_arr, q_off, k_arr, k_off, vt_arr, v_off, bias, *, n_kv_heads, group, tile, n_kv, q_chunk,
                n_back, diag_causal, out_dtype, name):
    B, _, S, _ = q_arr.shape
    n_q_heads = n_kv_heads * group
    nq = n_kv * group
    assert q_off % nq == 0 and k_off % n_kv == 0 and v_off % n_kv == 0
    assert S % q_chunk == 0 and q_chunk % tile == 0 and n_kv_heads % n_kv == 0
    n_qt = q_chunk // tile
    has_bias = bias is not None
    n_bias = bias.shape[1] if has_bias else 0
    in_specs = [
        pl.BlockSpec((1, nq, q_chunk, LANES), lambda u, b, c: (b, q_off // nq + u, c, 0)),
        pl.BlockSpec((1, n_kv, S, LANES), lambda u, b, c: (b, k_off // n_kv + u, 0, 0)),
        pl.BlockSpec((1, n_kv, S // tile, LANES, tile), lambda u, b, c: (b, v_off // n_kv + u, 0, 0, 0)),
    ]
    args = [q_arr, k_arr, vt_arr]
    out_bytes = jnp.dtype(out_dtype).itemsize
    blocks = [nq * q_chunk * LANES * 2, 2 * n_kv * S * LANES * 2, nq * q_chunk * LANES * out_bytes]
    if has_bias:
        in_specs.append(pl.BlockSpec((nq, n_bias, tile, tile), lambda u, b, c: (u, 0, 0, 0)))
        args.append(bias)
        blocks.append(nq * n_bias * tile * tile * 4)
    kern = functools.partial(_dense_attn_kernel, tile=tile, n_kv=n_kv, group=group, n_qt=n_qt, n_back=n_back,
                             has_bias=has_bias, n_bias=n_bias, diag_causal=diag_causal)
    assert nq % 2 == 0
    return pl.pallas_call(
        kern,
        grid=(n_kv_heads // n_kv, B, S // q_chunk),
        in_specs=in_specs,
        out_specs=pl.BlockSpec((1, q_chunk, nq * HEAD_DIM), lambda u, b, c: (b, c, u)),
        out_shape=jax.ShapeDtypeStruct((B, S, n_q_heads * HEAD_DIM), out_dtype),
        scratch_shapes=[pltpu.VMEM((nq, LANES, tile), F32),
                        pltpu.VMEM((2, nq, tile, tile), F32),
                        pltpu.VMEM((2, nq, tile, tile), BF16),
                        pltpu.VMEM((4, nq, 1, tile), F32)],
        compiler_params=pltpu.CompilerParams(
            dimension_semantics=("parallel", "parallel", "arbitrary"),
            vmem_limit_bytes=_vmem_limit(blocks, nq * (4 * tile * tile + 3 * LANES * tile) * 4)),
        name=name,
    )(*args)


def _bucket_np(dist):
    d = np.maximum(dist, 0)
    df = np.maximum(d.astype(np.float32), np.float32(1.0))
    large = REL_MAX_EXACT + (np.log(df / np.float32(REL_MAX_EXACT))
                             / np.float32(math.log(REL_MAX_DIST / REL_MAX_EXACT))
                             * np.float32(REL_BUCKETS - REL_MAX_EXACT)).astype(np.int32)
    large = np.minimum(large, REL_BUCKETS - 1)
    return np.where(d < REL_MAX_EXACT, d, large).astype(np.int32)


def _dist_values(rel_table, dist, ok):
    vals = rel_table[jnp.asarray(_bucket_np(dist))] * LOG2E
    return jnp.where(jnp.asarray(ok)[:, None], vals, MASK_NEG).T


def _skew_kernel(w_ref, o_ref, *, step, col0, mask_last_row):
    _, n_out, n_rows, width = o_ref.shape
    x = jnp.broadcast_to(w_ref[0], (n_rows, w_ref.shape[2]))
    x = pltpu.roll(x, 0, 1, stride=step, stride_axis=0)
    if mask_last_row:
        row = lax.broadcasted_iota(jnp.int32, x.shape, 0)
        x = jnp.where(row == n_rows - 1, MASK_NEG, x)
    for a in range(n_out):
        o_ref[0, a] = x[:, col0 + a * width:col0 + (a + 1) * width]


def _skew(w, n_rows, step, col0, n_out, width, mask_last_row=False):
    H, W = w.shape
    assert W % LANES == 0 and col0 % LANES == 0 and width % LANES == 0 and col0 + n_out * width <= W
    assert col0 - step * (n_rows - 1) >= 0
    return pl.pallas_call(
        functools.partial(_skew_kernel, step=step, col0=col0, mask_last_row=mask_last_row),
        grid=(H,),
        in_specs=[pl.BlockSpec((1, 1, W), lambda h: (h, 0, 0))],
        out_specs=pl.BlockSpec((1, n_out, n_rows, width), lambda h: (h, 0, 0, 0)),
        out_shape=jax.ShapeDtypeStruct((H, n_out, n_rows, width), F32),
        compiler_params=pltpu.CompilerParams(
            dimension_semantics=("parallel",),
            vmem_limit_bytes=_vmem_limit([W * 4, n_out * n_rows * width * 4], 3 * n_rows * W * 4)),
        name="bias_skew",
    )(w.reshape(H, 1, W))


def _tile_bias(rel_table, tile, n_tiles, dist_scale, max_dist):
    d = np.arange((n_tiles + 1) * tile) - tile
    ok = d >= 0
    if max_dist is not None:
        ok &= d <= max_dist
    w = _dist_values(rel_table, d * dist_scale, ok)
    return _skew(w, tile, 1, tile, n_tiles, tile)


def _merge_patterns(parts):
    ms = [p[HEAD_DIM:HEAD_DIM + 1, :] for p in parts]
    ls = [p[HEAD_DIM + 1:HEAD_DIM + 2, :] for p in parts]
    m_all = functools.reduce(jnp.maximum, ms)
    ws = [jnp.exp2(m - m_all) for m in ms]
    num = sum(w * p for w, p in zip(ws, parts))
    den = sum(w * l for w, l in zip(ws, ls))
    return num / jnp.maximum(den, TINY)


def _band_attn_kernel(*refs, tile, n_heads, n_qt, merge):
    if merge:
        q_ref, k_ref, v_ref, b_ref, other1_ref, other2_ref, o_ref = refs
    else:
        q_ref, k_ref, v_ref, b_ref, o_ref = refs
    chunk = pl.program_id(2)
    channel = lax.broadcasted_iota(jnp.int32, (LANES, tile), 0)

    def one_query_tile(t, _):
        qi = chunk * n_qt + t
        row0 = pl.multiple_of(t * tile, tile)
        k0 = jnp.maximum(qi - 1, 0)
        start = pl.multiple_of(k0 * tile, tile)
        case = jnp.where(qi == 0, 1, 0)
        scores = []
        for h in range(n_heads):
            q_t = q_ref[0, h, pl.ds(row0, tile), :].astype(F32).T.astype(BF16)
            scores.append(jnp.dot(k_ref[0, h, pl.ds(start, 2 * tile), :], q_t, preferred_element_type=F32)
                          + b_ref[h, case])
        stats = []
        for h in range(n_heads):
            m = jnp.max(scores[h], axis=0, keepdims=True)
            p = jnp.exp2(scores[h] - m)
            stats.append((m, jnp.sum(p, axis=0, keepdims=True), p.astype(BF16)))
        own = []
        for h in range(n_heads):
            m, l, p = stats[h]
            acc = lax.dot_general(v_ref[0, h, pl.ds(start, 2 * tile), :], p, (((0,), (0,)), ((), ())),
                                  preferred_element_type=F32)
            out_t = jnp.where(channel == HEAD_DIM, m, jnp.where(channel == HEAD_DIM + 1, l, acc))
            if merge:
                own.append(out_t)
            else:
                o_ref[0, h, pl.ds(row0, tile), :] = out_t.T
        if merge:
            rows = pl.ds(row0, tile)
            mixed = [_merge_patterns([own[h], other1_ref[0, h, rows, :].T, other2_ref[0, h, rows, :].T])
                     for h in range(n_heads)]
            for pair in range(n_heads // 2):
                both = jnp.concatenate([mixed[2 * pair][:HEAD_DIM], mixed[2 * pair + 1][:HEAD_DIM]], axis=0)
                o_ref[0, rows, pair * LANES:(pair + 1) * LANES] = both.T.astype(o_ref.dtype)
        return 0

    lax.fori_loop(0, n_qt, one_query_tile, 0)


def _band_attn(P, bias2, n_heads, q_chunk, name, others=None):
    B, _, n, _ = P.shape
    tile = bias2.shape[-1]
    assert n >= 2 * tile and n % q_chunk == 0 and q_chunk % tile == 0 and N_HEADS % n_heads == 0
    merge = others is not None
    kern = functools.partial(_band_attn_kernel, tile=tile, n_heads=n_heads, n_qt=q_chunk // tile, merge=merge)
    groups = N_HEADS // n_heads
    head_rows = pl.BlockSpec((1, n_heads, q_chunk, LANES), lambda u, b, c: (b, u, c, 0))
    in_specs = [
        head_rows,
        pl.BlockSpec((1, n_heads, n, LANES), lambda u, b, c: (b, groups + u, 0, 0)),
        pl.BlockSpec((1, n_heads, n, LANES), lambda u, b, c: (b, 2 * groups + u, 0, 0)),
        pl.BlockSpec((n_heads, 2, 2 * tile, tile), lambda u, b, c: (u, 0, 0, 0)),
    ]
    args = [P, P, P, bias2]
    blocks = [n_heads * q_chunk * LANES * 2, 2 * n_heads * n * LANES * 2, n_heads * 2 * 2 * tile * tile * 4,
              n_heads * q_chunk * LANES * 4]
    if merge:
        assert n_heads % 2 == 0
        in_specs += [head_rows, head_rows]
        args += list(others)
        blocks.append(2 * n_heads * q_chunk * LANES * 4)
        out_specs = pl.BlockSpec((1, q_chunk, n_heads * HEAD_DIM), lambda u, b, c: (b, c, u))
        out_shape = jax.ShapeDtypeStruct((B, n, N_HEADS * HEAD_DIM), BF16)
    else:
        out_specs = head_rows
        out_shape = jax.ShapeDtypeStruct((B, N_HEADS, n, LANES), F32)
    return pl.pallas_call(
        kern,
        grid=(groups, B, n // q_chunk),
        in_specs=in_specs,
        out_specs=out_specs,
        out_shape=out_shape,
        compiler_params=pltpu.CompilerParams(
            dimension_semantics=("parallel", "parallel", "arbitrary"),
            vmem_limit_bytes=_vmem_limit(blocks, n_heads * 8 * 2 * tile * tile * 4)),
        name=name,
    )(*args)


def _dilated_mixer(P, rel_table, batch, seq):
    def band_bias(window, dil):
        bias = _tile_bias(rel_table, BAND_BLOCK, 2, dil, window // dil)
        general = jnp.concatenate([bias[:, 1], bias[:, 0]], axis=1)
        first = jnp.concatenate([bias[:, 0], jnp.full_like(bias[:, 0], MASK_NEG)], axis=1)
        return jnp.stack([general, first], axis=1)

    strided = []
    for window, dil in DILATED_PATTERNS:
        if dil == 1:
            continue
        n = seq // dil
        Pr = P.reshape(batch, 3 * N_HEADS, n, dil, LANES).transpose(0, 3, 1, 2, 4)
        Pr = Pr.reshape(batch * dil, 3 * N_HEADS, n, LANES)
        o = _band_attn(Pr, band_bias(window, dil), N_HEADS, min(n, 512), f"dilated_attn_d{dil}")
        strided.append(o.reshape(batch, dil, N_HEADS, n, LANES).transpose(0, 2, 3, 1, 4)
                       .reshape(batch, N_HEADS, seq, LANES))
    (window, dil), = [pat for pat in DILATED_PATTERNS if pat[1] == 1]
    return _band_attn(P, band_bias(window, dil), N_HEADS // 2, 2 * BAND_BLOCK, "dilated_attn_merge", others=strided)


def _compress_kernel(x_ref, w1_ref, w2_ref, pos_ref, g_ref, o_ref):
    posw = jnp.dot(pos_ref[0], w1_ref[0], preferred_element_type=F32)[0:1]
    pre = jnp.dot(x_ref[0, 0], w1_ref[0], preferred_element_type=F32) + posw
    hid = 0.5 * pre * (1.0 + jnp.tanh(math.sqrt(2.0 / math.pi) * (pre + 0.044715 * (pre * pre * pre))))
    out = jnp.dot(hid.astype(BF16), w2_ref[0], preferred_element_type=F32)
    ms = jnp.sum(out * out, axis=-1, keepdims=True) * (1.0 / HEAD_DIM)
    normed = out * lax.rsqrt(ms + RMS_EPS) * g_ref[...]
    o_ref[0, 0] = jnp.where(pl.program_id(0) == 0, normed, out).astype(o_ref.dtype)


def _compress(blocks, w1, w2, pos, k_gain_pad):
    _, BK, n, K = blocks.shape
    return pl.pallas_call(
        _compress_kernel,
        grid=(2, BK),
        in_specs=[
            pl.BlockSpec((1, 1, n, K), lambda a, i: (a, i, 0, 0)),
            pl.BlockSpec((1, K, CMP_HIDDEN), lambda a, i: (a, 0, 0)),
            pl.BlockSpec((1, CMP_HIDDEN, LANES), lambda a, i: (a, 0, 0)),
            pl.BlockSpec((1, 8, K), lambda a, i: (a, 0, 0)),
            pl.BlockSpec((1, LANES), lambda a, i: (0, 0)),
        ],
        out_specs=pl.BlockSpec((1, 1, n, LANES), lambda a, i: (a, i, 0, 0)),
        out_shape=jax.ShapeDtypeStruct((2, BK, n, LANES), BF16),
        compiler_params=pltpu.CompilerParams(
            dimension_semantics=("parallel", "parallel"),
            vmem_limit_bytes=_vmem_limit([n * K * 2, K * CMP_HIDDEN * 2, CMP_HIDDEN * LANES * 2, 8 * K * 2,
                                          n * LANES * 2], 8 * n * CMP_HIDDEN * 4)),
        name="nsa_compress",
    )(blocks, w1, w2, pos, k_gain_pad)


def _rank_select(vals, own_row, n_blocks, top):
    jidx = lax.broadcasted_iota(jnp.int32, vals.shape, 0)
    cnt = jnp.zeros(vals.shape, F32)
    for j in range(n_blocks):
        row = vals[j:j + 1, :]
        ahead = (row > vals) | ((row == vals) & (jidx > j))
        cnt = cnt + jnp.where(ahead, 1.0, 0.0)
    return (jidx < own_row) & (cnt < top)


def _cmp_attn_kernel(q_ref, k_ref, vt_ref, b_ref, ov_ref, o_ref, qa_ref, *, tq):
    qi = pl.program_id(1)
    k = k_ref[0, 0]
    vt = vt_ref[0]
    scores = [lax.dot_general(k, q_ref[0, g], _NT, preferred_element_type=F32) + b_ref[g]
              for g in range(NSA_GROUP)]
    psum = None
    all_probs = []
    for s in scores:
        m = jnp.max(s, axis=0, keepdims=True)
        p = jnp.where(m > 0.5 * MASK_NEG, jnp.exp2(s - m), 0.0)
        l = jnp.sum(p, axis=0, keepdims=True)
        probs = p / jnp.maximum(l, TINY)
        all_probs.append(probs.astype(BF16))
        psum = probs if psum is None else psum + probs
    outs = [jnp.dot(vt, p, preferred_element_type=F32)[:HEAD_DIM] for p in all_probs]
    for pair in range(NSA_GROUP // 2):
        both = jnp.concatenate([outs[2 * pair], outs[2 * pair + 1]], axis=0)
        o_ref[0, :, pair * LANES:(pair + 1) * LANES] = both.T.astype(o_ref.dtype)
    imp = None
    for part in _split3(psum):
        term = jnp.dot(ov_ref[...], part, preferred_element_type=F32)
        imp = term if imp is None else imp + term
    n_sel = imp.shape[0]
    pos = qi * tq + lax.broadcasted_iota(jnp.int32, (1, tq), 1)
    own = pos // SEL_BLOCK
    jidx = lax.broadcasted_iota(jnp.int32, imp.shape, 0)
    vals = jnp.where(jidx < own, imp, -1.0)
    chosen = _rank_select(vals, own, n_sel, SEL_TOPK - 1) | (jidx == own)
    sel_cols = jnp.concatenate([jnp.zeros((HEAD_DIM, tq), F32), jnp.where(chosen, 0.0, SEL_NEG)], axis=0).T
    for g in range(NSA_GROUP):
        qa_ref[0, g] = (q_ref[0, g].astype(F32) + sel_cols).astype(qa_ref.dtype)


def _cmp_attn(P, kv_cmp, vt_cmp, bias_c, overlap_t, batch, seq):
    assert seq // SEL_BLOCK == LANES - HEAD_DIM
    tq = 256
    n_cmp = kv_cmp.shape[2]
    n_sel = overlap_t.shape[0]
    kern = functools.partial(_cmp_attn_kernel, tq=tq)
    return pl.pallas_call(
        kern,
        grid=(NSA_KV_HEADS, seq // tq, batch),
        in_specs=[
            pl.BlockSpec((1, NSA_GROUP, tq, LANES), lambda kh, qi, b: (b, kh, qi, 0)),
            pl.BlockSpec((1, 1, n_cmp, LANES), lambda kh, qi, b: (0, b * NSA_KV_HEADS + kh, 0, 0)),
            pl.BlockSpec((1, LANES, n_cmp), lambda kh, qi, b: (b * NSA_KV_HEADS + kh, 0, 0)),
            pl.BlockSpec((NSA_GROUP, n_cmp, tq), lambda kh, qi, b: (kh, 0, qi)),
            pl.BlockSpec((n_sel, n_cmp), lambda kh, qi, b: (0, 0)),
        ],
        out_specs=[
            pl.BlockSpec((1, tq, NSA_GROUP * HEAD_DIM), lambda kh, qi, b: (b, qi, kh)),
            pl.BlockSpec((1, NSA_GROUP, tq, LANES), lambda kh, qi, b: (b, kh, qi, 0)),
        ],
        out_shape=[
            jax.ShapeDtypeStruct((batch, seq, N_HEADS * HEAD_DIM), F32),
            jax.ShapeDtypeStruct((batch, N_HEADS, seq, LANES), BF16),
        ],
        compiler_params=pltpu.CompilerParams(
            dimension_semantics=("parallel", "parallel", "arbitrary"),
            vmem_limit_bytes=_vmem_limit([NSA_GROUP * tq * LANES * 2, 2 * n_cmp * LANES * 2,
                                          NSA_GROUP * tq * n_cmp * 4, NSA_GROUP * tq * LANES * 4, n_sel * tq * 2],
                                         16 * tq * n_cmp * 4)),
        name="nsa_cmp_attn",
    )(P, kv_cmp, vt_cmp, bias_c, overlap_t)


def _nsa_merge_kernel(c_ref, s_ref, w_ref, g_ref, o_ref):
    gates = jax.nn.sigmoid(g_ref[0])
    lane = lax.broadcasted_iota(jnp.int32, gates.shape, 1)
    for pair in range(N_HEADS // 2):
        cols = slice(pair * LANES, (pair + 1) * LANES)

        def gate(branch):
            a, b = 3 * (2 * pair) + branch, 3 * (2 * pair + 1) + branch
            return jnp.where(lane < HEAD_DIM, gates[:, a:a + 1], gates[:, b:b + 1])

        o = gate(0) * c_ref[0, :, cols] + gate(1) * s_ref[0, :, cols] + gate(2) * w_ref[0, :, cols]
        o_ref[0, :, cols] = o.astype(o_ref.dtype)


def _nsa_merge(o_cmp, o_sel, o_win, gate_logits, batch, seq):
    ts = 256
    spec = pl.BlockSpec((1, ts, D_MODEL), lambda b, s: (b, s, 0))
    return pl.pallas_call(
        _nsa_merge_kernel,
        grid=(batch, seq // ts),
        in_specs=[spec, spec, spec, pl.BlockSpec((1, ts, LANES), lambda b, s: (b, s, 0))],
        out_specs=spec,
        out_shape=jax.ShapeDtypeStruct((batch, seq, D_MODEL), BF16),
        compiler_params=pltpu.CompilerParams(
            dimension_semantics=("parallel", "parallel"),
            vmem_limit_bytes=_vmem_limit([3 * ts * D_MODEL * 4, ts * LANES * 4, ts * D_MODEL * 2],
                                         8 * ts * LANES * 4)),
        name="nsa_merge",
    )(o_cmp, o_sel, o_win, gate_logits.reshape(batch, seq, LANES))


def _nsa_mixer(P, vt, gate_logits, cmp_pos, cmp_w1, cmp_w2, k_gain, bias_sel, rel_table, batch, seq):
    KH = NSA_KV_HEADS
    n_chunk = seq // CMP_STRIDE
    n_sel = seq // SEL_BLOCK

    raw = P[:, N_HEADS:N_HEADS + 2 * KH, :, :HEAD_DIM].reshape(batch, 2, KH, n_chunk, CMP_STRIDE * HEAD_DIM)
    raw = raw.transpose(1, 0, 2, 3, 4).reshape(2, batch * KH, n_chunk, CMP_STRIDE * HEAD_DIM)
    blocks = jnp.concatenate([raw, jnp.roll(raw, -1, axis=2)], axis=-1)
    pos = jnp.broadcast_to(cmp_pos.reshape(2, 1, CMP_BLOCK * HEAD_DIM), (2, 8, CMP_BLOCK * HEAD_DIM)).astype(BF16)
    w2 = jnp.pad(cmp_w2, ((0, 0), (0, 0), (0, LANES - HEAD_DIM))).astype(BF16)
    k_gain_pad = jnp.pad(k_gain, (0, LANES - HEAD_DIM)).reshape(1, LANES)
    kv_cmp = _compress(blocks, cmp_w1.astype(BF16), w2, pos, k_gain_pad)

    d = np.arange(2 * seq) - (seq + CMP_BLOCK - 1)
    w = _dist_values(rel_table, d, d >= 0)
    bias_c = _skew(w, n_chunk, CMP_STRIDE, seq, 1, seq, mask_last_row=True)[:, 0]
    c_start = np.arange(n_chunk)[None, :] * CMP_STRIDE
    s_start = np.arange(n_sel)[:, None] * SEL_BLOCK
    overlap_t = ((c_start < s_start + SEL_BLOCK) & (c_start + CMP_BLOCK > s_start)
                 & (np.arange(n_chunk)[None, :] < n_chunk - 1))
    overlap_t = jnp.asarray(overlap_t.astype(np.float32), BF16)
    o_cmp, q_aug = _cmp_attn(P, kv_cmp, jnp.swapaxes(kv_cmp[1], -1, -2), bias_c, overlap_t, batch, seq)

    lane = jnp.arange(LANES)
    onehot = (np.arange(seq)[:, None] // SEL_BLOCK == np.arange(LANES - HEAD_DIM)[None, :]).astype(np.float32)
    onehot = jnp.asarray(np.pad(onehot, ((0, 0), (HEAD_DIM, 0))), BF16)
    k_aug = jnp.where(lane < HEAD_DIM, P[:, 24:28], onehot[None, None])
    o_sel = _dense_attn(q_aug, 0, k_aug, 0, vt, 0, bias_sel, n_kv_heads=KH,
                        group=NSA_GROUP, tile=DENSE_TILE, n_kv=1, q_chunk=DENSE_TILE, n_back=None,
                        diag_causal=False, out_dtype=F32, name="nsa_sel_attn")
    bias_win = _tile_bias(rel_table, DENSE_TILE, 3, 1, NSA_WINDOW - 1)
    o_win = _dense_attn(q_aug, 0, P, 28, vt, KH, bias_win, n_kv_heads=KH,
                        group=NSA_GROUP, tile=DENSE_TILE, n_kv=1, q_chunk=DENSE_TILE, n_back=2,
                        diag_causal=False, out_dtype=F32, name="nsa_win_attn")
    return _nsa_merge(o_cmp, o_sel, o_win, gate_logits, batch, seq)


def _fox_decay_kernel(f_ref, bf_ref, tri_ref, q_ref, k_ref, eq_ref, ek_ref, carry):
    @pl.when(pl.program_id(1) == 0)
    def _():
        carry[...] = jnp.zeros_like(carry)

    z = f_ref[0] + bf_ref[...]
    log_f = jnp.minimum(z, 0.0) - jnp.log1p(jnp.exp(-jnp.abs(z)))
    c = carry[0:1, :]
    for part in _split3(log_f):
        c = c + jnp.dot(tri_ref[...], part, preferred_element_type=F32)
    ts = c.shape[0]
    carry[...] = jnp.broadcast_to(c[ts - 1:ts, :], carry.shape)
    lane = lax.broadcasted_iota(jnp.int32, (ts, LANES), 1) - HEAD_DIM
    c = c * LOG2E
    for h in range(N_HEADS):
        c1, c2, c3 = _split3(c[:, h:h + 1])
        c1, c2, c3 = c1.astype(F32), c2.astype(F32), c3.astype(F32)
        eq = jnp.where(lane == 0, c1, jnp.where(lane == 1, c2, jnp.where(lane == 2, c3,
                       jnp.where((lane >= 3) & (lane < 6), 1.0, 0.0))))
        ek = jnp.where(lane == 3, -c1, jnp.where(lane == 4, -c2, jnp.where(lane == 5, -c3,
                       jnp.where((lane >= 0) & (lane < 3), 1.0, 0.0))))
        eq_ref[0, h] = (q_ref[0, h].astype(F32) + eq).astype(eq_ref.dtype)
        ek_ref[0, h] = (k_ref[0, h].astype(F32) + ek).astype(ek_ref.dtype)


def _fox_decay(P, f_logits, b_f, batch, seq):
    ts = 256
    tri = jnp.asarray(np.tril(np.ones((ts, ts), np.float32)), BF16)
    spec = pl.BlockSpec((1, N_HEADS, ts, LANES), lambda b, s: (b, 0, s, 0))
    return pl.pallas_call(
        _fox_decay_kernel,
        grid=(batch, seq // ts),
        in_specs=[
            pl.BlockSpec((1, ts, LANES), lambda b, s: (b, s, 0)),
            pl.BlockSpec((1, LANES), lambda b, s: (0, 0)),
            pl.BlockSpec((ts, ts), lambda b, s: (0, 0)),
            spec,
            pl.BlockSpec((1, N_HEADS, ts, LANES), lambda b, s: (b, 1, s, 0)),
        ],
        out_specs=[spec, spec],
        out_shape=[jax.ShapeDtypeStruct((batch, N_HEADS, seq, LANES), BF16)] * 2,
        scratch_shapes=[pltpu.VMEM((8, LANES), F32)],
        compiler_params=pltpu.CompilerParams(
            dimension_semantics=("parallel", "arbitrary"),
            vmem_limit_bytes=_vmem_limit([ts * LANES * 4, ts * ts * 2, 4 * N_HEADS * ts * LANES * 2],
                                         16 * ts * LANES * 4)),
        name="fox_decay",
    )(f_logits.reshape(batch, seq, LANES), jnp.pad(b_f, (0, LANES - N_HEADS)).reshape(1, LANES), tri, P, P)


def _fox_mixer(P, vt, f_logits, b_f, batch, seq):
    q_aug, k_aug = _fox_decay(P, f_logits, b_f, batch, seq)
    return _dense_attn(q_aug, 0, k_aug, 0, vt, 0, None, n_kv_heads=N_HEADS, group=1, tile=DENSE_TILE,
                       n_kv=4, q_chunk=2 * DENSE_TILE, n_back=None, diag_causal=True,
                       out_dtype=BF16, name="fox_attn")


def _moba_gate_kernel(q_ref, k_ref, qa_ref, *, n_blk):
    k = k_ref[0, 0].astype(F32)
    S = k.shape[0]
    k_mean = jnp.sum(k.reshape(n_blk, S // n_blk, LANES), axis=1) * (1.0 / (S // n_blk))
    hi = k_mean.astype(BF16)
    lo = (k_mean - hi.astype(F32)).astype(BF16)
    q = q_ref[0, 0]
    gate = (lax.dot_general(hi, q, _NT, preferred_element_type=F32)
            + lax.dot_general(lo, q, _NT, preferred_element_type=F32))
    own = lax.broadcasted_iota(jnp.int32, (1, S), 1) // (S // n_blk)
    jidx = lax.broadcasted_iota(jnp.int32, gate.shape, 0)
    vals = jnp.where(jidx < own, gate, -jnp.inf)
    chosen = _rank_select(vals, own, n_blk, MOBA_TOPK) | (jidx == own)
    sel_cols = jnp.concatenate([jnp.zeros((HEAD_DIM, S), F32), jnp.where(chosen, 0.0, SEL_NEG),
                                jnp.zeros((LANES - HEAD_DIM - n_blk, S), F32)], axis=0).T
    qa_ref[0, 0] = (q.astype(F32) + sel_cols).astype(qa_ref.dtype)


def _moba_gate(P, batch, seq):
    n_blk = seq // MOBA_BLOCK
    return pl.pallas_call(
        functools.partial(_moba_gate_kernel, n_blk=n_blk),
        grid=(batch, N_HEADS),
        in_specs=[
            pl.BlockSpec((1, 1, seq, LANES), lambda b, h: (b, h, 0, 0)),
            pl.BlockSpec((1, 1, seq, LANES), lambda b, h: (b, N_HEADS + h, 0, 0)),
        ],
        out_specs=pl.BlockSpec((1, 1, seq, LANES), lambda b, h: (b, h, 0, 0)),
        out_shape=jax.ShapeDtypeStruct((batch, N_HEADS, seq, LANES), BF16),
        compiler_params=pltpu.CompilerParams(
            dimension_semantics=("parallel", "parallel"),
            vmem_limit_bytes=_vmem_limit([3 * seq * LANES * 2], 5 * seq * LANES * 4 + 24 * n_blk * seq * 4)),
        name="moba_gate",
    )(P, P)


def _moba_mixer(P, vt, bias, batch, seq):
    q_aug = _moba_gate(P, batch, seq)
    lane = jnp.arange(LANES)
    onehot = (np.arange(seq)[:, None] // MOBA_BLOCK == np.arange(LANES - HEAD_DIM)[None, :]).astype(np.float32)
    onehot = jnp.asarray(np.pad(onehot, ((0, 0), (HEAD_DIM, 0))), BF16)
    k_aug = jnp.where(lane < HEAD_DIM, P[:, N_HEADS:2 * N_HEADS], onehot[None, None])
    return _dense_attn(q_aug, 0, k_aug, 0, vt, 0, bias, n_kv_heads=N_HEADS, group=1, tile=DENSE_TILE,
                       n_kv=4, q_chunk=2 * DENSE_TILE, n_back=None, diag_causal=False,
                       out_dtype=BF16, name="moba_attn")


def _column_scales(segments):
    gain = [jnp.tile(jnp.ones((HEAD_DIM,), F32) if g is None else g, n) for n, g in segments]
    flag = [jnp.full((n * HEAD_DIM,), 0.0 if g is None else 1.0, F32) for n, g in segments]
    return jnp.concatenate(gain).reshape(1, -1), jnp.concatenate(flag).reshape(1, -1)


def _qkv_scales(q_gain, k_gain):
    return _column_scales([(N_HEADS, q_gain * QK_SCALE), (N_HEADS, k_gain), (N_HEADS, None)])


def _dilated_layer(x2, norm_g, w, qg, kg, rel_table, batch, seq):
    gain, flag = _qkv_scales(qg, kg)
    P = _proj(x2, norm_g, w.astype(BF16), gain, flag, None, batch, seq)
    return _dilated_mixer(P, rel_table, batch, seq)


def _nsa_layer(x2, norm_g, w, cmp_pos, cmp_w1, cmp_w2, qg, kg, bias_sel, rel_table, batch, seq):
    KH = NSA_KV_HEADS
    kvd = KH * HEAD_DIM
    seg = [w[:, D_MODEL + i * kvd:D_MODEL + (i + 1) * kvd] for i in range(6)]
    gain, flag = _column_scales([(N_HEADS, qg * QK_SCALE), (KH, None), (KH, None), (KH, kg), (KH, kg)])
    w_main = jnp.concatenate([w[:, :D_MODEL], seg[0], seg[1], seg[2], seg[4]], axis=1).astype(BF16)
    w_aux = jnp.pad(w[:, D_MODEL + 6 * kvd:], ((0, 0), (0, LANES - 3 * N_HEADS))).astype(BF16)
    P, gate_logits = _proj(x2, norm_g, w_main, gain, flag, w_aux, batch, seq)
    vt = _proj_vt(x2, norm_g, jnp.concatenate([seg[3], seg[5]], axis=1).astype(BF16), batch, seq, DENSE_TILE)
    return _nsa_mixer(P, vt, gate_logits, cmp_pos, cmp_w1, cmp_w2, kg, bias_sel, rel_table, batch, seq)


def _qk_scales(q_gain, k_gain):
    return _column_scales([(N_HEADS, q_gain * QK_SCALE), (N_HEADS, k_gain)])


def _fox_layer(x2, norm_g, w, b_f, qg, kg, batch, seq):
    D = D_MODEL
    gain, flag = _qk_scales(qg, kg)
    w_aux = jnp.pad(w[:, 3 * D:], ((0, 0), (0, LANES - N_HEADS))).astype(BF16)
    P, f_logits = _proj(x2, norm_g, w[:, :2 * D].astype(BF16), gain, flag, w_aux, batch, seq)
    vt = _proj_vt(x2, norm_g, w[:, 2 * D:3 * D].astype(BF16), batch, seq, DENSE_TILE)
    return _fox_mixer(P, vt, f_logits, b_f, batch, seq)


def _moba_layer(x2, norm_g, w, qg, kg, bias, batch, seq):
    D = D_MODEL
    gain, flag = _qk_scales(qg, kg)
    P = _proj(x2, norm_g, w[:, :2 * D].astype(BF16), gain, flag, None, batch, seq)
    vt = _proj_vt(x2, norm_g, w[:, 2 * D:].astype(BF16), batch, seq, DENSE_TILE)
    return _moba_mixer(P, vt, bias, batch, seq)


def kernel(x, rel_table, attn_norm, mlp_norm, q_gain, k_gain, w_out, mlp_w_up, mlp_w_down, dsa_w_in, nsa_w_in,
           nsa_cmp_pos, nsa_cmp_w1, nsa_cmp_w2, fox_w_in, fox_b_f, moba_w_in):
    batch, seq, _ = x.shape
    depth = attn_norm.shape[0]
    x2 = x.reshape(batch * seq, D_MODEL)
    bias_dense = _tile_bias(rel_table, DENSE_TILE, 10, 1, None)
    for layer in range(depth):
        kind, r = layer % 4, layer // 4
        qg, kg, ng = q_gain[layer], k_gain[layer], attn_norm[layer]
        if kind == 0:
            mixed = _dilated_layer(x2, ng, dsa_w_in[r], qg, kg, rel_table, batch, seq)
        elif kind == 1:
            mixed = _nsa_layer(x2, ng, nsa_w_in[r], nsa_cmp_pos[r], nsa_cmp_w1[r], nsa_cmp_w2[r], qg, kg,
                               bias_dense, rel_table, batch, seq)
        elif kind == 2:
            mixed = _fox_layer(x2, ng, fox_w_in[r], fox_b_f[r], qg, kg, batch, seq)
        else:
            mixed = _moba_layer(x2, ng, moba_w_in[r], qg, kg, bias_dense, batch, seq)
        x2 = _out_mlp(x2, mixed.reshape(batch * seq, D_MODEL), w_out[layer].astype(BF16), mlp_norm[layer],
                      mlp_w_up[layer].astype(BF16), mlp_w_down[layer].astype(BF16))
    return x2.reshape(batch, seq, D_MODEL)
```

```python
import functools
import math

import jax
import jax.numpy as jnp
import numpy as np
from jax import lax
from jax.experimental import pallas as pl
from jax.experimental.pallas import tpu as pltpu

F32 = jnp.float32
BF16 = jnp.bfloat16

D_MODEL = 1024
N_HEADS = 16
HEAD_DIM = 64
D_FF = 4 * D_MODEL
RMS_EPS = 1e-6
MASK_NEG = -1e30
TINY = 1e-30
LOG2E = math.log2(math.e)
QK_SCALE = HEAD_DIM ** -0.5 * LOG2E

REL_BUCKETS = 32
REL_MAX_EXACT = REL_BUCKETS // 2
REL_MAX_DIST = 2048

DILATED_PATTERNS = ((128, 1), (512, 4), (2048, 16))
BAND_BLOCK = 128

NSA_KV_HEADS = 4
NSA_GROUP = N_HEADS // NSA_KV_HEADS
CMP_STRIDE = 16
CMP_BLOCK = 2 * CMP_STRIDE
CMP_HIDDEN = 256
SEL_BLOCK = 64
SEL_TOPK = 16
NSA_WINDOW = 512

MOBA_BLOCK = 256
MOBA_TOPK = 3

LANES = 128
VMEM_BYTES_V7X = 64 * 1024 * 1024
VMEM_HEADROOM = 6 * 1024 * 1024

DENSE_TILE = 256
SEL_NEG = -(2.0 ** 100)

_NT = (((1,), (1,)), ((), ()))


def _vmem_limit(block_bytes, temp_bytes):
    need = 2 * int(sum(block_bytes)) + int(temp_bytes) + (2 << 20)
    return int(min(need, VMEM_BYTES_V7X - VMEM_HEADROOM))


def _split3(v):
    a = v.astype(BF16)
    r = v - a.astype(F32)
    b = r.astype(BF16)
    c = (r - b.astype(F32)).astype(BF16)
    return a, b, c


def _proj_kernel(*refs, has_aux):
    if has_aux:
        x_ref, g_ref, w_ref, cg_ref, cf_ref, bd_ref, wa_ref, o_ref, oa_ref, h_scr = refs
    else:
        x_ref, g_ref, w_ref, cg_ref, cf_ref, bd_ref, o_ref, h_scr = refs

    @pl.when(pl.program_id(1) == 0)
    def _():
        x = x_ref[...]
        ms = jnp.mean(x * x, axis=-1, keepdims=True)
        h_scr[...] = (x * lax.rsqrt(ms + RMS_EPS) * g_ref[...]).astype(BF16)
        if has_aux:
            oa_ref[...] = jnp.dot(h_scr[...], wa_ref[...], preferred_element_type=F32)

    y_all = jnp.dot(h_scr[...], w_ref[...], preferred_element_type=F32)
    nb = bd_ref.shape[0]
    lane = lax.broadcasted_iota(jnp.int32, (y_all.shape[0], LANES), 1)
    for c in range(y_all.shape[1] // nb):
        y = y_all[:, c * nb:(c + 1) * nb]
        ms = jnp.dot((y * y).astype(BF16), bd_ref[...], preferred_element_type=F32)
        rs = lax.rsqrt(ms + RMS_EPS)
        out = y * jnp.where(cf_ref[:, c * nb:(c + 1) * nb] > 0.5, rs, 1.0) * cg_ref[:, c * nb:(c + 1) * nb]
        for pair in range(nb // LANES):
            both = out[:, pair * LANES:(pair + 1) * LANES]
            slot = (c * nb) // HEAD_DIM + 2 * pair
            o_ref[0, slot] = jnp.where(lane < HEAD_DIM, both, 0.0).astype(o_ref.dtype)
            o_ref[0, slot + 1] = jnp.where(lane < HEAD_DIM, pltpu.roll(both, HEAD_DIM, 1), 0.0).astype(o_ref.dtype)


def _proj_vt_kernel(x_ref, g_ref, w_ref, o_ref, h_scr):
    @pl.when(pl.program_id(1) == 0)
    def _():
        x = x_ref[...]
        ms = jnp.mean(x * x, axis=-1, keepdims=True)
        h_scr[...] = (x * lax.rsqrt(ms + RMS_EPS) * g_ref[...]).astype(BF16)

    y = jnp.dot(h_scr[...], w_ref[...], preferred_element_type=F32)
    tile = o_ref.shape[-1]
    lane = lax.broadcasted_iota(jnp.int32, (y.shape[0], LANES), 1)
    for pair in range(y.shape[1] // LANES):
        both = y[:, pair * LANES:(pair + 1) * LANES]
        for half, head in enumerate((both, pltpu.roll(both, HEAD_DIM, 1))):
            head_t = jnp.where(lane < HEAD_DIM, head, 0.0).T
            for kt in range(y.shape[0] // tile):
                o_ref[0, 2 * pair + half, kt] = head_t[:, kt * tile:(kt + 1) * tile].astype(o_ref.dtype)


def _proj_vt(x2, norm_g, w, batch, seq, tile):
    T = x2.shape[0]
    N = w.shape[1]
    tm, tn = 1024, 2 * LANES
    spb = seq // tm
    hpt = tn // HEAD_DIM
    blocks = [tm * D_MODEL * 4, D_MODEL * tn * 2, hpt * tm * LANES * 2]
    return pl.pallas_call(
        _proj_vt_kernel,
        grid=(T // tm, N // tn),
        in_specs=[
            pl.BlockSpec((tm, D_MODEL), lambda i, j: (i, 0)),
            pl.BlockSpec((1, D_MODEL), lambda i, j: (0, 0)),
            pl.BlockSpec((D_MODEL, tn), lambda i, j: (0, j)),
        ],
        out_specs=pl.BlockSpec((1, hpt, tm // tile, LANES, tile), lambda i, j: (i // spb, j, i % spb, 0, 0)),
        out_shape=jax.ShapeDtypeStruct((batch, N // HEAD_DIM, seq // tile, LANES, tile), BF16),
        scratch_shapes=[pltpu.VMEM((tm, D_MODEL), BF16)],
        compiler_params=pltpu.CompilerParams(
            dimension_semantics=("parallel", "arbitrary"),
            vmem_limit_bytes=_vmem_limit(blocks, tm * D_MODEL * 2 + 8 * tm * tn * 4)),
        name="proj_vt",
    )(x2, norm_g.reshape(1, D_MODEL), w)


def _proj(x2, norm_g, w, col_gain, col_flag, w_aux, batch, seq):
    T = x2.shape[0]
    N = w.shape[1]
    tm, nb = 1024, 2 * LANES
    tn = 2 * nb if N % (2 * nb) == 0 else nb
    spb = seq // tm
    bd = np.zeros((nb, nb), np.float32)
    for a in range(nb // HEAD_DIM):
        bd[a * HEAD_DIM:(a + 1) * HEAD_DIM, a * HEAD_DIM:(a + 1) * HEAD_DIM] = 1.0 / HEAD_DIM
    bd = jnp.asarray(bd, BF16)
    has_aux = w_aux is not None
    in_specs = [
        pl.BlockSpec((tm, D_MODEL), lambda i, j: (i, 0)),
        pl.BlockSpec((1, D_MODEL), lambda i, j: (0, 0)),
        pl.BlockSpec((D_MODEL, tn), lambda i, j: (0, j)),
        pl.BlockSpec((1, tn), lambda i, j: (0, j)),
        pl.BlockSpec((1, tn), lambda i, j: (0, j)),
        pl.BlockSpec((nb, nb), lambda i, j: (0, 0)),
    ]
    args = [x2, norm_g.reshape(1, D_MODEL), w, col_gain, col_flag, bd]
    hpt = tn // HEAD_DIM
    out_shape = [jax.ShapeDtypeStruct((batch, N // HEAD_DIM, seq, LANES), BF16)]
    out_specs = [pl.BlockSpec((1, hpt, tm, LANES), lambda i, j: (i // spb, j, i % spb, 0))]
    if has_aux:
        in_specs.append(pl.BlockSpec((D_MODEL, LANES), lambda i, j: (0, 0)))
        args.append(w_aux)
        out_shape.append(jax.ShapeDtypeStruct((T, LANES), F32))
        out_specs.append(pl.BlockSpec((tm, LANES), lambda i, j: (i, 0)))
    blocks = [tm * D_MODEL * 4, D_MODEL * tn * 2, hpt * tm * LANES * 2, nb * nb * 2, tm * LANES * 4,
              D_MODEL * LANES * 2]
    res = pl.pallas_call(
        functools.partial(_proj_kernel, has_aux=has_aux),
        grid=(T // tm, N // tn),
        in_specs=in_specs,
        out_specs=out_specs,
        out_shape=out_shape,
        scratch_shapes=[pltpu.VMEM((tm, D_MODEL), BF16)],
        compiler_params=pltpu.CompilerParams(
            dimension_semantics=("parallel", "arbitrary"),
            vmem_limit_bytes=_vmem_limit(blocks, tm * D_MODEL * 2 + 6 * tm * tn * 4)),
        name="proj",
    )(*args)
    return res if has_aux else res[0]


def _out_mlp_kernel(x_ref, m_ref, wo_ref, g_ref, wu_ref, wd_ref, o_ref, h_scr):
    @pl.when(pl.program_id(1) == 0)
    def _():
        x1 = x_ref[...] + jnp.dot(m_ref[...], wo_ref[...], preferred_element_type=F32)
        o_ref[...] = x1
        ms = jnp.mean(x1 * x1, axis=-1, keepdims=True)
        h_scr[...] = (x1 * lax.rsqrt(ms + RMS_EPS) * g_ref[...]).astype(BF16)

    u = jnp.dot(h_scr[...], wu_ref[...], preferred_element_type=F32)
    a = jnp.square(jnp.maximum(u, 0.0)).astype(BF16)
    o_ref[...] += jnp.dot(a, wd_ref[...], preferred_element_type=F32)


def _out_mlp(x2, mixed, w_out, norm_g, w_up, w_down):
    T = x2.shape[0]
    tm, tf = 1024, 1024
    blocks = [tm * D_MODEL * 4, tm * D_MODEL * 2, D_MODEL * D_MODEL * 2,
              D_MODEL * tf * 2, tf * D_MODEL * 2, tm * D_MODEL * 4]
    return pl.pallas_call(
        _out_mlp_kernel,
        grid=(T // tm, D_FF // tf),
        in_specs=[
            pl.BlockSpec((tm, D_MODEL), lambda i, f: (i, 0)),
            pl.BlockSpec((tm, D_MODEL), lambda i, f: (i, 0)),
            pl.BlockSpec((D_MODEL, D_MODEL), lambda i, f: (0, 0)),
            pl.BlockSpec((1, D_MODEL), lambda i, f: (0, 0)),
            pl.BlockSpec((D_MODEL, tf), lambda i, f: (0, f)),
            pl.BlockSpec((tf, D_MODEL), lambda i, f: (f, 0)),
        ],
        out_specs=pl.BlockSpec((tm, D_MODEL), lambda i, f: (i, 0)),
        out_shape=jax.ShapeDtypeStruct((T, D_MODEL), F32),
        scratch_shapes=[pltpu.VMEM((tm, D_MODEL), BF16)],
        compiler_params=pltpu.CompilerParams(
            dimension_semantics=("parallel", "arbitrary"),
            vmem_limit_bytes=_vmem_limit(blocks, tm * D_MODEL * 2 + 3 * tm * tf * 4 + 2 * tm * D_MODEL * 4)),
        name="out_mlp",
    )(x2, mixed, w_out, norm_g.reshape(1, D_MODEL), w_up, w_down)


def _dense_attn_kernel(*refs, tile, n_kv, group, n_qt, n_back, has_bias, n_bias, diag_causal):
    if has_bias:
        q_ref, k_ref, vt_ref, b_ref, o_ref, acc_scr, s_scr, p_scr, st_scr = refs
    else:
        q_ref, k_ref, vt_ref, o_ref, acc_scr, s_scr, p_scr, st_scr = refs
        b_ref = None
    chunk = pl.program_id(2)
    n_streams = n_kv * group
    if diag_causal:
        visible = (lax.broadcasted_iota(jnp.int32, (tile, tile), 0)
                   <= lax.broadcasted_iota(jnp.int32, (tile, tile), 1))

    def one_query_tile(t, _):
        qi = chunk * n_qt + t
        row0 = pl.multiple_of(t * tile, tile)
        qs = [q_ref[0, h, pl.ds(row0, tile), :].astype(F32).T.astype(BF16) for h in range(n_streams)]

        def issue_scores(d, slot):
            start = pl.multiple_of(jnp.maximum(qi - d, 0) * tile, tile)
            for h in range(n_streams):
                s = jnp.dot(k_ref[0, h // group, pl.ds(start, tile), :], qs[h],
                            preferred_element_type=F32)
                if has_bias:
                    s = s + b_ref[h, jnp.minimum(d, n_bias - 1)]
                s_scr[slot, h] = s

        def pv_matmuls(d, slot):
            return [jnp.dot(vt_ref[0, h // group, qi - d], p_scr[slot, h], preferred_element_type=F32)
                    for h in range(n_streams)]

        def accumulate(slot, pvs):
            for h in range(n_streams):
                acc_scr[h] = st_scr[2 + slot, h] * acc_scr[h] + pvs[h]

        def softmax(slot, first):
            for h in range(n_streams):
                s = s_scr[slot, h]
                if first:
                    if diag_causal:
                        s = jnp.where(visible, s, MASK_NEG)
                    m = jnp.max(s, axis=0, keepdims=True)
                    p = jnp.exp2(s - m)
                    st_scr[0, h] = m
                    st_scr[1, h] = jnp.sum(p, axis=0, keepdims=True)
                    st_scr[3 - slot, h] = jnp.zeros_like(m)
                else:
                    m0 = st_scr[0, h]
                    m = jnp.maximum(m0, jnp.max(s, axis=0, keepdims=True))
                    alpha = jnp.exp2(m0 - m)
                    p = jnp.exp2(s - m)
                    st_scr[0, h] = m
                    st_scr[1, h] = alpha * st_scr[1, h] + jnp.sum(p, axis=0, keepdims=True)
                    st_scr[3 - slot, h] = alpha
                p_scr[1 - slot, h] = p.astype(BF16)

        def tile_step(d, slot):
            pvs = pv_matmuls(d - 1, slot)
            issue_scores(d + 1, 1 - slot)
            softmax(slot, False)
            accumulate(slot, pvs)

        acc_scr[...] = jnp.zeros_like(acc_scr)
        issue_scores(0, 0)
        issue_scores(1, 1)
        softmax(0, True)
        n_steps = qi + 1 if n_back is None else jnp.minimum(qi, n_back) + 1

        def two_steps(j, _):
            tile_step(1 + 2 * j, 1)
            tile_step(2 + 2 * j, 0)
            return 0

        lax.fori_loop(0, (n_steps - 1) >> 1, two_steps, 0)
        odd_tail = ((n_steps - 1) & 1) == 1

        @pl.when(odd_tail)
        def _():
            tile_step(n_steps - 1, 1)
            accumulate(0, pv_matmuls(n_steps - 1, 0))

        @pl.when(jnp.logical_not(odd_tail))
        def _():
            accumulate(1, pv_matmuls(n_steps - 1, 1))

        outs = [acc_scr[h][:HEAD_DIM] / jnp.maximum(st_scr[1, h], TINY) for h in range(n_streams)]
        for pair in range(n_streams // 2):
            both = jnp.concatenate([outs[2 * pair], outs[2 * pair + 1]], axis=0)
            o_ref[0, pl.ds(row0, tile), pair * LANES:(pair + 1) * LANES] = both.T.astype(o_ref.dtype)
        return 0

    lax.fori_loop(0, n_qt, one_query_tile, 0)


def _dense_attn(q_arr, q_off, k_arr, k_off, vt_arr, v_off, bias, *, n_kv_heads, group, tile, n_kv, q_chunk,
                n_back, diag_causal, out_dtype, name):
    B, _, S, _ = q_arr.shape
    n_q_heads = n_kv_heads * group
    nq = n_kv * group
    assert q_off % nq == 0 and k_off % n_kv == 0 and v_off % n_kv == 0
    assert S % q_chunk == 0 and q_chunk % tile == 0 and n_kv_heads % n_kv == 0
    n_qt = q_chunk // tile
    has_bias = bias is not None
    n_bias = bias.shape[1] if has_bias else 0
    in_specs = [
        pl.BlockSpec((1, nq, q_chunk, LANES), lambda u, b, c: (b, q_off // nq + u, c, 0)),
        pl.BlockSpec((1, n_kv, S, LANES), lambda u, b, c: (b, k_off // n_kv + u, 0, 0)),
        pl.BlockSpec((1, n_kv, S // tile, LANES, tile), lambda u, b, c: (b, v_off // n_kv + u, 0, 0, 0)),
    ]
    args = [q_arr, k_arr, vt_arr]
    out_bytes = jnp.dtype(out_dtype).itemsize
    blocks = [nq * q_chunk * LANES * 2, 2 * n_kv * S * LANES * 2, nq * q_chunk * LANES * out_bytes]
    if has_bias:
        in_specs.append(pl.BlockSpec((nq, n_bias, tile, tile), lambda u, b, c: (u, 0, 0, 0)))
        args.append(bias)
        blocks.append(nq * n_bias * tile * tile * 4)
    kern = functools.partial(_dense_attn_kernel, tile=tile, n_kv=n_kv, group=group, n_qt=n_qt, n_back=n_back,
                             has_bias=has_bias, n_bias=n_bias, diag_causal=diag_causal)
    assert nq % 2 == 0
    return pl.pallas_call(
        kern,
        grid=(n_kv_heads // n_kv, B, S // q_chunk),
        in_specs=in_specs,
        out_specs=pl.BlockSpec((1, q_chunk, nq * HEAD_DIM), lambda u, b, c: (b, c, u)),
        out_shape=jax.ShapeDtypeStruct((B, S, n_q_heads * HEAD_DIM), out_dtype),
        scratch_shapes=[pltpu.VMEM((nq, LANES, tile), F32),
                        pltpu.VMEM((2, nq, tile, tile), F32),
                        pltpu.VMEM((2, nq, tile, tile), BF16),
                        pltpu.VMEM((4, nq, 1, tile), F32)],
        compiler_params=pltpu.CompilerParams(
            dimension_semantics=("parallel", "parallel", "arbitrary"),
            vmem_limit_bytes=_vmem_limit(blocks, nq * (4 * tile * tile + 3 * LANES * tile) * 4)),
        name=name,
    )(*args)


def _bucket_np(dist):
    d = np.maximum(dist, 0)
    df = np.maximum(d.astype(np.float32), np.float32(1.0))
    large = REL_MAX_EXACT + (np.log(df / np.float32(REL_MAX_EXACT))
                             / np.float32(math.log(REL_MAX_DIST / REL_MAX_EXACT))
                             * np.float32(REL_BUCKETS - REL_MAX_EXACT)).astype(np.int32)
    large = np.minimum(large, REL_BUCKETS - 1)
    return np.where(d < REL_MAX_EXACT, d, large).astype(np.int32)


def _dist_values(rel_table, dist, ok):
    vals = rel_table[jnp.asarray(_bucket_np(dist))] * LOG2E
    return jnp.where(jnp.asarray(ok)[:, None], vals, MASK_NEG).T


def _skew_kernel(w_ref, o_ref, *, step, col0, mask_last_row):
    _, n_out, n_rows, width = o_ref.shape
    x = jnp.broadcast_to(w_ref[0], (n_rows, w_ref.shape[2]))
    x = pltpu.roll(x, 0, 1, stride=step, stride_axis=0)
    if mask_last_row:
        row = lax.broadcasted_iota(jnp.int32, x.shape, 0)
        x = jnp.where(row == n_rows - 1, MASK_NEG, x)
    for a in range(n_out):
        o_ref[0, a] = x[:, col0 + a * width:col0 + (a + 1) * width]


def _skew(w, n_rows, step, col0, n_out, width, mask_last_row=False):
    H, W = w.shape
    assert W % LANES == 0 and col0 % LANES == 0 and width % LANES == 0 and col0 + n_out * width <= W
    assert col0 - step * (n_rows - 1) >= 0
    return pl.pallas_call(
        functools.partial(_skew_kernel, step=step, col0=col0, mask_last_row=mask_last_row),
        grid=(H,),
        in_specs=[pl.BlockSpec((1, 1, W), lambda h: (h, 0, 0))],
        out_specs=pl.BlockSpec((1, n_out, n_rows, width), lambda h: (h, 0, 0, 0)),
        out_shape=jax.ShapeDtypeStruct((H, n_out, n_rows, width), F32),
        compiler_params=pltpu.CompilerParams(
            dimension_semantics=("parallel",),
            vmem_limit_bytes=_vmem_limit([W * 4, n_out * n_rows * width * 4], 3 * n_rows * W * 4)),
        name="bias_skew",
    )(w.reshape(H, 1, W))


def _tile_bias(rel_table, tile, n_tiles, dist_scale, max_dist):
    d = np.arange((n_tiles + 1) * tile) - tile
    ok = d >= 0
    if max_dist is not None:
        ok &= d <= max_dist
    w = _dist_values(rel_table, d * dist_scale, ok)
    return _skew(w, tile, 1, tile, n_tiles, tile)


def _merge_patterns(parts):
    ms = [p[HEAD_DIM:HEAD_DIM + 1, :] for p in parts]
    ls = [p[HEAD_DIM + 1:HEAD_DIM + 2, :] for p in parts]
    m_all = functools.reduce(jnp.maximum, ms)
    ws = [jnp.exp2(m - m_all) for m in ms]
    num = sum(w * p for w, p in zip(ws, parts))
    den = sum(w * l for w, l in zip(ws, ls))
    return num / jnp.maximum(den, TINY)


def _band_attn_kernel(*refs, tile, n_tiles, n_heads, group, n_qt, mode):
    if mode == "merge":
        q_ref, k_ref, v_ref, b_ref, other1_ref, other2_ref, o_ref = refs
    else:
        q_ref, k_ref, v_ref, b_ref, o_ref = refs
    chunk = pl.program_id(2)
    channel = lax.broadcasted_iota(jnp.int32, (LANES, tile), 0)

    def one_query_tile(t, _):
        qi = chunk * n_qt + t
        row0 = pl.multiple_of(t * tile, tile)
        rows = pl.ds(row0, tile)
        k0 = jnp.maximum(qi - (n_tiles - 1), 0)
        keys = pl.ds(pl.multiple_of(k0 * tile, tile), n_tiles * tile)
        case = (n_tiles - 1) - (qi - k0)
        scores = [lax.dot_general(k_ref[0, h // group, keys, :], q_ref[0, h, rows, :], _NT,
                                  preferred_element_type=F32) + b_ref[h, case]
                  for h in range(n_heads)]
        stats = []
        for h in range(n_heads):
            m = jnp.max(scores[h], axis=0, keepdims=True)
            p = jnp.exp2(scores[h] - m)
            stats.append((m, jnp.sum(p, axis=0, keepdims=True), p.astype(BF16)))
        outs = []
        for h in range(n_heads):
            m, l, p = stats[h]
            acc = lax.dot_general(v_ref[0, h // group, keys, :], p, (((0,), (0,)), ((), ())),
                                  preferred_element_type=F32)
            if mode == "norm":
                outs.append(acc / jnp.maximum(l, TINY))
                continue
            out_t = jnp.where(channel == HEAD_DIM, m, jnp.where(channel == HEAD_DIM + 1, l, acc))
            if mode == "merge":
                outs.append(_merge_patterns([out_t, other1_ref[0, h, rows, :].T, other2_ref[0, h, rows, :].T]))
            else:
                o_ref[0, h, rows, :] = out_t.T
        if mode != "stats":
            for pair in range(n_heads // 2):
                both = jnp.concatenate([outs[2 * pair][:HEAD_DIM], outs[2 * pair + 1][:HEAD_DIM]], axis=0)
                o_ref[0, rows, pair * LANES:(pair + 1) * LANES] = both.T.astype(o_ref.dtype)
        return 0

    lax.fori_loop(0, n_qt, one_query_tile, 0)


def _band_cases(tile_bias):
    n_tiles = tile_bias.shape[1]
    masked = jnp.full_like(tile_bias[:, 0], MASK_NEG)
    cases = [jnp.concatenate([tile_bias[:, d] for d in range(n_tiles - 1 - c, -1, -1)] + [masked] * c, axis=1)
             for c in range(n_tiles)]
    return jnp.stack(cases, axis=1)


def _band_attn(q_arr, q_off, kv_arr, k_off, v_off, bias, *, n_q_heads, n_heads, group, q_chunk, mode, name,
               others=None):
    B, _, n, _ = q_arr.shape
    n_tiles, tile = bias.shape[1], bias.shape[3]
    n_kv = n_heads // group
    assert n >= n_tiles * tile and n % q_chunk == 0 and q_chunk % tile == 0 and n_q_heads % n_heads == 0
    assert q_off % n_heads == 0 and k_off % n_kv == 0 and v_off % n_kv == 0
    kern = functools.partial(_band_attn_kernel, tile=tile, n_tiles=n_tiles, n_heads=n_heads, group=group,
                             n_qt=q_chunk // tile, mode=mode)
    head_rows = pl.BlockSpec((1, n_heads, q_chunk, LANES), lambda u, b, c: (b, u, c, 0))
    in_specs = [
        pl.BlockSpec((1, n_heads, q_chunk, LANES), lambda u, b, c: (b, q_off // n_heads + u, c, 0)),
        pl.BlockSpec((1, n_kv, n, LANES), lambda u, b, c: (b, k_off // n_kv + u, 0, 0)),
        pl.BlockSpec((1, n_kv, n, LANES), lambda u, b, c: (b, v_off // n_kv + u, 0, 0)),
        pl.BlockSpec((n_heads, n_tiles, n_tiles * tile, tile), lambda u, b, c: (u, 0, 0, 0)),
    ]
    args = [q_arr, kv_arr, kv_arr, bias]
    blocks = [n_heads * q_chunk * LANES * 2, 2 * n_kv * n * LANES * 2, n_heads * n_tiles * n_tiles * tile * tile * 4,
              n_heads * q_chunk * LANES * 4]
    if mode == "merge":
        in_specs += [head_rows, head_rows]
        args += list(others)
        blocks.append(2 * n_heads * q_chunk * LANES * 4)
    if mode == "stats":
        out_specs = head_rows
        out_shape = jax.ShapeDtypeStruct((B, n_q_heads, n, LANES), F32)
    else:
        assert n_heads % 2 == 0
        out_specs = pl.BlockSpec((1, q_chunk, n_heads * HEAD_DIM), lambda u, b, c: (b, c, u))
        out_shape = jax.ShapeDtypeStruct((B, n, n_q_heads * HEAD_DIM), BF16 if mode == "merge" else F32)
    return pl.pallas_call(
        kern,
        grid=(n_q_heads // n_heads, B, n // q_chunk),
        in_specs=in_specs,
        out_specs=out_specs,
        out_shape=out_shape,
        compiler_params=pltpu.CompilerParams(
            dimension_semantics=("parallel", "parallel", "arbitrary"),
            vmem_limit_bytes=_vmem_limit(blocks, n_heads * 8 * n_tiles * tile * tile * 4)),
        name=name,
    )(*args)


def _dilated_mixer(P, rel_table, batch, seq):
    def band(arr, window, dil, n_heads, q_chunk, mode, name, others=None):
        bias = _band_cases(_tile_bias(rel_table, BAND_BLOCK, 2, dil, window // dil))
        return _band_attn(arr, 0, arr, N_HEADS, 2 * N_HEADS, bias, n_q_heads=N_HEADS, n_heads=n_heads, group=1,
                          q_chunk=q_chunk, mode=mode, name=name, others=others)

    strided = []
    for window, dil in DILATED_PATTERNS:
        if dil == 1:
            continue
        n = seq // dil
        Pr = P.reshape(batch, 3 * N_HEADS, n, dil, LANES).transpose(0, 3, 1, 2, 4)
        Pr = Pr.reshape(batch * dil, 3 * N_HEADS, n, LANES)
        o = band(Pr, window, dil, N_HEADS, min(n, 512), "stats", f"dilated_attn_d{dil}")
        strided.append(o.reshape(batch, dil, N_HEADS, n, LANES).transpose(0, 2, 3, 1, 4)
                       .reshape(batch, N_HEADS, seq, LANES))
    (window, dil), = [pat for pat in DILATED_PATTERNS if pat[1] == 1]
    return band(P, window, dil, N_HEADS // 2, 2 * BAND_BLOCK, "merge", "dilated_attn_merge", others=strided)


def _compress_kernel(x_ref, w1_ref, w2_ref, pos_ref, g_ref, o_ref):
    posw = jnp.dot(pos_ref[0], w1_ref[0], preferred_element_type=F32)[0:1]
    pre = jnp.dot(x_ref[0, 0], w1_ref[0], preferred_element_type=F32) + posw
    hid = 0.5 * pre * (1.0 + jnp.tanh(math.sqrt(2.0 / math.pi) * (pre + 0.044715 * (pre * pre * pre))))
    out = jnp.dot(hid.astype(BF16), w2_ref[0], preferred_element_type=F32)
    ms = jnp.sum(out * out, axis=-1, keepdims=True) * (1.0 / HEAD_DIM)
    normed = out * lax.rsqrt(ms + RMS_EPS) * g_ref[...]
    o_ref[0, 0] = jnp.where(pl.program_id(0) == 0, normed, out).astype(o_ref.dtype)


def _compress(blocks, w1, w2, pos, k_gain_pad):
    _, BK, n, K = blocks.shape
    return pl.pallas_call(
        _compress_kernel,
        grid=(2, BK),
        in_specs=[
            pl.BlockSpec((1, 1, n, K), lambda a, i: (a, i, 0, 0)),
            pl.BlockSpec((1, K, CMP_HIDDEN), lambda a, i: (a, 0, 0)),
            pl.BlockSpec((1, CMP_HIDDEN, LANES), lambda a, i: (a, 0, 0)),
            pl.BlockSpec((1, 8, K), lambda a, i: (a, 0, 0)),
            pl.BlockSpec((1, LANES), lambda a, i: (0, 0)),
        ],
        out_specs=pl.BlockSpec((1, 1, n, LANES), lambda a, i: (a, i, 0, 0)),
        out_shape=jax.ShapeDtypeStruct((2, BK, n, LANES), BF16),
        compiler_params=pltpu.CompilerParams(
            dimension_semantics=("parallel", "parallel"),
            vmem_limit_bytes=_vmem_limit([n * K * 2, K * CMP_HIDDEN * 2, CMP_HIDDEN * LANES * 2, 8 * K * 2,
                                          n * LANES * 2], 8 * n * CMP_HIDDEN * 4)),
        name="nsa_compress",
    )(blocks, w1, w2, pos, k_gain_pad)


def _rank_select(vals, own_row, n_blocks, top):
    jidx = lax.broadcasted_iota(jnp.int32, vals.shape, 0)
    cnt = jnp.zeros(vals.shape, F32)
    for j in range(n_blocks):
        row = vals[j:j + 1, :]
        ahead = (row > vals) | ((row == vals) & (jidx > j))
        cnt = cnt + jnp.where(ahead, 1.0, 0.0)
    return (jidx < own_row) & (cnt < top)


def _cmp_attn_kernel(q_ref, k_ref, vt_ref, b_ref, ov_ref, o_ref, qa_ref, *, tq):
    qi = pl.program_id(1)
    k = k_ref[0, 0]
    vt = vt_ref[0]
    scores = [lax.dot_general(k, q_ref[0, g], _NT, preferred_element_type=F32) + b_ref[g]
              for g in range(NSA_GROUP)]
    psum = None
    all_probs = []
    for s in scores:
        m = jnp.max(s, axis=0, keepdims=True)
        p = jnp.where(m > 0.5 * MASK_NEG, jnp.exp2(s - m), 0.0)
        l = jnp.sum(p, axis=0, keepdims=True)
        probs = p / jnp.maximum(l, TINY)
        all_probs.append(probs.astype(BF16))
        psum = probs if psum is None else psum + probs
    outs = [jnp.dot(vt, p, preferred_element_type=F32)[:HEAD_DIM] for p in all_probs]
    for pair in range(NSA_GROUP // 2):
        both = jnp.concatenate([outs[2 * pair], outs[2 * pair + 1]], axis=0)
        o_ref[0, :, pair * LANES:(pair + 1) * LANES] = both.T.astype(o_ref.dtype)
    imp = None
    for part in _split3(psum):
        term = jnp.dot(ov_ref[...], part, preferred_element_type=F32)
        imp = term if imp is None else imp + term
    n_sel = imp.shape[0]
    pos = qi * tq + lax.broadcasted_iota(jnp.int32, (1, tq), 1)
    own = pos // SEL_BLOCK
    jidx = lax.broadcasted_iota(jnp.int32, imp.shape, 0)
    vals = jnp.where(jidx < own, imp, -1.0)
    chosen = _rank_select(vals, own, n_sel, SEL_TOPK - 1) | (jidx == own)
    sel_cols = jnp.concatenate([jnp.zeros((HEAD_DIM, tq), F32), jnp.where(chosen, 0.0, SEL_NEG)], axis=0).T
    for g in range(NSA_GROUP):
        qa_ref[0, g] = (q_ref[0, g].astype(F32) + sel_cols).astype(qa_ref.dtype)


def _cmp_attn(P, kv_cmp, vt_cmp, bias_c, overlap_t, batch, seq):
    assert seq // SEL_BLOCK == LANES - HEAD_DIM
    tq = 256
    n_cmp = kv_cmp.shape[2]
    n_sel = overlap_t.shape[0]
    kern = functools.partial(_cmp_attn_kernel, tq=tq)
    return pl.pallas_call(
        kern,
        grid=(NSA_KV_HEADS, seq // tq, batch),
        in_specs=[
            pl.BlockSpec((1, NSA_GROUP, tq, LANES), lambda kh, qi, b: (b, kh, qi, 0)),
            pl.BlockSpec((1, 1, n_cmp, LANES), lambda kh, qi, b: (0, b * NSA_KV_HEADS + kh, 0, 0)),
            pl.BlockSpec((1, LANES, n_cmp), lambda kh, qi, b: (b * NSA_KV_HEADS + kh, 0, 0)),
            pl.BlockSpec((NSA_GROUP, n_cmp, tq), lambda kh, qi, b: (kh, 0, qi)),
            pl.BlockSpec((n_sel, n_cmp), lambda kh, qi, b: (0, 0)),
        ],
        out_specs=[
            pl.BlockSpec((1, tq, NSA_GROUP * HEAD_DIM), lambda kh, qi, b: (b, qi, kh)),
            pl.BlockSpec((1, NSA_GROUP, tq, LANES), lambda kh, qi, b: (b, kh, qi, 0)),
        ],
        out_shape=[
            jax.ShapeDtypeStruct((batch, seq, N_HEADS * HEAD_DIM), F32),
            jax.ShapeDtypeStruct((batch, N_HEADS, seq, LANES), BF16),
        ],
        compiler_params=pltpu.CompilerParams(
            dimension_semantics=("parallel", "parallel", "arbitrary"),
            vmem_limit_bytes=_vmem_limit([NSA_GROUP * tq * LANES * 2, 2 * n_cmp * LANES * 2,
                                          NSA_GROUP * tq * n_cmp * 4, NSA_GROUP * tq * LANES * 4, n_sel * tq * 2],
                                         16 * tq * n_cmp * 4)),
        name="nsa_cmp_attn",
    )(P, kv_cmp, vt_cmp, bias_c, overlap_t)


def _nsa_merge_kernel(c_ref, s_ref, w_ref, g_ref, o_ref):
    gates = jax.nn.sigmoid(g_ref[0])
    lane = lax.broadcasted_iota(jnp.int32, gates.shape, 1)
    for pair in range(N_HEADS // 2):
        cols = slice(pair * LANES, (pair + 1) * LANES)

        def gate(branch):
            a, b = 3 * (2 * pair) + branch, 3 * (2 * pair + 1) + branch
            return jnp.where(lane < HEAD_DIM, gates[:, a:a + 1], gates[:, b:b + 1])

        o = gate(0) * c_ref[0, :, cols] + gate(1) * s_ref[0, :, cols] + gate(2) * w_ref[0, :, cols]
        o_ref[0, :, cols] = o.astype(o_ref.dtype)


def _nsa_merge(o_cmp, o_sel, o_win, gate_logits, batch, seq):
    ts = 256
    spec = pl.BlockSpec((1, ts, D_MODEL), lambda b, s: (b, s, 0))
    return pl.pallas_call(
        _nsa_merge_kernel,
        grid=(batch, seq // ts),
        in_specs=[spec, spec, spec, pl.BlockSpec((1, ts, LANES), lambda b, s: (b, s, 0))],
        out_specs=spec,
        out_shape=jax.ShapeDtypeStruct((batch, seq, D_MODEL), BF16),
        compiler_params=pltpu.CompilerParams(
            dimension_semantics=("parallel", "parallel"),
            vmem_limit_bytes=_vmem_limit([3 * ts * D_MODEL * 4, ts * LANES * 4, ts * D_MODEL * 2],
                                         8 * ts * LANES * 4)),
        name="nsa_merge",
    )(o_cmp, o_sel, o_win, gate_logits.reshape(batch, seq, LANES))


def _nsa_mixer(P, vt, gate_logits, cmp_pos, cmp_w1, cmp_w2, k_gain, bias_sel, rel_table, batch, seq):
    KH = NSA_KV_HEADS
    n_chunk = seq // CMP_STRIDE
    n_sel = seq // SEL_BLOCK

    raw = P[:, N_HEADS:N_HEADS + 2 * KH, :, :HEAD_DIM].reshape(batch, 2, KH, n_chunk, CMP_STRIDE * HEAD_DIM)
    raw = raw.transpose(1, 0, 2, 3, 4).reshape(2, batch * KH, n_chunk, CMP_STRIDE * HEAD_DIM)
    blocks = jnp.concatenate([raw, jnp.roll(raw, -1, axis=2)], axis=-1)
    pos = jnp.broadcast_to(cmp_pos.reshape(2, 1, CMP_BLOCK * HEAD_DIM), (2, 8, CMP_BLOCK * HEAD_DIM)).astype(BF16)
    w2 = jnp.pad(cmp_w2, ((0, 0), (0, 0), (0, LANES - HEAD_DIM))).astype(BF16)
    k_gain_pad = jnp.pad(k_gain, (0, LANES - HEAD_DIM)).reshape(1, LANES)
    kv_cmp = _compress(blocks, cmp_w1.astype(BF16), w2, pos, k_gain_pad)

    d = np.arange(2 * seq) - (seq + CMP_BLOCK - 1)
    w = _dist_values(rel_table, d, d >= 0)
    bias_c = _skew(w, n_chunk, CMP_STRIDE, seq, 1, seq, mask_last_row=True)[:, 0]
    c_start = np.arange(n_chunk)[None, :] * CMP_STRIDE
    s_start = np.arange(n_sel)[:, None] * SEL_BLOCK
    overlap_t = ((c_start < s_start + SEL_BLOCK) & (c_start + CMP_BLOCK > s_start)
                 & (np.arange(n_chunk)[None, :] < n_chunk - 1))
    overlap_t = jnp.asarray(overlap_t.astype(np.float32), BF16)
    o_cmp, q_aug = _cmp_attn(P, kv_cmp, jnp.swapaxes(kv_cmp[1], -1, -2), bias_c, overlap_t, batch, seq)

    lane = jnp.arange(LANES)
    onehot = (np.arange(seq)[:, None] // SEL_BLOCK == np.arange(LANES - HEAD_DIM)[None, :]).astype(np.float32)
    onehot = jnp.asarray(np.pad(onehot, ((0, 0), (HEAD_DIM, 0))), BF16)
    k_aug = jnp.where(lane < HEAD_DIM, P[:, 24:28], onehot[None, None])
    o_sel = _dense_attn(q_aug, 0, k_aug, 0, vt, 0, bias_sel, n_kv_heads=KH,
                        group=NSA_GROUP, tile=DENSE_TILE, n_kv=1, q_chunk=DENSE_TILE, n_back=None,
                        diag_causal=False, out_dtype=F32, name="nsa_sel_attn")
    n_win_tiles = -(-(NSA_WINDOW - 1) // DENSE_TILE) + 1
    bias_win = _band_cases(_tile_bias(rel_table, DENSE_TILE, n_win_tiles, 1, NSA_WINDOW - 1))
    o_win = _band_attn(q_aug, 0, P, 28, 32, bias_win, n_q_heads=N_HEADS, n_heads=NSA_GROUP, group=NSA_GROUP,
                       q_chunk=2 * DENSE_TILE, mode="norm", name="nsa_win_attn")
    return _nsa_merge(o_cmp, o_sel, o_win, gate_logits, batch, seq)


def _fox_decay_kernel(f_ref, bf_ref, tri_ref, q_ref, k_ref, eq_ref, ek_ref, carry):
    @pl.when(pl.program_id(1) == 0)
    def _():
        carry[...] = jnp.zeros_like(carry)

    z = f_ref[0] + bf_ref[...]
    log_f = jnp.minimum(z, 0.0) - jnp.log1p(jnp.exp(-jnp.abs(z)))
    c = carry[0:1, :]
    for part in _split3(log_f):
        c = c + jnp.dot(tri_ref[...], part, preferred_element_type=F32)
    ts = c.shape[0]
    carry[...] = jnp.broadcast_to(c[ts - 1:ts, :], carry.shape)
    lane = lax.broadcasted_iota(jnp.int32, (ts, LANES), 1) - HEAD_DIM
    c = c * LOG2E
    for h in range(N_HEADS):
        c1, c2, c3 = _split3(c[:, h:h + 1])
        c1, c2, c3 = c1.astype(F32), c2.astype(F32), c3.astype(F32)
        eq = jnp.where(lane == 0, c1, jnp.where(lane == 1, c2, jnp.where(lane == 2, c3,
                       jnp.where((lane >= 3) & (lane < 6), 1.0, 0.0))))
        ek = jnp.where(lane == 3, -c1, jnp.where(lane == 4, -c2, jnp.where(lane == 5, -c3,
                       jnp.where((lane >= 0) & (lane < 3), 1.0, 0.0))))
        eq_ref[0, h] = (q_ref[0, h].astype(F32) + eq).astype(eq_ref.dtype)
        ek_ref[0, h] = (k_ref[0, h].astype(F32) + ek).astype(ek_ref.dtype)


def _fox_decay(P, f_logits, b_f, batch, seq):
    ts = 256
    tri = jnp.asarray(np.tril(np.ones((ts, ts), np.float32)), BF16)
    spec = pl.BlockSpec((1, N_HEADS, ts, LANES), lambda b, s: (b, 0, s, 0))
    return pl.pallas_call(
        _fox_decay_kernel,
        grid=(batch, seq // ts),
        in_specs=[
            pl.BlockSpec((1, ts, LANES), lambda b, s: (b, s, 0)),
            pl.BlockSpec((1, LANES), lambda b, s: (0, 0)),
            pl.BlockSpec((ts, ts), lambda b, s: (0, 0)),
            spec,
            pl.BlockSpec((1, N_HEADS, ts, LANES), lambda b, s: (b, 1, s, 0)),
        ],
        out_specs=[spec, spec],
        out_shape=[jax.ShapeDtypeStruct((batch, N_HEADS, seq, LANES), BF16)] * 2,
        scratch_shapes=[pltpu.VMEM((8, LANES), F32)],
        compiler_params=pltpu.CompilerParams(
            dimension_semantics=("parallel", "arbitrary"),
            vmem_limit_bytes=_vmem_limit([ts * LANES * 4, ts * ts * 2, 4 * N_HEADS * ts * LANES * 2],
                                         16 * ts * LANES * 4)),
        name="fox_decay",
    )(f_logits.reshape(batch, seq, LANES), jnp.pad(b_f, (0, LANES - N_HEADS)).reshape(1, LANES), tri, P, P)


def _fox_mixer(P, vt, f_logits, b_f, batch, seq):
    q_aug, k_aug = _fox_decay(P, f_logits, b_f, batch, seq)
    return _dense_attn(q_aug, 0, k_aug, 0, vt, 0, None, n_kv_heads=N_HEADS, group=1, tile=DENSE_TILE,
                       n_kv=4, q_chunk=2 * DENSE_TILE, n_back=None, diag_causal=True,
                       out_dtype=BF16, name="fox_attn")


def _moba_gate_kernel(q_ref, k_ref, qa_ref, *, n_blk):
    k = k_ref[0, 0].astype(F32)
    S = k.shape[0]
    k_mean = jnp.sum(k.reshape(n_blk, S // n_blk, LANES), axis=1) * (1.0 / (S // n_blk))
    hi = k_mean.astype(BF16)
    lo = (k_mean - hi.astype(F32)).astype(BF16)
    q = q_ref[0, 0]
    gate = (lax.dot_general(hi, q, _NT, preferred_element_type=F32)
            + lax.dot_general(lo, q, _NT, preferred_element_type=F32))
    own = lax.broadcasted_iota(jnp.int32, (1, S), 1) // (S // n_blk)
    jidx = lax.broadcasted_iota(jnp.int32, gate.shape, 0)
    vals = jnp.where(jidx < own, gate, -jnp.inf)
    chosen = _rank_select(vals, own, n_blk, MOBA_TOPK) | (jidx == own)
    sel_cols = jnp.concatenate([jnp.zeros((HEAD_DIM, S), F32), jnp.where(chosen, 0.0, SEL_NEG),
                                jnp.zeros((LANES - HEAD_DIM - n_blk, S), F32)], axis=0).T
    qa_ref[0, 0] = (q.astype(F32) + sel_cols).astype(qa_ref.dtype)


def _moba_gate(P, batch, seq):
    n_blk = seq // MOBA_BLOCK
    return pl.pallas_call(
        functools.partial(_moba_gate_kernel, n_blk=n_blk),
        grid=(batch, N_HEADS),
        in_specs=[
            pl.BlockSpec((1, 1, seq, LANES), lambda b, h: (b, h, 0, 0)),
            pl.BlockSpec((1, 1, seq, LANES), lambda b, h: (b, N_HEADS + h, 0, 0)),
        ],
        out_specs=pl.BlockSpec((1, 1, seq, LANES), lambda b, h: (b, h, 0, 0)),
        out_shape=jax.ShapeDtypeStruct((batch, N_HEADS, seq, LANES), BF16),
        compiler_params=pltpu.CompilerParams(
            dimension_semantics=("parallel", "parallel"),
            vmem_limit_bytes=_vmem_limit([3 * seq * LANES * 2], 5 * seq * LANES * 4 + 24 * n_blk * seq * 4)),
        name="moba_gate",
    )(P, P)


def _moba_mixer(P, vt, bias, batch, seq):
    q_aug = _moba_gate(P, batch, seq)
    lane = jnp.arange(LANES)
    onehot = (np.arange(seq)[:, None] // MOBA_BLOCK == np.arange(LANES - HEAD_DIM)[None, :]).astype(np.float32)
    onehot = jnp.asarray(np.pad(onehot, ((0, 0), (HEAD_DIM, 0))), BF16)
    k_aug = jnp.where(lane < HEAD_DIM, P[:, N_HEADS:2 * N_HEADS], onehot[None, None])
    return _dense_attn(q_aug, 0, k_aug, 0, vt, 0, bias, n_kv_heads=N_HEADS, group=1, tile=DENSE_TILE,
                       n_kv=4, q_chunk=2 * DENSE_TILE, n_back=None, diag_causal=False,
                       out_dtype=BF16, name="moba_attn")


def _column_scales(segments):
    gain = [jnp.tile(jnp.ones((HEAD_DIM,), F32) if g is None else g, n) for n, g in segments]
    flag = [jnp.full((n * HEAD_DIM,), 0.0 if g is None else 1.0, F32) for n, g in segments]
    return jnp.concatenate(gain).reshape(1, -1), jnp.concatenate(flag).reshape(1, -1)


def _qkv_scales(q_gain, k_gain):
    return _column_scales([(N_HEADS, q_gain * QK_SCALE), (N_HEADS, k_gain), (N_HEADS, None)])


def _dilated_layer(x2, norm_g, w, qg, kg, rel_table, batch, seq):
    gain, flag = _qkv_scales(qg, kg)
    P = _proj(x2, norm_g, w.astype(BF16), gain, flag, None, batch, seq)
    return _dilated_mixer(P, rel_table, batch, seq)


def _nsa_layer(x2, norm_g, w, cmp_pos, cmp_w1, cmp_w2, qg, kg, bias_sel, rel_table, batch, seq):
    KH = NSA_KV_HEADS
    kvd = KH * HEAD_DIM
    seg = [w[:, D_MODEL + i * kvd:D_MODEL + (i + 1) * kvd] for i in range(6)]
    gain, flag = _column_scales([(N_HEADS, qg * QK_SCALE), (KH, None), (KH, None), (KH, kg), (KH, kg), (KH, None)])
    w_main = jnp.concatenate([w[:, :D_MODEL], seg[0], seg[1], seg[2], seg[4], seg[5]], axis=1).astype(BF16)
    w_aux = jnp.pad(w[:, D_MODEL + 6 * kvd:], ((0, 0), (0, LANES - 3 * N_HEADS))).astype(BF16)
    P, gate_logits = _proj(x2, norm_g, w_main, gain, flag, w_aux, batch, seq)
    vt = _proj_vt(x2, norm_g, seg[3].astype(BF16), batch, seq, DENSE_TILE)
    return _nsa_mixer(P, vt, gate_logits, cmp_pos, cmp_w1, cmp_w2, kg, bias_sel, rel_table, batch, seq)


def _qk_scales(q_gain, k_gain):
    return _column_scales([(N_HEADS, q_gain * QK_SCALE), (N_HEADS, k_gain)])


def _fox_layer(x2, norm_g, w, b_f, qg, kg, batch, seq):
    D = D_MODEL
    gain, flag = _qk_scales(qg, kg)
    w_aux = jnp.pad(w[:, 3 * D:], ((0, 0), (0, LANES - N_HEADS))).astype(BF16)
    P, f_logits = _proj(x2, norm_g, w[:, :2 * D].astype(BF16), gain, flag, w_aux, batch, seq)
    vt = _proj_vt(x2, norm_g, w[:, 2 * D:3 * D].astype(BF16), batch, seq, DENSE_TILE)
    return _fox_mixer(P, vt, f_logits, b_f, batch, seq)


def _moba_layer(x2, norm_g, w, qg, kg, bias, batch, seq):
    D = D_MODEL
    gain, flag = _qk_scales(qg, kg)
    P = _proj(x2, norm_g, w[:, :2 * D].astype(BF16), gain, flag, None, batch, seq)
    vt = _proj_vt(x2, norm_g, w[:, 2 * D:].astype(BF16), batch, seq, DENSE_TILE)
    return _moba_mixer(P, vt, bias, batch, seq)


def kernel(x, rel_table, attn_norm, mlp_norm, q_gain, k_gain, w_out, mlp_w_up, mlp_w_down, dsa_w_in, nsa_w_in,
           nsa_cmp_pos, nsa_cmp_w1, nsa_cmp_w2, fox_w_in, fox_b_f, moba_w_in):
    batch, seq, _ = x.shape
    depth = attn_norm.shape[0]
    x2 = x.reshape(batch * seq, D_MODEL)
    bias_dense = _tile_bias(rel_table, DENSE_TILE, 10, 1, None)
    for layer in range(depth):
        kind, r = layer % 4, layer // 4
        qg, kg, ng = q_gain[layer], k_gain[layer], attn_norm[layer]
        if kind == 0:
            mixed = _dilated_layer(x2, ng, dsa_w_in[r], qg, kg, rel_table, batch, seq)
        elif kind == 1:
            mixed = _nsa_layer(x2, ng, nsa_w_in[r], nsa_cmp_pos[r], nsa_cmp_w1[r], nsa_cmp_w2[r], qg, kg,
                               bias_dense, rel_table, batch, seq)
        elif kind == 2:
            mixed = _fox_layer(x2, ng, fox_w_in[r], fox_b_f[r], qg, kg, batch, seq)
        else:
            mixed = _moba_layer(x2, ng, moba_w_in[r], qg, kg, bias_dense, batch, seq)
        x2 = _out_mlp(x2, mixed.reshape(batch * seq, D_MODEL), w_out[layer].astype(BF16), mlp_norm[layer],
                      mlp_w_up[layer].astype(BF16), mlp_w_down[layer].astype(BF16))
    return x2.reshape(batch, seq, D_MODEL)
```

```python
import functools
import math

import jax
import jax.numpy as jnp
import numpy as np
from jax import lax
from jax.experimental import pallas as pl
from jax.experimental.pallas import tpu as pltpu

F32 = jnp.float32
BF16 = jnp.bfloat16

D_MODEL = 1024
N_HEADS = 16
HEAD_DIM = 64
D_FF = 4 * D_MODEL
RMS_EPS = 1e-6
MASK_NEG = -1e30
TINY = 1e-30
LOG2E = math.log2(math.e)
QK_SCALE = HEAD_DIM ** -0.5 * LOG2E

REL_BUCKETS = 32
REL_MAX_EXACT = REL_BUCKETS // 2
REL_MAX_DIST = 2048

DILATED_PATTERNS = ((128, 1), (512, 4), (2048, 16))
BAND_BLOCK = 128

NSA_KV_HEADS = 4
NSA_GROUP = N_HEADS // NSA_KV_HEADS
CMP_STRIDE = 16
CMP_BLOCK = 2 * CMP_STRIDE
CMP_HIDDEN = 256
SEL_BLOCK = 64
SEL_TOPK = 16
NSA_WINDOW = 512

MOBA_BLOCK = 256
MOBA_TOPK = 3

LANES = 128
VMEM_BYTES_V7X = 64 * 1024 * 1024
VMEM_HEADROOM = 6 * 1024 * 1024

DENSE_TILE = 256
SEL_NEG = -(2.0 ** 100)

_NT = (((1,), (1,)), ((), ()))


def _vmem_limit(block_bytes, temp_bytes):
    need = 2 * int(sum(block_bytes)) + int(temp_bytes) + (2 << 20)
    return int(min(need, VMEM_BYTES_V7X - VMEM_HEADROOM))


def _split3(v):
    a = v.astype(BF16)
    r = v - a.astype(F32)
    b = r.astype(BF16)
    c = (r - b.astype(F32)).astype(BF16)
    return a, b, c


def _proj_kernel(*refs, has_aux):
    if has_aux:
        x_ref, g_ref, w_ref, cg_ref, cf_ref, bd_ref, wa_ref, o_ref, oa_ref, h_scr = refs
    else:
        x_ref, g_ref, w_ref, cg_ref, cf_ref, bd_ref, o_ref, h_scr = refs

    @pl.when(pl.program_id(1) == 0)
    def _():
        x = x_ref[...]
        ms = jnp.mean(x * x, axis=-1, keepdims=True)
        h_scr[...] = (x * lax.rsqrt(ms + RMS_EPS) * g_ref[...]).astype(BF16)
        if has_aux:
            oa_ref[...] = jnp.dot(h_scr[...], wa_ref[...], preferred_element_type=F32)

    y_all = jnp.dot(h_scr[...], w_ref[...], preferred_element_type=F32)
    nb = bd_ref.shape[0]
    lane = lax.broadcasted_iota(jnp.int32, (y_all.shape[0], LANES), 1)
    for c in range(y_all.shape[1] // nb):
        y = y_all[:, c * nb:(c + 1) * nb]
        ms = jnp.dot((y * y).astype(BF16), bd_ref[...], preferred_element_type=F32)
        rs = lax.rsqrt(ms + RMS_EPS)
        out = y * jnp.where(cf_ref[:, c * nb:(c + 1) * nb] > 0.5, rs, 1.0) * cg_ref[:, c * nb:(c + 1) * nb]
        for pair in range(nb // LANES):
            both = out[:, pair * LANES:(pair + 1) * LANES]
            slot = (c * nb) // HEAD_DIM + 2 * pair
            o_ref[0, slot] = jnp.where(lane < HEAD_DIM, both, 0.0).astype(o_ref.dtype)
            o_ref[0, slot + 1] = jnp.where(lane < HEAD_DIM, pltpu.roll(both, HEAD_DIM, 1), 0.0).astype(o_ref.dtype)


def _proj_vt_kernel(x_ref, g_ref, w_ref, o_ref, h_scr):
    @pl.when(pl.program_id(1) == 0)
    def _():
        x = x_ref[...]
        ms = jnp.mean(x * x, axis=-1, keepdims=True)
        h_scr[...] = (x * lax.rsqrt(ms + RMS_EPS) * g_ref[...]).astype(BF16)

    y = jnp.dot(h_scr[...], w_ref[...], preferred_element_type=F32)
    tile = o_ref.shape[-1]
    lane = lax.broadcasted_iota(jnp.int32, (y.shape[0], LANES), 1)
    for pair in range(y.shape[1] // LANES):
        both = y[:, pair * LANES:(pair + 1) * LANES]
        for half, head in enumerate((both, pltpu.roll(both, HEAD_DIM, 1))):
            head_t = jnp.where(lane < HEAD_DIM, head, 0.0).T
            for kt in range(y.shape[0] // tile):
                o_ref[0, 2 * pair + half, kt] = head_t[:, kt * tile:(kt + 1) * tile].astype(o_ref.dtype)


def _proj_vt(x2, norm_g, w, batch, seq, tile):
    T = x2.shape[0]
    N = w.shape[1]
    tm, tn = 1024, 2 * LANES
    spb = seq // tm
    hpt = tn // HEAD_DIM
    blocks = [tm * D_MODEL * 4, D_MODEL * tn * 2, hpt * tm * LANES * 2]
    return pl.pallas_call(
        _proj_vt_kernel,
        grid=(T // tm, N // tn),
        in_specs=[
            pl.BlockSpec((tm, D_MODEL), lambda i, j: (i, 0)),
            pl.BlockSpec((1, D_MODEL), lambda i, j: (0, 0)),
            pl.BlockSpec((D_MODEL, tn), lambda i, j: (0, j)),
        ],
        out_specs=pl.BlockSpec((1, hpt, tm // tile, LANES, tile), lambda i, j: (i // spb, j, i % spb, 0, 0)),
        out_shape=jax.ShapeDtypeStruct((batch, N // HEAD_DIM, seq // tile, LANES, tile), BF16),
        scratch_shapes=[pltpu.VMEM((tm, D_MODEL), BF16)],
        compiler_params=pltpu.CompilerParams(
            dimension_semantics=("parallel", "arbitrary"),
            vmem_limit_bytes=_vmem_limit(blocks, tm * D_MODEL * 2 + 8 * tm * tn * 4)),
        name="proj_vt",
    )(x2, norm_g.reshape(1, D_MODEL), w)


def _proj(x2, norm_g, w, col_gain, col_flag, w_aux, batch, seq):
    T = x2.shape[0]
    N = w.shape[1]
    tm, nb = 1024, 2 * LANES
    tn = next(c * nb for c in (3, 2, 1) if N % (c * nb) == 0)
    spb = seq // tm
    bd = np.zeros((nb, nb), np.float32)
    for a in range(nb // HEAD_DIM):
        bd[a * HEAD_DIM:(a + 1) * HEAD_DIM, a * HEAD_DIM:(a + 1) * HEAD_DIM] = 1.0 / HEAD_DIM
    bd = jnp.asarray(bd, BF16)
    has_aux = w_aux is not None
    in_specs = [
        pl.BlockSpec((tm, D_MODEL), lambda i, j: (i, 0)),
        pl.BlockSpec((1, D_MODEL), lambda i, j: (0, 0)),
        pl.BlockSpec((D_MODEL, tn), lambda i, j: (0, j)),
        pl.BlockSpec((1, tn), lambda i, j: (0, j)),
        pl.BlockSpec((1, tn), lambda i, j: (0, j)),
        pl.BlockSpec((nb, nb), lambda i, j: (0, 0)),
    ]
    args = [x2, norm_g.reshape(1, D_MODEL), w, col_gain, col_flag, bd]
    hpt = tn // HEAD_DIM
    out_shape = [jax.ShapeDtypeStruct((batch, N // HEAD_DIM, seq, LANES), BF16)]
    out_specs = [pl.BlockSpec((1, hpt, tm, LANES), lambda i, j: (i // spb, j, i % spb, 0))]
    if has_aux:
        in_specs.append(pl.BlockSpec((D_MODEL, LANES), lambda i, j: (0, 0)))
        args.append(w_aux)
        out_shape.append(jax.ShapeDtypeStruct((T, LANES), F32))
        out_specs.append(pl.BlockSpec((tm, LANES), lambda i, j: (i, 0)))
    blocks = [tm * D_MODEL * 4, D_MODEL * tn * 2, hpt * tm * LANES * 2, nb * nb * 2, tm * LANES * 4,
              D_MODEL * LANES * 2]
    res = pl.pallas_call(
        functools.partial(_proj_kernel, has_aux=has_aux),
        grid=(T // tm, N // tn),
        in_specs=in_specs,
        out_specs=out_specs,
        out_shape=out_shape,
        scratch_shapes=[pltpu.VMEM((tm, D_MODEL), BF16)],
        compiler_params=pltpu.CompilerParams(
            dimension_semantics=("parallel", "arbitrary"),
            vmem_limit_bytes=_vmem_limit(blocks, tm * D_MODEL * 2 + 6 * tm * tn * 4)),
        name="proj",
    )(*args)
    return res if has_aux else res[0]


def _out_mlp_kernel(x_ref, m_ref, wo_ref, g_ref, wu_ref, wd_ref, o_ref, h_scr):
    @pl.when(pl.program_id(1) == 0)
    def _():
        x1 = x_ref[...] + jnp.dot(m_ref[...], wo_ref[...], preferred_element_type=F32)
        o_ref[...] = x1
        ms = jnp.mean(x1 * x1, axis=-1, keepdims=True)
        h_scr[...] = (x1 * lax.rsqrt(ms + RMS_EPS) * g_ref[...]).astype(BF16)

    u = jnp.dot(h_scr[...], wu_ref[...], preferred_element_type=F32)
    a = jnp.square(jnp.maximum(u, 0.0)).astype(BF16)
    o_ref[...] += jnp.dot(a, wd_ref[...], preferred_element_type=F32)


def _out_mlp(x2, mixed, w_out, norm_g, w_up, w_down):
    T = x2.shape[0]
    tm, tf = 1024, 1024
    blocks = [tm * D_MODEL * 4, tm * D_MODEL * 2, D_MODEL * D_MODEL * 2,
              D_MODEL * tf * 2, tf * D_MODEL * 2, tm * D_MODEL * 4]
    return pl.pallas_call(
        _out_mlp_kernel,
        grid=(T // tm, D_FF // tf),
        in_specs=[
            pl.BlockSpec((tm, D_MODEL), lambda i, f: (i, 0)),
            pl.BlockSpec((tm, D_MODEL), lambda i, f: (i, 0)),
            pl.BlockSpec((D_MODEL, D_MODEL), lambda i, f: (0, 0)),
            pl.BlockSpec((1, D_MODEL), lambda i, f: (0, 0)),
            pl.BlockSpec((D_MODEL, tf), lambda i, f: (0, f)),
            pl.BlockSpec((tf, D_MODEL), lambda i, f: (f, 0)),
        ],
        out_specs=pl.BlockSpec((tm, D_MODEL), lambda i, f: (i, 0)),
        out_shape=jax.ShapeDtypeStruct((T, D_MODEL), F32),
        scratch_shapes=[pltpu.VMEM((tm, D_MODEL), BF16)],
        compiler_params=pltpu.CompilerParams(
            dimension_semantics=("parallel", "arbitrary"),
            vmem_limit_bytes=_vmem_limit(blocks, tm * D_MODEL * 2 + 3 * tm * tf * 4 + 2 * tm * D_MODEL * 4)),
        name="out_mlp",
    )(x2, mixed, w_out, norm_g.reshape(1, D_MODEL), w_up, w_down)


def _dense_attn_kernel(*refs, tile, n_kv, group, n_qt, n_back, has_bias, n_bias, diag_causal):
    if has_bias:
        q_ref, k_ref, vt_ref, b_ref, o_ref, acc_scr, s_scr, p_scr, st_scr = refs
    else:
        q_ref, k_ref, vt_ref, o_ref, acc_scr, s_scr, p_scr, st_scr = refs
        b_ref = None
    chunk = pl.program_id(2)
    n_streams = n_kv * group
    if diag_causal:
        visible = (lax.broadcasted_iota(jnp.int32, (tile, tile), 0)
                   <= lax.broadcasted_iota(jnp.int32, (tile, tile), 1))

    def one_query_tile(t, _):
        qi = chunk * n_qt + t
        row0 = pl.multiple_of(t * tile, tile)
        qs = [q_ref[0, h, pl.ds(row0, tile), :].astype(F32).T.astype(BF16) for h in range(n_streams)]

        def issue_scores(d, slot):
            start = pl.multiple_of(jnp.maximum(qi - d, 0) * tile, tile)
            for h in range(n_streams):
                s = jnp.dot(k_ref[0, h // group, pl.ds(start, tile), :], qs[h],
                            preferred_element_type=F32)
                if has_bias:
                    s = s + b_ref[h, jnp.minimum(d, n_bias - 1)]
                s_scr[slot, h] = s

        def pv_matmuls(d, slot):
            return [jnp.dot(vt_ref[0, h // group, qi - d], p_scr[slot, h], preferred_element_type=F32)
                    for h in range(n_streams)]

        def accumulate(slot, pvs):
            for h in range(n_streams):
                acc_scr[h] = st_scr[2 + slot, h] * acc_scr[h] + pvs[h]

        def softmax(slot, first):
            for h in range(n_streams):
                s = s_scr[slot, h]
                if first:
                    if diag_causal:
                        s = jnp.where(visible, s, MASK_NEG)
                    m = jnp.max(s, axis=0, keepdims=True)
                    p = jnp.exp2(s - m)
                    st_scr[0, h] = m
                    st_scr[1, h] = jnp.sum(p, axis=0, keepdims=True)
                    st_scr[3 - slot, h] = jnp.zeros_like(m)
                else:
                    m0 = st_scr[0, h]
                    m = jnp.maximum(m0, jnp.max(s, axis=0, keepdims=True))
                    alpha = jnp.exp2(m0 - m)
                    p = jnp.exp2(s - m)
                    st_scr[0, h] = m
                    st_scr[1, h] = alpha * st_scr[1, h] + jnp.sum(p, axis=0, keepdims=True)
                    st_scr[3 - slot, h] = alpha
                p_scr[1 - slot, h] = p.astype(BF16)

        def tile_step(d, slot):
            pvs = pv_matmuls(d - 1, slot)
            issue_scores(d + 1, 1 - slot)
            softmax(slot, False)
            accumulate(slot, pvs)

        acc_scr[...] = jnp.zeros_like(acc_scr)
        issue_scores(0, 0)
        issue_scores(1, 1)
        softmax(0, True)
        n_steps = qi + 1 if n_back is None else jnp.minimum(qi, n_back) + 1

        def two_steps(j, _):
            tile_step(1 + 2 * j, 1)
            tile_step(2 + 2 * j, 0)
            return 0

        lax.fori_loop(0, (n_steps - 1) >> 1, two_steps, 0)
        odd_tail = ((n_steps - 1) & 1) == 1

        @pl.when(odd_tail)
        def _():
            tile_step(n_steps - 1, 1)
            accumulate(0, pv_matmuls(n_steps - 1, 0))

        @pl.when(jnp.logical_not(odd_tail))
        def _():
            accumulate(1, pv_matmuls(n_steps - 1, 1))

        outs = [acc_scr[h][:HEAD_DIM] / jnp.maximum(st_scr[1, h], TINY) for h in range(n_streams)]
        for pair in range(n_streams // 2):
            both = jnp.concatenate([outs[2 * pair], outs[2 * pair + 1]], axis=0)
            o_ref[0, pl.ds(row0, tile), pair * LANES:(pair + 1) * LANES] = both.T.astype(o_ref.dtype)
        return 0

    lax.fori_loop(0, n_qt, one_query_tile, 0)


def _dense_attn(q_arr, q_off, k_arr, k_off, vt_arr, v_off, bias, *, n_kv_heads, group, tile, n_kv, q_chunk,
                n_back, diag_causal, out_dtype, name):
    B, _, S, _ = q_arr.shape
    n_q_heads = n_kv_heads * group
    nq = n_kv * group
    assert q_off % nq == 0 and k_off % n_kv == 0 and v_off % n_kv == 0
    assert S % q_chunk == 0 and q_chunk % tile == 0 and n_kv_heads % n_kv == 0
    n_qt = q_chunk // tile
    has_bias = bias is not None
    n_bias = bias.shape[1] if has_bias else 0
    in_specs = [
        pl.BlockSpec((1, nq, q_chunk, LANES), lambda u, b, c: (b, q_off // nq + u, c, 0)),
        pl.BlockSpec((1, n_kv, S, LANES), lambda u, b, c: (b, k_off // n_kv + u, 0, 0)),
        pl.BlockSpec((1, n_kv, S // tile, LANES, tile), lambda u, b, c: (b, v_off // n_kv + u, 0, 0, 0)),
    ]
    args = [q_arr, k_arr, vt_arr]
    out_bytes = jnp.dtype(out_dtype).itemsize
    blocks = [nq * q_chunk * LANES * 2, 2 * n_kv * S * LANES * 2, nq * q_chunk * LANES * out_bytes]
    if has_bias:
        in_specs.append(pl.BlockSpec((nq, n_bias, tile, tile), lambda u, b, c: (u, 0, 0, 0)))
        args.append(bias)
        blocks.append(nq * n_bias * tile * tile * 4)
    kern = functools.partial(_dense_attn_kernel, tile=tile, n_kv=n_kv, group=group, n_qt=n_qt, n_back=n_back,
                             has_bias=has_bias, n_bias=n_bias, diag_causal=diag_causal)
    assert nq % 2 == 0
    return pl.pallas_call(
        kern,
        grid=(n_kv_heads // n_kv, B, S // q_chunk),
        in_specs=in_specs,
        out_specs=pl.BlockSpec((1, q_chunk, nq * HEAD_DIM), lambda u, b, c: (b, c, u)),
        out_shape=jax.ShapeDtypeStruct((B, S, n_q_heads * HEAD_DIM), out_dtype),
        scratch_shapes=[pltpu.VMEM((nq, LANES, tile), F32),
                        pltpu.VMEM((2, nq, tile, tile), F32),
                        pltpu.VMEM((2, nq, tile, tile), BF16),
                        pltpu.VMEM((4, nq, 1, tile), F32)],
        compiler_params=pltpu.CompilerParams(
            dimension_semantics=("parallel", "parallel", "arbitrary"),
            vmem_limit_bytes=_vmem_limit(blocks, nq * (4 * tile * tile + 3 * LANES * tile) * 4)),
        name=name,
    )(*args)


def _bucket_np(dist):
    d = np.maximum(dist, 0)
    df = np.maximum(d.astype(np.float32), np.float32(1.0))
    large = REL_MAX_EXACT + (np.log(df / np.float32(REL_MAX_EXACT))
                             / np.float32(math.log(REL_MAX_DIST / REL_MAX_EXACT))
                             * np.float32(REL_BUCKETS - REL_MAX_EXACT)).astype(np.int32)
    large = np.minimum(large, REL_BUCKETS - 1)
    return np.where(d < REL_MAX_EXACT, d, large).astype(np.int32)


def _dist_values(rel_table, dist, ok):
    vals = rel_table[jnp.asarray(_bucket_np(dist))] * LOG2E
    return jnp.where(jnp.asarray(ok)[:, None], vals, MASK_NEG).T


def _skew_kernel(w_ref, o_ref, *, step, col0, mask_last_row):
    _, n_out, n_rows, width = o_ref.shape
    x = jnp.broadcast_to(w_ref[0], (n_rows, w_ref.shape[2]))
    x = pltpu.roll(x, 0, 1, stride=step, stride_axis=0)
    if mask_last_row:
        row = lax.broadcasted_iota(jnp.int32, x.shape, 0)
        x = jnp.where(row == n_rows - 1, MASK_NEG, x)
    for a in range(n_out):
        o_ref[0, a] = x[:, col0 + a * width:col0 + (a + 1) * width]


def _skew(w, n_rows, step, col0, n_out, width, mask_last_row=False):
    H, W = w.shape
    assert W % LANES == 0 and col0 % LANES == 0 and width % LANES == 0 and col0 + n_out * width <= W
    assert col0 - step * (n_rows - 1) >= 0
    return pl.pallas_call(
        functools.partial(_skew_kernel, step=step, col0=col0, mask_last_row=mask_last_row),
        grid=(H,),
        in_specs=[pl.BlockSpec((1, 1, W), lambda h: (h, 0, 0))],
        out_specs=pl.BlockSpec((1, n_out, n_rows, width), lambda h: (h, 0, 0, 0)),
        out_shape=jax.ShapeDtypeStruct((H, n_out, n_rows, width), F32),
        compiler_params=pltpu.CompilerParams(
            dimension_semantics=("parallel",),
            vmem_limit_bytes=_vmem_limit([W * 4, n_out * n_rows * width * 4], 3 * n_rows * W * 4)),
        name="bias_skew",
    )(w.reshape(H, 1, W))


def _tile_bias(rel_table, tile, n_tiles, dist_scale, max_dist):
    d = np.arange((n_tiles + 1) * tile) - tile
    ok = d >= 0
    if max_dist is not None:
        ok &= d <= max_dist
    w = _dist_values(rel_table, d * dist_scale, ok)
    return _skew(w, tile, 1, tile, n_tiles, tile)


def _merge_patterns(parts):
    ms = [p[HEAD_DIM:HEAD_DIM + 1, :] for p in parts]
    ls = [p[HEAD_DIM + 1:HEAD_DIM + 2, :] for p in parts]
    m_all = functools.reduce(jnp.maximum, ms)
    ws = [jnp.exp2(m - m_all) for m in ms]
    num = sum(w * p for w, p in zip(ws, parts))
    den = sum(w * l for w, l in zip(ws, ls))
    return num / jnp.maximum(den, TINY)


def _band_attn_kernel(*refs, tile, n_tiles, n_heads, group, n_qt, mode):
    if mode == "merge":
        q_ref, k_ref, v_ref, b_ref, other1_ref, other2_ref, o_ref = refs
    else:
        q_ref, k_ref, v_ref, b_ref, o_ref = refs
    chunk = pl.program_id(2)
    channel = lax.broadcasted_iota(jnp.int32, (LANES, tile), 0)

    def one_query_tile(t, _):
        qi = chunk * n_qt + t
        row0 = pl.multiple_of(t * tile, tile)
        rows = pl.ds(row0, tile)
        k0 = jnp.maximum(qi - (n_tiles - 1), 0)
        keys = pl.ds(pl.multiple_of(k0 * tile, tile), n_tiles * tile)
        case = (n_tiles - 1) - (qi - k0)
        scores = [lax.dot_general(k_ref[0, h // group, keys, :], q_ref[0, h, rows, :], _NT,
                                  preferred_element_type=F32) + b_ref[h, case]
                  for h in range(n_heads)]
        stats = []
        for h in range(n_heads):
            m = jnp.max(scores[h], axis=0, keepdims=True)
            p = jnp.exp2(scores[h] - m)
            stats.append((m, jnp.sum(p, axis=0, keepdims=True), p.astype(BF16)))
        outs = []
        for h in range(n_heads):
            m, l, p = stats[h]
            acc = lax.dot_general(v_ref[0, h // group, keys, :], p, (((0,), (0,)), ((), ())),
                                  preferred_element_type=F32)
            if mode == "norm":
                outs.append(acc / jnp.maximum(l, TINY))
                continue
            out_t = jnp.where(channel == HEAD_DIM, m, jnp.where(channel == HEAD_DIM + 1, l, acc))
            if mode == "merge":
                outs.append(_merge_patterns([out_t, other1_ref[0, h, rows, :].T, other2_ref[0, h, rows, :].T]))
            else:
                o_ref[0, h, rows, :] = out_t.T
        if mode != "stats":
            for pair in range(n_heads // 2):
                both = jnp.concatenate([outs[2 * pair][:HEAD_DIM], outs[2 * pair + 1][:HEAD_DIM]], axis=0)
                o_ref[0, rows, pair * LANES:(pair + 1) * LANES] = both.T.astype(o_ref.dtype)
        return 0

    lax.fori_loop(0, n_qt, one_query_tile, 0)


def _band_cases(tile_bias):
    n_tiles = tile_bias.shape[1]
    masked = jnp.full_like(tile_bias[:, 0], MASK_NEG)
    cases = [jnp.concatenate([tile_bias[:, d] for d in range(n_tiles - 1 - c, -1, -1)] + [masked] * c, axis=1)
             for c in range(n_tiles)]
    return jnp.stack(cases, axis=1)


def _band_attn(q_arr, q_off, kv_arr, k_off, v_off, bias, *, n_q_heads, n_heads, group, q_chunk, mode, name,
               others=None):
    B, _, n, _ = q_arr.shape
    n_tiles, tile = bias.shape[1], bias.shape[3]
    n_kv = n_heads // group
    assert n >= n_tiles * tile and n % q_chunk == 0 and q_chunk % tile == 0 and n_q_heads % n_heads == 0
    assert q_off % n_heads == 0 and k_off % n_kv == 0 and v_off % n_kv == 0
    kern = functools.partial(_band_attn_kernel, tile=tile, n_tiles=n_tiles, n_heads=n_heads, group=group,
                             n_qt=q_chunk // tile, mode=mode)
    head_rows = pl.BlockSpec((1, n_heads, q_chunk, LANES), lambda u, b, c: (b, u, c, 0))
    in_specs = [
        pl.BlockSpec((1, n_heads, q_chunk, LANES), lambda u, b, c: (b, q_off // n_heads + u, c, 0)),
        pl.BlockSpec((1, n_kv, n, LANES), lambda u, b, c: (b, k_off // n_kv + u, 0, 0)),
        pl.BlockSpec((1, n_kv, n, LANES), lambda u, b, c: (b, v_off // n_kv + u, 0, 0)),
        pl.BlockSpec((n_heads, n_tiles, n_tiles * tile, tile), lambda u, b, c: (u, 0, 0, 0)),
    ]
    args = [q_arr, kv_arr, kv_arr, bias]
    blocks = [n_heads * q_chunk * LANES * 2, 2 * n_kv * n * LANES * 2, n_heads * n_tiles * n_tiles * tile * tile * 4,
              n_heads * q_chunk * LANES * 4]
    if mode == "merge":
        in_specs += [head_rows, head_rows]
        args += list(others)
        blocks.append(2 * n_heads * q_chunk * LANES * 4)
    if mode == "stats":
        out_specs = head_rows
        out_shape = jax.ShapeDtypeStruct((B, n_q_heads, n, LANES), F32)
    else:
        assert n_heads % 2 == 0
        out_specs = pl.BlockSpec((1, q_chunk, n_heads * HEAD_DIM), lambda u, b, c: (b, c, u))
        out_shape = jax.ShapeDtypeStruct((B, n, n_q_heads * HEAD_DIM), BF16 if mode == "merge" else F32)
    return pl.pallas_call(
        kern,
        grid=(n_q_heads // n_heads, B, n // q_chunk),
        in_specs=in_specs,
        out_specs=out_specs,
        out_shape=out_shape,
        compiler_params=pltpu.CompilerParams(
            dimension_semantics=("parallel", "parallel", "arbitrary"),
            vmem_limit_bytes=_vmem_limit(blocks, n_heads * 8 * n_tiles * tile * tile * 4)),
        name=name,
    )(*args)


def _dilated_mixer(P, rel_table, batch, seq):
    def band(arr, window, dil, n_heads, q_chunk, mode, name, others=None):
        bias = _band_cases(_tile_bias(rel_table, BAND_BLOCK, 2, dil, window // dil))
        return _band_attn(arr, 0, arr, N_HEADS, 2 * N_HEADS, bias, n_q_heads=N_HEADS, n_heads=n_heads, group=1,
                          q_chunk=q_chunk, mode=mode, name=name, others=others)

    strided = []
    for window, dil in DILATED_PATTERNS:
        if dil == 1:
            continue
        n = seq // dil
        Pr = P.reshape(batch, 3 * N_HEADS, n, dil, LANES).transpose(0, 3, 1, 2, 4)
        Pr = Pr.reshape(batch * dil, 3 * N_HEADS, n, LANES)
        o = band(Pr, window, dil, N_HEADS, min(n, 512), "stats", f"dilated_attn_d{dil}")
        strided.append(o.reshape(batch, dil, N_HEADS, n, LANES).transpose(0, 2, 3, 1, 4)
                       .reshape(batch, N_HEADS, seq, LANES))
    (window, dil), = [pat for pat in DILATED_PATTERNS if pat[1] == 1]
    return band(P, window, dil, N_HEADS // 2, 2 * BAND_BLOCK, "merge", "dilated_attn_merge", others=strided)


def _compress_kernel(x_ref, w1_ref, w2_ref, pos_ref, g_ref, o_ref):
    posw = jnp.dot(pos_ref[0], w1_ref[0], preferred_element_type=F32)[0:1]
    pre = jnp.dot(x_ref[0, 0], w1_ref[0], preferred_element_type=F32) + posw
    hid = 0.5 * pre * (1.0 + jnp.tanh(math.sqrt(2.0 / math.pi) * (pre + 0.044715 * (pre * pre * pre))))
    out = jnp.dot(hid.astype(BF16), w2_ref[0], preferred_element_type=F32)
    ms = jnp.sum(out * out, axis=-1, keepdims=True) * (1.0 / HEAD_DIM)
    normed = out * lax.rsqrt(ms + RMS_EPS) * g_ref[...]
    o_ref[0, 0] = jnp.where(pl.program_id(0) == 0, normed, out).astype(o_ref.dtype)


def _compress(blocks, w1, w2, pos, k_gain_pad):
    _, BK, n, K = blocks.shape
    return pl.pallas_call(
        _compress_kernel,
        grid=(2, BK),
        in_specs=[
            pl.BlockSpec((1, 1, n, K), lambda a, i: (a, i, 0, 0)),
            pl.BlockSpec((1, K, CMP_HIDDEN), lambda a, i: (a, 0, 0)),
            pl.BlockSpec((1, CMP_HIDDEN, LANES), lambda a, i: (a, 0, 0)),
            pl.BlockSpec((1, 8, K), lambda a, i: (a, 0, 0)),
            pl.BlockSpec((1, LANES), lambda a, i: (0, 0)),
        ],
        out_specs=pl.BlockSpec((1, 1, n, LANES), lambda a, i: (a, i, 0, 0)),
        out_shape=jax.ShapeDtypeStruct((2, BK, n, LANES), BF16),
        compiler_params=pltpu.CompilerParams(
            dimension_semantics=("parallel", "parallel"),
            vmem_limit_bytes=_vmem_limit([n * K * 2, K * CMP_HIDDEN * 2, CMP_HIDDEN * LANES * 2, 8 * K * 2,
                                          n * LANES * 2], 8 * n * CMP_HIDDEN * 4)),
        name="nsa_compress",
    )(blocks, w1, w2, pos, k_gain_pad)


def _rank_select(vals, own_row, n_blocks, top):
    jidx = lax.broadcasted_iota(jnp.int32, vals.shape, 0)
    cnt = jnp.zeros(vals.shape, F32)
    for j in range(n_blocks):
        row = vals[j:j + 1, :]
        ahead = (row > vals) | ((row == vals) & (jidx > j))
        cnt = cnt + jnp.where(ahead, 1.0, 0.0)
    return (jidx < own_row) & (cnt < top)


def _cmp_attn_kernel(q_ref, k_ref, vt_ref, b_ref, ov_ref, o_ref, qa_ref, *, tq):
    qi = pl.program_id(1)
    k = k_ref[0, 0]
    vt = vt_ref[0]
    scores = [lax.dot_general(k, q_ref[0, g], _NT, preferred_element_type=F32) + b_ref[g]
              for g in range(NSA_GROUP)]
    psum = None
    all_probs = []
    for s in scores:
        m = jnp.max(s, axis=0, keepdims=True)
        p = jnp.where(m > 0.5 * MASK_NEG, jnp.exp2(s - m), 0.0)
        l = jnp.sum(p, axis=0, keepdims=True)
        probs = p / jnp.maximum(l, TINY)
        all_probs.append(probs.astype(BF16))
        psum = probs if psum is None else psum + probs
    outs = [jnp.dot(vt, p, preferred_element_type=F32)[:HEAD_DIM] for p in all_probs]
    for pair in range(NSA_GROUP // 2):
        both = jnp.concatenate([outs[2 * pair], outs[2 * pair + 1]], axis=0)
        o_ref[0, :, pair * LANES:(pair + 1) * LANES] = both.T.astype(o_ref.dtype)
    imp = None
    for part in _split3(psum):
        term = jnp.dot(ov_ref[...], part, preferred_element_type=F32)
        imp = term if imp is None else imp + term
    n_sel = imp.shape[0]
    pos = qi * tq + lax.broadcasted_iota(jnp.int32, (1, tq), 1)
    own = pos // SEL_BLOCK
    jidx = lax.broadcasted_iota(jnp.int32, imp.shape, 0)
    vals = jnp.where(jidx < own, imp, -1.0)
    chosen = _rank_select(vals, own, n_sel, SEL_TOPK - 1) | (jidx == own)
    sel_cols = jnp.concatenate([jnp.zeros((HEAD_DIM, tq), F32), jnp.where(chosen, 0.0, SEL_NEG)], axis=0).T
    for g in range(NSA_GROUP):
        qa_ref[0, g] = (q_ref[0, g].astype(F32) + sel_cols).astype(qa_ref.dtype)


def _cmp_attn(P, kv_cmp, vt_cmp, bias_c, overlap_t, batch, seq):
    assert seq // SEL_BLOCK == LANES - HEAD_DIM
    tq = 256
    n_cmp = kv_cmp.shape[2]
    n_sel = overlap_t.shape[0]
    kern = functools.partial(_cmp_attn_kernel, tq=tq)
    return pl.pallas_call(
        kern,
        grid=(NSA_KV_HEADS, seq // tq, batch),
        in_specs=[
            pl.BlockSpec((1, NSA_GROUP, tq, LANES), lambda kh, qi, b: (b, kh, qi, 0)),
            pl.BlockSpec((1, 1, n_cmp, LANES), lambda kh, qi, b: (0, b * NSA_KV_HEADS + kh, 0, 0)),
            pl.BlockSpec((1, LANES, n_cmp), lambda kh, qi, b: (b * NSA_KV_HEADS + kh, 0, 0)),
            pl.BlockSpec((NSA_GROUP, n_cmp, tq), lambda kh, qi, b: (kh, 0, qi)),
            pl.BlockSpec((n_sel, n_cmp), lambda kh, qi, b: (0, 0)),
        ],
        out_specs=[
            pl.BlockSpec((1, tq, NSA_GROUP * HEAD_DIM), lambda kh, qi, b: (b, qi, kh)),
            pl.BlockSpec((1, NSA_GROUP, tq, LANES), lambda kh, qi, b: (b, kh, qi, 0)),
        ],
        out_shape=[
            jax.ShapeDtypeStruct((batch, seq, N_HEADS * HEAD_DIM), F32),
            jax.ShapeDtypeStruct((batch, N_HEADS, seq, LANES), BF16),
        ],
        compiler_params=pltpu.CompilerParams(
            dimension_semantics=("parallel", "parallel", "arbitrary"),
            vmem_limit_bytes=_vmem_limit([NSA_GROUP * tq * LANES * 2, 2 * n_cmp * LANES * 2,
                                          NSA_GROUP * tq * n_cmp * 4, NSA_GROUP * tq * LANES * 4, n_sel * tq * 2],
                                         16 * tq * n_cmp * 4)),
        name="nsa_cmp_attn",
    )(P, kv_cmp, vt_cmp, bias_c, overlap_t)


def _nsa_merge_kernel(c_ref, s_ref, w_ref, g_ref, o_ref):
    gates = jax.nn.sigmoid(g_ref[0])
    lane = lax.broadcasted_iota(jnp.int32, gates.shape, 1)
    for pair in range(N_HEADS // 2):
        cols = slice(pair * LANES, (pair + 1) * LANES)

        def gate(branch):
            a, b = 3 * (2 * pair) + branch, 3 * (2 * pair + 1) + branch
            return jnp.where(lane < HEAD_DIM, gates[:, a:a + 1], gates[:, b:b + 1])

        o = gate(0) * c_ref[0, :, cols] + gate(1) * s_ref[0, :, cols] + gate(2) * w_ref[0, :, cols]
        o_ref[0, :, cols] = o.astype(o_ref.dtype)


def _nsa_merge(o_cmp, o_sel, o_win, gate_logits, batch, seq):
    ts = 256
    spec = pl.BlockSpec((1, ts, D_MODEL), lambda b, s: (b, s, 0))
    return pl.pallas_call(
        _nsa_merge_kernel,
        grid=(batch, seq // ts),
        in_specs=[spec, spec, spec, pl.BlockSpec((1, ts, LANES), lambda b, s: (b, s, 0))],
        out_specs=spec,
        out_shape=jax.ShapeDtypeStruct((batch, seq, D_MODEL), BF16),
        compiler_params=pltpu.CompilerParams(
            dimension_semantics=("parallel", "parallel"),
            vmem_limit_bytes=_vmem_limit([3 * ts * D_MODEL * 4, ts * LANES * 4, ts * D_MODEL * 2],
                                         8 * ts * LANES * 4)),
        name="nsa_merge",
    )(o_cmp, o_sel, o_win, gate_logits.reshape(batch, seq, LANES))


def _nsa_mixer(P, vt, gate_logits, cmp_pos, cmp_w1, cmp_w2, k_gain, bias_sel, rel_table, batch, seq):
    KH = NSA_KV_HEADS
    n_chunk = seq // CMP_STRIDE
    n_sel = seq // SEL_BLOCK

    raw = P[:, N_HEADS:N_HEADS + 2 * KH, :, :HEAD_DIM].reshape(batch, 2, KH, n_chunk, CMP_STRIDE * HEAD_DIM)
    raw = raw.transpose(1, 0, 2, 3, 4).reshape(2, batch * KH, n_chunk, CMP_STRIDE * HEAD_DIM)
    blocks = jnp.concatenate([raw, jnp.roll(raw, -1, axis=2)], axis=-1)
    pos = jnp.broadcast_to(cmp_pos.reshape(2, 1, CMP_BLOCK * HEAD_DIM), (2, 8, CMP_BLOCK * HEAD_DIM)).astype(BF16)
    w2 = jnp.pad(cmp_w2, ((0, 0), (0, 0), (0, LANES - HEAD_DIM))).astype(BF16)
    k_gain_pad = jnp.pad(k_gain, (0, LANES - HEAD_DIM)).reshape(1, LANES)
    kv_cmp = _compress(blocks, cmp_w1.astype(BF16), w2, pos, k_gain_pad)

    d = np.arange(2 * seq) - (seq + CMP_BLOCK - 1)
    w = _dist_values(rel_table, d, d >= 0)
    bias_c = _skew(w, n_chunk, CMP_STRIDE, seq, 1, seq, mask_last_row=True)[:, 0]
    c_start = np.arange(n_chunk)[None, :] * CMP_STRIDE
    s_start = np.arange(n_sel)[:, None] * SEL_BLOCK
    overlap_t = ((c_start < s_start + SEL_BLOCK) & (c_start + CMP_BLOCK > s_start)
                 & (np.arange(n_chunk)[None, :] < n_chunk - 1))
    overlap_t = jnp.asarray(overlap_t.astype(np.float32), BF16)
    o_cmp, q_aug = _cmp_attn(P, kv_cmp, jnp.swapaxes(kv_cmp[1], -1, -2), bias_c, overlap_t, batch, seq)

    lane = jnp.arange(LANES)
    onehot = (np.arange(seq)[:, None] // SEL_BLOCK == np.arange(LANES - HEAD_DIM)[None, :]).astype(np.float32)
    onehot = jnp.asarray(np.pad(onehot, ((0, 0), (HEAD_DIM, 0))), BF16)
    k_aug = jnp.where(lane < HEAD_DIM, P[:, 24:28], onehot[None, None])
    o_sel = _dense_attn(q_aug, 0, k_aug, 0, vt, 0, bias_sel, n_kv_heads=KH,
                        group=NSA_GROUP, tile=DENSE_TILE, n_kv=1, q_chunk=DENSE_TILE, n_back=None,
                        diag_causal=False, out_dtype=F32, name="nsa_sel_attn")
    n_win_tiles = -(-(NSA_WINDOW - 1) // DENSE_TILE) + 1
    bias_win = _band_cases(_tile_bias(rel_table, DENSE_TILE, n_win_tiles, 1, NSA_WINDOW - 1))
    o_win = _band_attn(q_aug, 0, P, 28, 32, bias_win, n_q_heads=N_HEADS, n_heads=NSA_GROUP, group=NSA_GROUP,
                       q_chunk=2 * DENSE_TILE, mode="norm", name="nsa_win_attn")
    return _nsa_merge(o_cmp, o_sel, o_win, gate_logits, batch, seq)


def _fox_decay_kernel(f_ref, bf_ref, tri_ref, q_ref, k_ref, eq_ref, ek_ref, carry):
    @pl.when(pl.program_id(1) == 0)
    def _():
        carry[...] = jnp.zeros_like(carry)

    z = f_ref[0] + bf_ref[...]
    log_f = jnp.minimum(z, 0.0) - jnp.log1p(jnp.exp(-jnp.abs(z)))
    c = carry[0:1, :]
    for part in _split3(log_f):
        c = c + jnp.dot(tri_ref[...], part, preferred_element_type=F32)
    ts = c.shape[0]
    carry[...] = jnp.broadcast_to(c[ts - 1:ts, :], carry.shape)
    lane = lax.broadcasted_iota(jnp.int32, (ts, LANES), 1) - HEAD_DIM
    c = c * LOG2E
    for h in range(N_HEADS):
        c1, c2, c3 = _split3(c[:, h:h + 1])
        c1, c2, c3 = c1.astype(F32), c2.astype(F32), c3.astype(F32)
        eq = jnp.where(lane == 0, c1, jnp.where(lane == 1, c2, jnp.where(lane == 2, c3,
                       jnp.where((lane >= 3) & (lane < 6), 1.0, 0.0))))
        ek = jnp.where(lane == 3, -c1, jnp.where(lane == 4, -c2, jnp.where(lane == 5, -c3,
                       jnp.where((lane >= 0) & (lane < 3), 1.0, 0.0))))
        eq_ref[0, h] = (q_ref[0, h].astype(F32) + eq).astype(eq_ref.dtype)
        ek_ref[0, h] = (k_ref[0, h].astype(F32) + ek).astype(ek_ref.dtype)


def _fox_decay(P, f_logits, b_f, batch, seq):
    ts = 256
    tri = jnp.asarray(np.tril(np.ones((ts, ts), np.float32)), BF16)
    spec = pl.BlockSpec((1, N_HEADS, ts, LANES), lambda b, s: (b, 0, s, 0))
    return pl.pallas_call(
        _fox_decay_kernel,
        grid=(batch, seq // ts),
        in_specs=[
            pl.BlockSpec((1, ts, LANES), lambda b, s: (b, s, 0)),
            pl.BlockSpec((1, LANES), lambda b, s: (0, 0)),
            pl.BlockSpec((ts, ts), lambda b, s: (0, 0)),
            spec,
            pl.BlockSpec((1, N_HEADS, ts, LANES), lambda b, s: (b, 1, s, 0)),
        ],
        out_specs=[spec, spec],
        out_shape=[jax.ShapeDtypeStruct((batch, N_HEADS, seq, LANES), BF16)] * 2,
        scratch_shapes=[pltpu.VMEM((8, LANES), F32)],
        compiler_params=pltpu.CompilerParams(
            dimension_semantics=("parallel", "arbitrary"),
            vmem_limit_bytes=_vmem_limit([ts * LANES * 4, ts * ts * 2, 4 * N_HEADS * ts * LANES * 2],
                                         16 * ts * LANES * 4)),
        name="fox_decay",
    )(f_logits.reshape(batch, seq, LANES), jnp.pad(b_f, (0, LANES - N_HEADS)).reshape(1, LANES), tri, P, P)


def _fox_mixer(P, vt, f_logits, b_f, batch, seq):
    q_aug, k_aug = _fox_decay(P, f_logits, b_f, batch, seq)
    return _dense_attn(q_aug, 0, k_aug, 0, vt, 0, None, n_kv_heads=N_HEADS, group=1, tile=DENSE_TILE,
                       n_kv=4, q_chunk=2 * DENSE_TILE, n_back=None, diag_causal=True,
                       out_dtype=BF16, name="fox_attn")


def _moba_gate_kernel(q_ref, k_ref, qa_ref, *, n_blk):
    k = k_ref[0, 0].astype(F32)
    S = k.shape[0]
    k_mean = jnp.sum(k.reshape(n_blk, S // n_blk, LANES), axis=1) * (1.0 / (S // n_blk))
    hi = k_mean.astype(BF16)
    lo = (k_mean - hi.astype(F32)).astype(BF16)
    q = q_ref[0, 0]
    gate = (lax.dot_general(hi, q, _NT, preferred_element_type=F32)
            + lax.dot_general(lo, q, _NT, preferred_element_type=F32))
    own = lax.broadcasted_iota(jnp.int32, (1, S), 1) // (S // n_blk)
    jidx = lax.broadcasted_iota(jnp.int32, gate.shape, 0)
    vals = jnp.where(jidx < own, gate, -jnp.inf)
    chosen = _rank_select(vals, own, n_blk, MOBA_TOPK) | (jidx == own)
    sel_cols = jnp.concatenate([jnp.zeros((HEAD_DIM, S), F32), jnp.where(chosen, 0.0, SEL_NEG),
                                jnp.zeros((LANES - HEAD_DIM - n_blk, S), F32)], axis=0).T
    qa_ref[0, 0] = (q.astype(F32) + sel_cols).astype(qa_ref.dtype)


def _moba_gate(P, batch, seq):
    n_blk = seq // MOBA_BLOCK
    return pl.pallas_call(
        functools.partial(_moba_gate_kernel, n_blk=n_blk),
        grid=(batch, N_HEADS),
        in_specs=[
            pl.BlockSpec((1, 1, seq, LANES), lambda b, h: (b, h, 0, 0)),
            pl.BlockSpec((1, 1, seq, LANES), lambda b, h: (b, N_HEADS + h, 0, 0)),
        ],
        out_specs=pl.BlockSpec((1, 1, seq, LANES), lambda b, h: (b, h, 0, 0)),
        out_shape=jax.ShapeDtypeStruct((batch, N_HEADS, seq, LANES), BF16),
        compiler_params=pltpu.CompilerParams(
            dimension_semantics=("parallel", "parallel"),
            vmem_limit_bytes=_vmem_limit([3 * seq * LANES * 2], 5 * seq * LANES * 4 + 24 * n_blk * seq * 4)),
        name="moba_gate",
    )(P, P)


def _moba_mixer(P, vt, bias, batch, seq):
    q_aug = _moba_gate(P, batch, seq)
    lane = jnp.arange(LANES)
    onehot = (np.arange(seq)[:, None] // MOBA_BLOCK == np.arange(LANES - HEAD_DIM)[None, :]).astype(np.float32)
    onehot = jnp.asarray(np.pad(onehot, ((0, 0), (HEAD_DIM, 0))), BF16)
    k_aug = jnp.where(lane < HEAD_DIM, P[:, N_HEADS:2 * N_HEADS], onehot[None, None])
    return _dense_attn(q_aug, 0, k_aug, 0, vt, 0, bias, n_kv_heads=N_HEADS, group=1, tile=DENSE_TILE,
                       n_kv=4, q_chunk=2 * DENSE_TILE, n_back=None, diag_causal=False,
                       out_dtype=BF16, name="moba_attn")


def _column_scales(segments):
    gain = [jnp.tile(jnp.ones((HEAD_DIM,), F32) if g is None else g, n) for n, g in segments]
    flag = [jnp.full((n * HEAD_DIM,), 0.0 if g is None else 1.0, F32) for n, g in segments]
    return jnp.concatenate(gain).reshape(1, -1), jnp.concatenate(flag).reshape(1, -1)


def _qkv_scales(q_gain, k_gain):
    return _column_scales([(N_HEADS, q_gain * QK_SCALE), (N_HEADS, k_gain), (N_HEADS, None)])


def _dilated_layer(x2, norm_g, w, qg, kg, rel_table, batch, seq):
    gain, flag = _qkv_scales(qg, kg)
    P = _proj(x2, norm_g, w.astype(BF16), gain, flag, None, batch, seq)
    return _dilated_mixer(P, rel_table, batch, seq)


def _nsa_layer(x2, norm_g, w, cmp_pos, cmp_w1, cmp_w2, qg, kg, bias_sel, rel_table, batch, seq):
    KH = NSA_KV_HEADS
    kvd = KH * HEAD_DIM
    seg = [w[:, D_MODEL + i * kvd:D_MODEL + (i + 1) * kvd] for i in range(6)]
    gain, flag = _column_scales([(N_HEADS, qg * QK_SCALE), (KH, None), (KH, None), (KH, kg), (KH, kg), (KH, None)])
    w_main = jnp.concatenate([w[:, :D_MODEL], seg[0], seg[1], seg[2], seg[4], seg[5]], axis=1).astype(BF16)
    w_aux = jnp.pad(w[:, D_MODEL + 6 * kvd:], ((0, 0), (0, LANES - 3 * N_HEADS))).astype(BF16)
    P, gate_logits = _proj(x2, norm_g, w_main, gain, flag, w_aux, batch, seq)
    vt = _proj_vt(x2, norm_g, seg[3].astype(BF16), batch, seq, DENSE_TILE)
    return _nsa_mixer(P, vt, gate_logits, cmp_pos, cmp_w1, cmp_w2, kg, bias_sel, rel_table, batch, seq)


def _qk_scales(q_gain, k_gain):
    return _column_scales([(N_HEADS, q_gain * QK_SCALE), (N_HEADS, k_gain)])


def _fox_layer(x2, norm_g, w, b_f, qg, kg, batch, seq):
    D = D_MODEL
    gain, flag = _qk_scales(qg, kg)
    w_aux = jnp.pad(w[:, 3 * D:], ((0, 0), (0, LANES - N_HEADS))).astype(BF16)
    P, f_logits = _proj(x2, norm_g, w[:, :2 * D].astype(BF16), gain, flag, w_aux, batch, seq)
    vt = _proj_vt(x2, norm_g, w[:, 2 * D:3 * D].astype(BF16), batch, seq, DENSE_TILE)
    return _fox_mixer(P, vt, f_logits, b_f, batch, seq)


def _moba_layer(x2, norm_g, w, qg, kg, bias, batch, seq):
    D = D_MODEL
    gain, flag = _qk_scales(qg, kg)
    P = _proj(x2, norm_g, w[:, :2 * D].astype(BF16), gain, flag, None, batch, seq)
    vt = _proj_vt(x2, norm_g, w[:, 2 * D:].astype(BF16), batch, seq, DENSE_TILE)
    return _moba_mixer(P, vt, bias, batch, seq)


def kernel(x, rel_table, attn_norm, mlp_norm, q_gain, k_gain, w_out, mlp_w_up, mlp_w_down, dsa_w_in, nsa_w_in,
           nsa_cmp_pos, nsa_cmp_w1, nsa_cmp_w2, fox_w_in, fox_b_f, moba_w_in):
    batch, seq, _ = x.shape
    depth = attn_norm.shape[0]
    x2 = x.reshape(batch * seq, D_MODEL)
    bias_dense = _tile_bias(rel_table, DENSE_TILE, 10, 1, None)
    for layer in range(depth):
        kind, r = layer % 4, layer // 4
        qg, kg, ng = q_gain[layer], k_gain[layer], attn_norm[layer]
        if kind == 0:
            mixed = _dilated_layer(x2, ng, dsa_w_in[r], qg, kg, rel_table, batch, seq)
        elif kind == 1:
            mixed = _nsa_layer(x2, ng, nsa_w_in[r], nsa_cmp_pos[r], nsa_cmp_w1[r], nsa_cmp_w2[r], qg, kg,
                               bias_dense, rel_table, batch, seq)
        elif kind == 2:
            mixed = _fox_layer(x2, ng, fox_w_in[r], fox_b_f[r], qg, kg, batch, seq)
        else:
            mixed = _moba_layer(x2, ng, moba_w_in[r], qg, kg, bias_dense, batch, seq)
        x2 = _out_mlp(x2, mixed.reshape(batch * seq, D_MODEL), w_out[layer].astype(BF16), mlp_norm[layer],
                      mlp_w_up[layer].astype(BF16), mlp_w_down[layer].astype(BF16))
    return x2.reshape(batch, seq, D_MODEL)
```
